```python
import math
import jax, jax.numpy as jnp
from jax import lax
import numpy as np

D_MODEL = 1024
BATCH = 2
SEQ = 8192
DEPTH = 4

HEAD_DIM = 64
ROT_DIM = HEAD_DIM // 4
ROPE_THETA = 500000.0
NORM_EPS = 1e-6
N_MIXERS = 3
DILATED_GROUPS = ((128, 1), (512, 4), (2048, 16))
A_HEADS = 16
A_BLOCK = 64
B_HEADS = D_MODEL // (2 * HEAD_DIM)
B_QUERY_BLOCK = 128
SUBLN_EPS = 1e-5
C_Q_HEADS = 16
C_KV_HEADS = 2
C_HALF_WINDOW = 128
C_BLOCK = 128
D_FF = 2816
N_EXPERTS = 8
TOP_K = 2
D_FF_EXPERT = 3584
MOE_BLOCK = 512

A_W_IN = len(DILATED_GROUPS) * 3 * A_HEADS * HEAD_DIM
B_W_IN = 3 * B_HEADS * 2 * HEAD_DIM
C_W_IN = (C_Q_HEADS + 2 * C_KV_HEADS) * HEAD_DIM

kernel_name = "hybrid_dilated_diff_swa_moe_encoder"


def rmsnorm(x, g, eps=NORM_EPS):
    xf = x.astype(jnp.float32)
    y = xf * lax.rsqrt(jnp.mean(xf * xf, axis=-1, keepdims=True) + eps) * g.astype(jnp.float32)
    return y.astype(x.dtype)


def rope_tables(positions):
    inv_freq = ROPE_THETA ** (-(jnp.arange(0, ROT_DIM, 2, dtype=jnp.float32) / ROT_DIM))
    ang = positions.astype(jnp.float32)[..., None] * inv_freq
    return jnp.cos(ang)[:, :, None, :], jnp.sin(ang)[:, :, None, :]


def rope(t, cos, sin):
    half = ROT_DIM // 2
    rot = t[..., :ROT_DIM].astype(jnp.float32)
    x1, x2 = rot[..., :half], rot[..., half:]
    rotated = jnp.concatenate([x1 * cos - x2 * sin, x2 * cos + x1 * sin], axis=-1)
    return jnp.concatenate([rotated.astype(t.dtype), t[..., ROT_DIM:]], axis=-1)


def banded_attention(q, k, v, half_window, blk, sink=None):
    B, L, H, Dh = q.shape
    G = k.shape[2]
    rep = H // G
    nb = -(-L // blk)
    pad = nb * blk - L
    qb = jnp.pad(q, ((0, 0), (0, pad), (0, 0), (0, 0))).reshape(B, nb, blk, G, rep, Dh)

    def windows(t):
        tp = jnp.pad(t, ((0, 0), (blk, pad + blk), (0, 0), (0, 0))).reshape(B, nb + 2, blk, G, Dh)
        return jnp.concatenate([tp[:, :-2], tp[:, 1:-1], tp[:, 2:]], axis=2)

    kw, vw = windows(k), windows(v)
    q_pos = jnp.arange(nb * blk).reshape(nb, blk)
    k_pos = (jnp.arange(nb)[:, None] - 1) * blk + jnp.arange(3 * blk)[None, :]
    kp = k_pos[:, None, :]
    mask = (jnp.abs(q_pos[:, :, None] - kp) <= half_window) & (kp >= 0) & (kp < L)
    s = jnp.einsum('bnqgrd,bnkgd->bngrqk', qb, kw).astype(jnp.float32) * (Dh ** -0.5)
    s = jnp.where(mask[None, :, None, None], s, -jnp.inf)
    m = jnp.max(s, axis=-1, keepdims=True)
    if sink is not None:
        sk = sink.astype(jnp.float32).reshape(1, 1, G, rep, 1, 1)
        m = jnp.maximum(m, sk)
    m = jnp.where(jnp.isfinite(m), m, 0.0)
    p = jnp.exp(s - m)
    denom = jnp.sum(p, axis=-1, keepdims=True)
    if sink is not None:
        denom = denom + jnp.exp(sk - m)
    o = jnp.einsum('bngrqk,bnkgd->bnqgrd', (p / denom).astype(v.dtype), vw)
    o = o.reshape(B, nb * blk, H, Dh)[:, :L]
    lse = (m + jnp.log(denom))[..., 0]
    lse = lse.transpose(0, 1, 4, 2, 3).reshape(B, nb * blk, H)[:, :L]
    return o, lse


def to_strided(t, d):
    B, S = t.shape[:2]
    rest = t.shape[2:]
    return jnp.swapaxes(t.reshape((B, S // d, d) + rest), 1, 2).reshape((B * d, S // d) + rest)


def from_strided(t, d, B):
    L = t.shape[1]
    rest = t.shape[2:]
    return jnp.swapaxes(t.reshape((B, d, L) + rest), 1, 2).reshape((B, L * d) + rest)


def dilated_mixer(h, cos, sin, w_in, w_out):
    B, S, _ = h.shape
    proj = (h @ w_in).reshape(B, S, len(DILATED_GROUPS), 3, A_HEADS, HEAD_DIM)
    outs, lses = [], []
    for gi, (window, dil) in enumerate(DILATED_GROUPS):
        q = rope(proj[:, :, gi, 0], cos, sin)
        k = rope(proj[:, :, gi, 1], cos, sin)
        v = proj[:, :, gi, 2]
        half = (window // 2) // dil
        o, l = banded_attention(to_strided(q, dil), to_strided(k, dil), to_strided(v, dil), half, A_BLOCK)
        outs.append(from_strided(o, dil, B))
        lses.append(from_strided(l, dil, B))
    alpha = jax.nn.softmax(jnp.stack(lses, axis=0), axis=0)
    o = jnp.einsum('gbsh,gbshd->bshd', alpha, jnp.stack(outs, axis=0).astype(jnp.float32))
    return o.astype(h.dtype).reshape(B, S, A_HEADS * HEAD_DIM) @ w_out


def diff_mixer(h, cos, sin, w_in, lq1, lk1, lq2, lk2, subln, w_out, lambda_init):
    B, S, _ = h.shape
    q, k, v = jnp.split(h @ w_in, 3, axis=-1)
    q = rope(q.reshape(B, S, 2 * B_HEADS, HEAD_DIM), cos, sin).reshape(B, S, B_HEADS, 2, HEAD_DIM)
    k = rope(k.reshape(B, S, 2 * B_HEADS, HEAD_DIM), cos, sin).reshape(B, S, B_HEADS, 2, HEAD_DIM)
    v = v.reshape(B, S, B_HEADS, 2 * HEAD_DIM)
    lam = (jnp.exp(jnp.sum(lq1.astype(jnp.float32) * lk1.astype(jnp.float32)))
           - jnp.exp(jnp.sum(lq2.astype(jnp.float32) * lk2.astype(jnp.float32))) + lambda_init)
    kh = k.transpose(0, 2, 3, 1, 4)
    vh = v.transpose(0, 2, 1, 3)
    nqb = S // B_QUERY_BLOCK
    qb = q.reshape(B, nqb, B_QUERY_BLOCK, B_HEADS, 2, HEAD_DIM).transpose(1, 0, 3, 4, 2, 5)
    scale = HEAD_DIM ** -0.5

    def block(qblk):
        s = jnp.einsum('bhcqd,bhckd->bhcqk', qblk, kh).astype(jnp.float32) * scale
        a = jax.nn.softmax(s, axis=-1)
        attn = a[:, :, 0] - lam * a[:, :, 1]
        return jnp.einsum('bhqk,bhkd->bhqd', attn.astype(vh.dtype), vh)

    o = lax.map(block, qb)
    o = o.transpose(1, 0, 3, 2, 4).reshape(B, S, B_HEADS, 2 * HEAD_DIM).astype(jnp.float32)
    o = o * lax.rsqrt(jnp.mean(o * o, axis=-1, keepdims=True) + SUBLN_EPS) * subln.astype(jnp.float32)
    o = o * (1.0 - lambda_init)
    return o.astype(h.dtype).reshape(B, S, B_HEADS * 2 * HEAD_DIM) @ w_out


def swa_sink_mixer(h, cos, sin, w_in, sink, w_out):
    B, S, _ = h.shape
    proj = h @ w_in
    nq, nk = C_Q_HEADS * HEAD_DIM, C_KV_HEADS * HEAD_DIM
    q = rope(proj[..., :nq].reshape(B, S, C_Q_HEADS, HEAD_DIM), cos, sin)
    k = rope(proj[..., nq:nq + nk].reshape(B, S, C_KV_HEADS, HEAD_DIM), cos, sin)
    v = proj[..., nq + nk:].reshape(B, S, C_KV_HEADS, HEAD_DIM)
    o, _ = banded_attention(q, k, v, C_HALF_WINDOW, C_BLOCK, sink)
    return o.reshape(B, S, C_Q_HEADS * HEAD_DIM) @ w_out


def dense_swiglu(h, w1, w3, w2):
    return (jax.nn.silu(h @ w1) * (h @ w3)) @ w2


def moe_swiglu(h, router, w1, w3, w2):
    B, S, D = h.shape
    T = B * S
    xt = h.reshape(T, D)
    logits = (xt @ router).astype(jnp.float32)
    top_vals, top_idx = lax.top_k(logits, TOP_K)
    gates = jax.nn.softmax(top_vals, axis=-1)
    n = T * TOP_K
    expert = top_idx.reshape(n)
    token = jnp.repeat(jnp.arange(T, dtype=jnp.int32), TOP_K)
    gate = gates.reshape(n)
    order = jnp.argsort(expert)
    e_sorted, t_sorted, g_sorted = expert[order], token[order], gate[order]
    counts = jnp.zeros((N_EXPERTS,), jnp.int32).at[expert].add(1)
    padded = ((counts + MOE_BLOCK - 1) // MOE_BLOCK) * MOE_BLOCK
    start = jnp.cumsum(counts) - counts
    pend = jnp.cumsum(padded)
    pstart = pend - padded
    dest = pstart[e_sorted] + (jnp.arange(n, dtype=jnp.int32) - start[e_sorted])
    n_blocks = -(-(n + N_EXPERTS * (MOE_BLOCK - 1)) // MOE_BLOCK)
    buf = jnp.zeros((n_blocks * MOE_BLOCK, D), h.dtype).at[dest].set(xt[t_sorted])
    block_start = jnp.arange(n_blocks, dtype=jnp.int32) * MOE_BLOCK
    block_expert = jnp.minimum(jnp.searchsorted(pend, block_start, side='right'), N_EXPERTS - 1)

    def expert_block(args):
        xb, e = args
        return (jax.nn.silu(xb @ w1[e]) * (xb @ w3[e])) @ w2[e]

    ybuf = lax.map(expert_block, (buf.reshape(n_blocks, MOE_BLOCK, D), block_expert)).reshape(-1, D)
    y = jnp.zeros((T, D), h.dtype).at[t_sorted].add(ybuf[dest] * g_sorted[:, None].astype(h.dtype))
    return y.reshape(B, S, D)


def setup_inputs(seed: int = 0) -> dict:
    key = jax.random.key(seed)
    keys = jax.random.split(key, 48)
    ctr = [0]

    def nk():
        k = keys[ctr[0]]
        ctr[0] += 1
        return k

    def w(shape, fan_in, scale=1.0):
        return jax.random.normal(nk(), shape, jnp.float32) * (scale * fan_in ** -0.5)

    def gain(n=D_MODEL):
        return 1.0 + 0.02 * jax.random.normal(nk(), (n,), jnp.float32)

    inp = {}
    inp['x'] = jax.random.normal(nk(), (BATCH, SEQ, D_MODEL), jnp.float32)
    start = jax.random.randint(nk(), (BATCH, 1), 0, 4096, dtype=jnp.int32)
    inp['positions'] = start + jnp.arange(SEQ, dtype=jnp.int32)[None, :]
    inp['l0_norm_mix'] = gain()
    inp['l0_a_w_in'] = w((D_MODEL, A_W_IN), D_MODEL)
    inp['l0_a_w_out'] = w((A_HEADS * HEAD_DIM, D_MODEL), A_HEADS * HEAD_DIM)
    inp['l0_norm_ffn'] = gain()
    inp['l0_ffn_w1'] = w((D_MODEL, D_FF), D_MODEL)
    inp['l0_ffn_w3'] = w((D_MODEL, D_FF), D_MODEL)
    inp['l0_ffn_w2'] = w((D_FF, D_MODEL), D_FF)
    inp['l1_norm_mix'] = gain()
    inp['l1_b_w_in'] = w((D_MODEL, B_W_IN), D_MODEL)
    inp['l1_b_lambda_q1'] = 0.1 * jax.random.normal(nk(), (HEAD_DIM,), jnp.float32)
    inp['l1_b_lambda_k1'] = 0.1 * jax.random.normal(nk(), (HEAD_DIM,), jnp.float32)
    inp['l1_b_lambda_q2'] = 0.1 * jax.random.normal(nk(), (HEAD_DIM,), jnp.float32)
    inp['l1_b_lambda_k2'] = 0.1 * jax.random.normal(nk(), (HEAD_DIM,), jnp.float32)
    inp['l1_b_subln'] = gain(2 * HEAD_DIM)
    inp['l1_b_w_out'] = w((B_HEADS * 2 * HEAD_DIM, D_MODEL), B_HEADS * 2 * HEAD_DIM)
    inp['l1_norm_ffn'] = gain()
    inp['l1_moe_router'] = w((D_MODEL, N_EXPERTS), D_MODEL)
    inp['l1_moe_w1'] = w((N_EXPERTS, D_MODEL, D_FF_EXPERT), D_MODEL)
    inp['l1_moe_w3'] = w((N_EXPERTS, D_MODEL, D_FF_EXPERT), D_MODEL)
    inp['l1_moe_w2'] = w((N_EXPERTS, D_FF_EXPERT, D_MODEL), D_FF_EXPERT)
    inp['l2_norm_mix'] = gain()
    inp['l2_c_w_in'] = w((D_MODEL, C_W_IN), D_MODEL)
    inp['l2_c_sink'] = 0.5 * jax.random.normal(nk(), (C_Q_HEADS,), jnp.float32)
    inp['l2_c_w_out'] = w((C_Q_HEADS * HEAD_DIM, D_MODEL), C_Q_HEADS * HEAD_DIM)
    inp['l2_norm_ffn'] = gain()
    inp['l2_ffn_w1'] = w((D_MODEL, D_FF), D_MODEL)
    inp['l2_ffn_w3'] = w((D_MODEL, D_FF), D_MODEL)
    inp['l2_ffn_w2'] = w((D_FF, D_MODEL), D_FF)
    inp['l3_norm_mix'] = gain()
    inp['l3_a_w_in'] = w((D_MODEL, A_W_IN), D_MODEL)
    inp['l3_a_w_out'] = w((A_HEADS * HEAD_DIM, D_MODEL), A_HEADS * HEAD_DIM)
    inp['l3_norm_ffn'] = gain()
    inp['l3_moe_router'] = w((D_MODEL, N_EXPERTS), D_MODEL)
    inp['l3_moe_w1'] = w((N_EXPERTS, D_MODEL, D_FF_EXPERT), D_MODEL)
    inp['l3_moe_w3'] = w((N_EXPERTS, D_MODEL, D_FF_EXPERT), D_MODEL)
    inp['l3_moe_w2'] = w((N_EXPERTS, D_FF_EXPERT, D_MODEL), D_FF_EXPERT)
    inp['final_norm'] = gain()
    return inp


def reference(x, positions,
              l0_norm_mix, l0_a_w_in, l0_a_w_out, l0_norm_ffn, l0_ffn_w1, l0_ffn_w3, l0_ffn_w2,
              l1_norm_mix, l1_b_w_in, l1_b_lambda_q1, l1_b_lambda_k1, l1_b_lambda_q2, l1_b_lambda_k2,
              l1_b_subln, l1_b_w_out, l1_norm_ffn, l1_moe_router, l1_moe_w1, l1_moe_w3, l1_moe_w2,
              l2_norm_mix, l2_c_w_in, l2_c_sink, l2_c_w_out, l2_norm_ffn, l2_ffn_w1, l2_ffn_w3, l2_ffn_w2,
              l3_norm_mix, l3_a_w_in, l3_a_w_out, l3_norm_ffn, l3_moe_router, l3_moe_w1, l3_moe_w3, l3_moe_w2,
              final_norm):
    cos, sin = rope_tables(positions)
    mixers = [
        (l0_norm_mix, (l0_a_w_in, l0_a_w_out)),
        (l1_norm_mix, (l1_b_w_in, l1_b_lambda_q1, l1_b_lambda_k1, l1_b_lambda_q2, l1_b_lambda_k2,
                       l1_b_subln, l1_b_w_out)),
        (l2_norm_mix, (l2_c_w_in, l2_c_sink, l2_c_w_out)),
        (l3_norm_mix, (l3_a_w_in, l3_a_w_out)),
    ]
    ffns = [
        (l0_norm_ffn, (l0_ffn_w1, l0_ffn_w3, l0_ffn_w2)),
        (l1_norm_ffn, (l1_moe_router, l1_moe_w1, l1_moe_w3, l1_moe_w2)),
        (l2_norm_ffn, (l2_ffn_w1, l2_ffn_w3, l2_ffn_w2)),
        (l3_norm_ffn, (l3_moe_router, l3_moe_w1, l3_moe_w3, l3_moe_w2)),
    ]
    for i in range(DEPTH):
        g_mix, p_mix = mixers[i]
        h = rmsnorm(x, g_mix)
        kind = i % N_MIXERS
        if kind == 0:
            y = dilated_mixer(h, cos, sin, *p_mix)
        elif kind == 1:
            lambda_init = 0.8 - 0.6 * math.exp(-0.3 * i)
            y = diff_mixer(h, cos, sin, *p_mix, lambda_init)
        else:
            y = swa_sink_mixer(h, cos, sin, *p_mix)
        x = x + y
        g_ffn, p_ffn = ffns[i]
        h = rmsnorm(x, g_ffn)
        y = dense_swiglu(h, *p_ffn) if i % 2 == 0 else moe_swiglu(h, *p_ffn)
        x = x + y
    return rmsnorm(x, final_norm)
```

```python
import functools
import math

import jax
import jax.numpy as jnp
from jax import lax
from jax.experimental import pallas as pl
from jax.experimental.pallas import tpu as pltpu

F32 = jnp.float32
BF16 = jnp.bfloat16

D_MODEL = 1024
HEAD_DIM = 64
ROT_DIM = HEAD_DIM // 4
ROPE_THETA = 500000.0
NORM_EPS = 1e-6
SUBLN_EPS = 1e-5
DILATIONS = (1, 4, 16)
A_HALF = 64
C_HALF = 128
N_EXPERTS = 8
LANES = 128
VMEM_LIMIT = 56 * 1024 * 1024

MOE_BLOCK = 512


def _cparams(sem):
    return pltpu.CompilerParams(dimension_semantics=sem, vmem_limit_bytes=VMEM_LIMIT)


def _rope_tables(positions):
    inv_freq = ROPE_THETA ** (-(jnp.arange(0, ROT_DIM, 2, dtype=F32) / ROT_DIM))
    ang = positions.astype(F32)[..., None] * inv_freq
    c, s = jnp.cos(ang), jnp.sin(ang)
    ones = jnp.ones(ang.shape[:-1] + (HEAD_DIM - ROT_DIM,), F32)
    cos64 = jnp.concatenate([c, c, ones], axis=-1)
    sin64 = jnp.concatenate([-s, s, 0.0 * ones], axis=-1)
    return (jnp.concatenate([cos64, cos64], axis=-1),
            jnp.concatenate([sin64, sin64], axis=-1))


def _to_strided(t, d):
    B, S, C = t.shape
    return jnp.swapaxes(t.reshape(B, S // d, d, C), 1, 2)


def _rope_chunk(x, cos, sin, first_half):
    partner = jnp.where(first_half, pltpu.roll(x, LANES - ROT_DIM // 2, 1),
                        pltpu.roll(x, ROT_DIM // 2, 1))
    return x * cos + partner * sin


def _proj_kernel(*refs, dil, rope_period, rope_chunks):
    if dil > 1:
        x_ref, g_ref, w_ref, cos_ref, sin_ref, perm_ref, o_ref, h_scr = refs
    else:
        x_ref, g_ref, w_ref, cos_ref, sin_ref, o_ref, h_scr = refs
    j = pl.program_id(2)
    tm = x_ref.shape[0]
    tn = w_ref.shape[1]
    n = tm // dil

    @pl.when(j == 0)
    def _():
        x = x_ref[...]
        ms = jnp.mean(x * x, axis=-1, keepdims=True)
        h = (x * lax.rsqrt(ms + NORM_EPS) * g_ref[...]).astype(BF16)
        if dil > 1:
            h = jnp.dot(perm_ref[...], h, preferred_element_type=F32).astype(BF16)
        h_scr[...] = h

    res = jnp.dot(h_scr[...], w_ref[...], preferred_element_type=F32)

    def store(val):
        val = val.astype(o_ref.dtype)
        for r in range(dil):
            o_ref[r] = val[r * n:(r + 1) * n]

    def rope_all():
        lane = lax.broadcasted_iota(jnp.int32, (1, LANES), 1)
        first_half = (lane % HEAD_DIM) < (ROT_DIM // 2)
        cos = cos_ref[...].reshape(tm, LANES)
        sin = sin_ref[...].reshape(tm, LANES)
        parts = []
        for c in range(tn // LANES):
            xc = res[:, c * LANES:(c + 1) * LANES]
            if rope_chunks is None or rope_chunks[c]:
                xc = _rope_chunk(xc, cos, sin, first_half)
            parts.append(xc)
        return jnp.concatenate(parts, axis=1)

    if rope_period is None:
        store(rope_all())
    else:
        is_rope = (j % rope_period) != (rope_period - 1)

        @pl.when(is_rope)
        def _():
            store(rope_all())

        @pl.when(jnp.logical_not(is_rope))
        def _():
            store(res)


def _norm_proj(x, g, w, cos, sin, *, dil, col0, ncols, tn, rope_period, rope_chunks, tm=512):
    B, S, D = x.shape
    n = tm // dil
    L = S // dil
    grid = (B, S // tm, ncols // tn)
    cb0 = col0 // tn
    in_specs = [
        pl.BlockSpec((None, tm, D), lambda b, i, j: (b, i, 0)),
        pl.BlockSpec((1, D), lambda b, i, j: (0, 0)),
        pl.BlockSpec((D, tn), lambda b, i, j: (0, cb0 + j)),
        pl.BlockSpec((None, dil, n, LANES), lambda b, i, j: (b, 0, i, 0)),
        pl.BlockSpec((None, dil, n, LANES), lambda b, i, j: (b, 0, i, 0)),
    ]
    args = [x, g.reshape(1, D), w, cos, sin]
    if dil > 1:
        p = jnp.arange(tm)
        src = (p % n) * dil + p // n
        perm = (src[:, None] == jnp.arange(tm)[None, :]).astype(BF16)
        in_specs.append(pl.BlockSpec((tm, tm), lambda b, i, j: (0, 0)))
        args.append(perm)
    return pl.pallas_call(
        functools.partial(_proj_kernel, dil=dil, rope_period=rope_period, rope_chunks=rope_chunks),
        grid=grid,
        in_specs=in_specs,
        out_specs=pl.BlockSpec((None, dil, n, tn), lambda b, i, j: (b, 0, i, j)),
        out_shape=jax.ShapeDtypeStruct((B, dil, L, ncols), BF16),
        scratch_shapes=[pltpu.VMEM((tm, D), BF16)],
        compiler_params=_cparams(("parallel", "parallel", "arbitrary")),
        name=f"norm_proj_d{dil}",
    )(*args)


def _band_pair(q2, kw, vw, mask, lane_lo, sink=None):
    outs, lses = [], []
    for hsel in (lane_lo, jnp.logical_not(lane_lo)):
        qh = jnp.where(hsel, q2, jnp.zeros_like(q2)) * jnp.asarray(HEAD_DIM ** -0.5, q2.dtype)
        s = lax.dot_general(qh, kw, (((1,), (1,)), ((), ())), preferred_element_type=F32)
        s = jnp.where(mask, s, -jnp.inf)
        m = jnp.max(s, axis=-1, keepdims=True)
        outs.append((s, m))
    res = []
    for idx, (s, m) in enumerate(outs):
        if sink is not None:
            sk = sink[idx]
            m = jnp.maximum(m, sk)
        p = jnp.exp(s - m)
        l = jnp.sum(p, axis=-1, keepdims=True)
        if sink is not None:
            l = l + jnp.exp(sk - m)
        o = jnp.dot(p.astype(BF16), vw, preferred_element_type=F32) / l
        res.append((o, m + jnp.log(l)))
    o = jnp.where(lane_lo, res[0][0], res[1][0])
    lse = jnp.where(lane_lo, res[0][1], res[1][1])
    return o, lse


def _attn_a_kernel(*refs, TB, S):
    qkv = refs[:9]
    o_ref = refs[9]
    scr = refs[10:]
    t = pl.program_id(2)
    lane = lax.broadcasted_iota(jnp.int32, (1, LANES), 1)
    lane_lo = lane < HEAD_DIM
    W = 4 * A_HALF
    for g, dil in enumerate(DILATIONS):
        q_ref, k_ref, v_ref = qkv[3 * g:3 * g + 3]
        o_scr, l_scr = scr[2 * g], scr[2 * g + 1]
        nq = TB // dil
        tq = min(nq, 128)
        nsub = nq // tq
        L = S // dil

        def body(idx, carry, q_ref=q_ref, k_ref=k_ref, v_ref=v_ref, o_scr=o_scr, l_scr=l_scr,
                 dil=dil, nq=nq, tq=tq, nsub=nsub, L=L):
            r = idx // nsub
            jj = idx % nsub
            qs = t * nq + jj * tq
            ws = pl.multiple_of(jnp.clip(qs - A_HALF, 0, L - W), A_HALF)
            q2 = q_ref[r, pl.ds(pl.multiple_of(jj * tq, tq), tq), :]
            kw = k_ref[r, pl.ds(ws, W), :]
            vw = v_ref[r, pl.ds(ws, W), :]
            qpos = qs + lax.broadcasted_iota(jnp.int32, (tq, W), 0)
            kpos = ws + lax.broadcasted_iota(jnp.int32, (tq, W), 1)
            mask = jnp.abs(qpos - kpos) <= A_HALF
            o, lse = _band_pair(q2, kw, vw, mask, lane_lo)
            row0 = jj * (tq * dil) + r
            if dil == 1:
                o_scr[pl.ds(pl.multiple_of(row0, tq), tq), :] = o
                l_scr[pl.ds(pl.multiple_of(row0, tq), tq), :] = lse
            else:
                o_scr[pl.ds(row0, tq, stride=dil), :] = o
                l_scr[pl.ds(row0, tq, stride=dil), :] = lse
            return carry

        lax.fori_loop(0, dil * nsub, body, 0)

    l0, l1, l2 = scr[1][...], scr[3][...], scr[5][...]
    mx = jnp.maximum(jnp.maximum(l0, l1), l2)
    e0, e1, e2 = jnp.exp(l0 - mx), jnp.exp(l1 - mx), jnp.exp(l2 - mx)
    den = e0 + e1 + e2
    o = (e0 * scr[0][...] + e1 * scr[2][...] + e2 * scr[4][...]) / den
    o_ref[...] = o.astype(o_ref.dtype)


def _attn_a(qkvs, B, S, TB=2048):
    n_pairs = (16 * HEAD_DIM) // LANES
    in_specs, args = [], []
    for g, dil in enumerate(DILATIONS):
        L = S // dil
        in_specs += [
            pl.BlockSpec((None, dil, TB // dil, LANES), lambda b, hp, t: (b, 0, t, hp)),
            pl.BlockSpec((None, dil, L, LANES), lambda b, hp, t: (b, 0, 0, n_pairs + hp)),
            pl.BlockSpec((None, dil, L, LANES), lambda b, hp, t: (b, 0, 0, 2 * n_pairs + hp)),
        ]
        args += [qkvs[g]] * 3
    return pl.pallas_call(
        functools.partial(_attn_a_kernel, TB=TB, S=S),
        grid=(B, n_pairs, S // TB),
        in_specs=in_specs,
        out_specs=pl.BlockSpec((None, TB, LANES), lambda b, hp, t: (b, t, hp)),
        out_shape=jax.ShapeDtypeStruct((B, S, n_pairs * LANES), BF16),
        scratch_shapes=[pltpu.VMEM((TB, LANES), F32)] * 6,
        compiler_params=_cparams(("parallel", "parallel", "arbitrary")),
        name="attn_dilated",
    )(*args)


def _attn_c_kernel(q_ref, k_ref, v_ref, sink_ref, o_ref, *, TB, S):
    t = pl.program_id(1)
    lane = lax.broadcasted_iota(jnp.int32, (1, LANES), 1)
    lane_lo = lane < HEAD_DIM
    tq = 128
    W = tq + 2 * C_HALF
    n_pairs = q_ref.shape[1] // LANES

    def body(jj, carry):
        qs = t * TB + jj * tq
        ws = pl.multiple_of(jnp.clip(qs - C_HALF, 0, S - W), C_HALF)
        kw = k_ref[pl.ds(ws, W), :]
        vw = v_ref[pl.ds(ws, W), :]
        qpos = qs + lax.broadcasted_iota(jnp.int32, (tq, W), 0)
        kpos = ws + lax.broadcasted_iota(jnp.int32, (tq, W), 1)
        mask = jnp.abs(qpos - kpos) <= C_HALF
        rows = pl.ds(pl.multiple_of(jj * tq, tq), tq)
        for p in range(n_pairs):
            q2 = q_ref[rows, p * LANES:(p + 1) * LANES]
            sk = sink_ref[p:p + 1, :]
            sink = (sk[:, :1], sk[:, LANES - 1:])
            o, _ = _band_pair(q2, kw, vw, mask, lane_lo, sink=sink)
            o_ref[rows, p * LANES:(p + 1) * LANES] = o.astype(o_ref.dtype)
        return carry

    lax.fori_loop(0, TB // tq, body, 0)


def _attn_c(qkv, sink_tab, B, S, TB=1024):
    nq = 16 * HEAD_DIM
    return pl.pallas_call(
        functools.partial(_attn_c_kernel, TB=TB, S=S),
        grid=(B, S // TB),
        in_specs=[
            pl.BlockSpec((None, TB, nq), lambda b, t: (b, t, 0)),
            pl.BlockSpec((None, S, LANES), lambda b, t: (b, 0, nq // LANES)),
            pl.BlockSpec((None, S, LANES), lambda b, t: (b, 0, nq // LANES + 1)),
            pl.BlockSpec((8, LANES), lambda b, t: (0, 0)),
        ],
        out_specs=pl.BlockSpec((None, TB, nq), lambda b, t: (b, t, 0)),
        out_shape=jax.ShapeDtypeStruct((B, S, nq), BF16),
        compiler_params=_cparams(("parallel", "arbitrary")),
        name="attn_swa_sink",
    )(qkv, qkv, qkv, sink_tab)


def _attn_b_kernel(lam_ref, q_ref, k_ref, v_ref, subln_ref, o_ref, m_scr, l_scr, acc_scr, *, lambda_init):
    kv = pl.program_id(3)
    nkv = pl.num_programs(3)
    lane = lax.broadcasted_iota(jnp.int32, (1, LANES), 1)
    lane_lo = lane < HEAD_DIM

    @pl.when(kv == 0)
    def _():
        m_scr[...] = jnp.full(m_scr.shape, -jnp.inf, F32)
        l_scr[...] = jnp.zeros(l_scr.shape, F32)
        acc_scr[...] = jnp.zeros(acc_scr.shape, F32)

    q2 = q_ref[...]
    k = k_ref[...]
    v = v_ref[...]
    for c, hsel in enumerate((lane_lo, jnp.logical_not(lane_lo))):
        qh = jnp.where(hsel, q2, jnp.zeros_like(q2)) * jnp.asarray(HEAD_DIM ** -0.5, q2.dtype)
        s = lax.dot_general(qh, k, (((1,), (1,)), ((), ())), preferred_element_type=F32)
        m_old = m_scr[c]
        m_new = jnp.maximum(m_old, jnp.max(s, axis=-1, keepdims=True))
        alpha = jnp.exp(m_old - m_new)
        p = jnp.exp(s - m_new)
        l_scr[c] = alpha * l_scr[c] + jnp.sum(p, axis=-1, keepdims=True)
        acc_scr[c] = alpha * acc_scr[c] + jnp.dot(p.astype(BF16), v, preferred_element_type=F32)
        m_scr[c] = m_new

    @pl.when(kv == nkv - 1)
    def _():
        lp = lam_ref[...]
        lam = (jnp.exp(jnp.sum(lp[0:1] * lp[1:2], axis=-1, keepdims=True))
               - jnp.exp(jnp.sum(lp[2:3] * lp[3:4], axis=-1, keepdims=True)) + lambda_init)
        o = acc_scr[0] / l_scr[0] - lam * (acc_scr[1] / l_scr[1])
        ms = jnp.mean(o * o, axis=-1, keepdims=True)
        o = o * lax.rsqrt(ms + SUBLN_EPS) * subln_ref[...] * (1.0 - lambda_init)
        o_ref[...] = o.astype(o_ref.dtype)


def _attn_b(qkv, lam_params, subln, lambda_init, B, S, tq=512, tk=512):
    H = 8
    return pl.pallas_call(
        functools.partial(_attn_b_kernel, lambda_init=lambda_init),
        grid=(B, H, S // tq, S // tk),
        in_specs=[
            pl.BlockSpec((4, HEAD_DIM), lambda b, h, i, kv: (0, 0)),
            pl.BlockSpec((None, tq, LANES), lambda b, h, i, kv: (b, i, h)),
            pl.BlockSpec((None, tk, LANES), lambda b, h, i, kv: (b, kv, H + h)),
            pl.BlockSpec((None, tk, LANES), lambda b, h, i, kv: (b, kv, 2 * H + h)),
            pl.BlockSpec((1, LANES), lambda b, h, i, kv: (0, 0)),
        ],
        out_specs=pl.BlockSpec((None, tq, LANES), lambda b, h, i, kv: (b, i, h)),
        out_shape=jax.ShapeDtypeStruct((B, S, H * LANES), BF16),
        scratch_shapes=[pltpu.VMEM((2, tq, 1), F32), pltpu.VMEM((2, tq, 1), F32),
                        pltpu.VMEM((2, tq, LANES), F32)],
        compiler_params=_cparams(("parallel", "parallel", "parallel", "arbitrary")),
        name="attn_diff",
    )(lam_params, qkv, qkv, qkv, subln.reshape(1, LANES))


def _out_proj_kernel(x_ref, o_ref, w_ref, y_ref):
    y_ref[...] = x_ref[...] + jnp.dot(o_ref[...], w_ref[...], preferred_element_type=F32)


def _out_proj(x2, o2, w, tm=1024):
    T, D = x2.shape
    K = o2.shape[1]
    return pl.pallas_call(
        _out_proj_kernel,
        grid=(T // tm,),
        in_specs=[pl.BlockSpec((tm, D), lambda i: (i, 0)),
                  pl.BlockSpec((tm, K), lambda i: (i, 0)),
                  pl.BlockSpec((K, D), lambda i: (0, 0))],
        out_specs=pl.BlockSpec((tm, D), lambda i: (i, 0)),
        out_shape=jax.ShapeDtypeStruct((T, D), F32),
        compiler_params=_cparams(("parallel",)),
        name="out_proj",
    )(x2, o2, w)


def _ffn_kernel(x_ref, g_ref, w1_ref, w3_ref, w2_ref, y_ref, h_scr, acc_scr):
    f = pl.program_id(1)

    @pl.when(f == 0)
    def _():
        x = x_ref[...]
        ms = jnp.mean(x * x, axis=-1, keepdims=True)
        h_scr[...] = (x * lax.rsqrt(ms + NORM_EPS) * g_ref[...]).astype(BF16)
        acc_scr[...] = x

    h = h_scr[...]
    u = jnp.dot(h, w1_ref[...], preferred_element_type=F32)
    v = jnp.dot(h, w3_ref[...], preferred_element_type=F32)
    a = (u * jax.nn.sigmoid(u) * v).astype(BF16)
    acc_scr[...] += jnp.dot(a, w2_ref[...], preferred_element_type=F32)

    @pl.when(f == pl.num_programs(1) - 1)
    def _():
        y_ref[...] = acc_scr[...]


def _ffn_dense(x2, g, w1, w3, w2, tm=1024, tf=256):
    T, D = x2.shape
    F = w1.shape[1]
    return pl.pallas_call(
        _ffn_kernel,
        grid=(T // tm, F // tf),
        in_specs=[pl.BlockSpec((tm, D), lambda i, f: (i, 0)),
                  pl.BlockSpec((1, D), lambda i, f: (0, 0)),
                  pl.BlockSpec((D, tf), lambda i, f: (0, f)),
                  pl.BlockSpec((D, tf), lambda i, f: (0, f)),
                  pl.BlockSpec((tf, D), lambda i, f: (f, 0))],
        out_specs=pl.BlockSpec((tm, D), lambda i, f: (i, 0)),
        out_shape=jax.ShapeDtypeStruct((T, D), F32),
        scratch_shapes=[pltpu.VMEM((tm, D), BF16), pltpu.VMEM((tm, D), F32)],
        compiler_params=_cparams(("parallel", "arbitrary")),
        name="ffn_dense",
    )(x2, g.reshape(1, D), w1, w3, w2)


def _router_kernel(x_ref, g_ref, r_ref, h_ref, mi_ref, mf_ref, cnt_ref, carry_scr):
    i = pl.program_id(0)
    tm = x_ref.shape[0]

    @pl.when(i == 0)
    def _():
        carry_scr[...] = jnp.zeros(carry_scr.shape, F32)

    x = x_ref[...]
    ms = jnp.mean(x * x, axis=-1, keepdims=True)
    h = x * lax.rsqrt(ms + NORM_EPS) * g_ref[...]
    h_ref[...] = h
    lane = lax.broadcasted_iota(jnp.int32, (tm, LANES), 1)
    logits = jnp.dot(h, r_ref[...], preferred_element_type=F32, precision=lax.Precision.HIGHEST)
    logits = jnp.where(lane < N_EXPERTS, logits, -jnp.inf)
    v0 = jnp.max(logits, axis=-1, keepdims=True)
    i0 = jnp.min(jnp.where(logits == v0, lane, LANES), axis=-1, keepdims=True)
    rest = jnp.where(lane == i0, -jnp.inf, logits)
    v1 = jnp.max(rest, axis=-1, keepdims=True)
    i1 = jnp.min(jnp.where(rest == v1, lane, LANES), axis=-1, keepdims=True)
    tt = jnp.exp(v1 - v0)
    g0 = 1.0 / (1.0 + tt)
    g1 = tt / (1.0 + tt)
    sel0 = lane == i0
    sel1 = lane == i1
    onehot = jnp.where(jnp.logical_or(sel0, sel1), 1.0, 0.0)
    row = lax.broadcasted_iota(jnp.int32, (tm, tm), 0)
    col = lax.broadcasted_iota(jnp.int32, (tm, tm), 1)
    tri = jnp.where(row > col, 1.0, 0.0).astype(BF16)
    before = carry_scr[...] + jnp.dot(tri, onehot.astype(BF16), preferred_element_type=F32)
    rank0 = jnp.sum(jnp.where(sel0, before, 0.0), axis=-1, keepdims=True)
    rank1 = jnp.sum(jnp.where(sel1, before, 0.0), axis=-1, keepdims=True)
    carry_scr[...] = carry_scr[...] + jnp.sum(onehot, axis=0, keepdims=True)
    mi = jnp.where(lane == 0, i0, jnp.where(lane == 1, i1, 0))
    mi = jnp.where(lane == 2, rank0.astype(jnp.int32), jnp.where(lane == 3, rank1.astype(jnp.int32), mi))
    mi_ref[...] = mi
    mf_ref[...] = jnp.where(lane == 0, g0, jnp.where(lane == 1, g1, 0.0))
    cnt_ref[...] = carry_scr[...]


def _router(x2, g, router, tm=512):
    T, D = x2.shape
    rpad = jnp.zeros((D, LANES), F32).at[:, :N_EXPERTS].set(router)
    return pl.pallas_call(
        _router_kernel,
        grid=(T // tm,),
        in_specs=[pl.BlockSpec((tm, D), lambda i: (i, 0)),
                  pl.BlockSpec((1, D), lambda i: (0, 0)),
                  pl.BlockSpec((D, LANES), lambda i: (0, 0))],
        out_specs=[pl.BlockSpec((tm, D), lambda i: (i, 0)),
                   pl.BlockSpec((tm, LANES), lambda i: (i, 0)),
                   pl.BlockSpec((tm, LANES), lambda i: (i, 0)),
                   pl.BlockSpec((1, LANES), lambda i: (0, 0))],
        out_shape=[jax.ShapeDtypeStruct((T, D), F32),
                   jax.ShapeDtypeStruct((T, LANES), jnp.int32),
                   jax.ShapeDtypeStruct((T, LANES), F32),
                   jax.ShapeDtypeStruct((1, LANES), F32)],
        scratch_shapes=[pltpu.VMEM((1, LANES), F32)],
        compiler_params=_cparams(("arbitrary",)),
        name="moe_router",
    )(x2, g.reshape(1, D), rpad)


def _dispatch_kernel(dest_ref, h_hbm, buf_in, buf_hbm, sem, *, tm):
    del buf_in
    i = pl.program_id(0)

    def copy(tok, slot):
        return pltpu.make_async_copy(h_hbm.at[pl.ds(tok, 1)], buf_hbm.at[pl.ds(slot, 1)], sem)

    def issue(n, carry):
        tok = i * tm + n
        copy(tok, dest_ref[2 * n]).start()
        copy(tok, dest_ref[2 * n + 1]).start()
        return carry

    lax.fori_loop(0, tm, issue, 0)

    def drain(n, carry):
        copy(0, 0).wait()
        copy(0, 0).wait()
        return carry

    lax.fori_loop(0, tm, drain, 0)


def _dispatch(h2, dest_flat, n_rows, tm=512):
    T, D = h2.shape
    buf0 = jnp.zeros((n_rows, D), F32)
    return pl.pallas_call(
        functools.partial(_dispatch_kernel, tm=tm),
        grid=(T // tm,),
        in_specs=[pl.BlockSpec((2 * tm,), lambda i: (i,), memory_space=pltpu.SMEM),
                  pl.BlockSpec(memory_space=pl.ANY),
                  pl.BlockSpec(memory_space=pl.ANY)],
        out_specs=pl.BlockSpec(memory_space=pl.ANY),
        out_shape=jax.ShapeDtypeStruct((n_rows, D), F32),
        scratch_shapes=[pltpu.SemaphoreType.DMA(())],
        input_output_aliases={2: 0},
        compiler_params=_cparams(("arbitrary",)),
        name="moe_dispatch",
    )(dest_flat, h2, buf0)


def _moe_ffn_kernel(be_ref, nu_ref, x_ref, w1_ref, w3_ref, w2_ref, y_ref, acc_scr):
    b = pl.program_id(0)
    f = pl.program_id(1)

    @pl.when(b < nu_ref[0])
    def _():
        @pl.when(f == 0)
        def _():
            acc_scr[...] = jnp.zeros(acc_scr.shape, F32)

        h = x_ref[...].astype(BF16)
        u = jnp.dot(h, w1_ref[...], preferred_element_type=F32)
        v = jnp.dot(h, w3_ref[...], preferred_element_type=F32)
        a = (u * jax.nn.sigmoid(u) * v).astype(BF16)
        acc_scr[...] += jnp.dot(a, w2_ref[...], preferred_element_type=F32)

        @pl.when(f == pl.num_programs(1) - 1)
        def _():
            y_ref[...] = acc_scr[...]

    @pl.when(jnp.logical_and(b >= nu_ref[0], f == 0))
    def _():
        y_ref[...] = jnp.zeros(y_ref.shape, F32)


def _moe_ffn(buf, block_expert, n_used, w1, w3, w2, tf=512):
    R, D = buf.shape
    E, _, F = w1.shape
    nb = R // MOE_BLOCK

    def row_map(b, f, be, nu):
        return (jnp.minimum(b, nu[0] - 1), 0)

    def w_in_map(b, f, be, nu):
        live = b < nu[0]
        return (be[jnp.minimum(b, nu[0] - 1)], 0, jnp.where(live, f, F // tf - 1))

    def w_out_map(b, f, be, nu):
        live = b < nu[0]
        return (be[jnp.minimum(b, nu[0] - 1)], jnp.where(live, f, F // tf - 1), 0)

    return pl.pallas_call(
        _moe_ffn_kernel,
        grid_spec=pltpu.PrefetchScalarGridSpec(
            num_scalar_prefetch=2,
            grid=(nb, F // tf),
            in_specs=[pl.BlockSpec((MOE_BLOCK, D), row_map),
                      pl.BlockSpec((None, D, tf), w_in_map),
                      pl.BlockSpec((None, D, tf), w_in_map),
                      pl.BlockSpec((None, tf, D), w_out_map)],
            out_specs=pl.BlockSpec((MOE_BLOCK, D), lambda b, f, be, nu: (b, 0)),
            scratch_shapes=[pltpu.VMEM((MOE_BLOCK, D), F32)],
        ),
        out_shape=jax.ShapeDtypeStruct((R, D), F32),
        compiler_params=_cparams(("arbitrary", "arbitrary")),
        name="moe_ffn",
    )(block_expert, n_used, buf, w1, w3, w2)


def _combine_kernel(dest_ref, x_ref, mf_ref, g_ref, ybuf_hbm, y_ref, rows_scr, sem, *, tm, final_norm):
    def copy(slot, k, n):
        return pltpu.make_async_copy(ybuf_hbm.at[pl.ds(slot, 1)], rows_scr.at[k, pl.ds(n, 1)], sem)

    def issue(n, carry):
        copy(dest_ref[2 * n], 0, n).start()
        copy(dest_ref[2 * n + 1], 1, n).start()
        return carry

    lax.fori_loop(0, tm, issue, 0)

    def drain(n, carry):
        copy(0, 0, 0).wait()
        copy(0, 1, 0).wait()
        return carry

    lax.fori_loop(0, tm, drain, 0)

    mf = mf_ref[...]
    y = x_ref[...] + mf[:, 0:1] * rows_scr[0] + mf[:, 1:2] * rows_scr[1]
    if final_norm:
        ms = jnp.mean(y * y, axis=-1, keepdims=True)
        y = y * lax.rsqrt(ms + NORM_EPS) * g_ref[...]
    y_ref[...] = y


def _combine(x2, mf, dest_flat, ybuf, g_final, final_norm, tm=512):
    T, D = x2.shape
    return pl.pallas_call(
        functools.partial(_combine_kernel, tm=tm, final_norm=final_norm),
        grid=(T // tm,),
        in_specs=[pl.BlockSpec((2 * tm,), lambda i: (i,), memory_space=pltpu.SMEM),
                  pl.BlockSpec((tm, D), lambda i: (i, 0)),
                  pl.BlockSpec((tm, LANES), lambda i: (i, 0)),
                  pl.BlockSpec((1, D), lambda i: (0, 0)),
                  pl.BlockSpec(memory_space=pl.ANY)],
        out_specs=pl.BlockSpec((tm, D), lambda i: (i, 0)),
        out_shape=jax.ShapeDtypeStruct((T, D), F32),
        scratch_shapes=[pltpu.VMEM((2, tm, D), F32), pltpu.SemaphoreType.DMA(())],
        compiler_params=_cparams(("arbitrary",)),
        name="moe_combine",
    )(dest_flat, x2, mf, g_final.reshape(1, D), ybuf)


def _moe_layer(x2, g, router, w1, w3, w2, g_final, final_norm):
    T, D = x2.shape
    h2, mi, mf, cnt = _router(x2, g, router)
    counts = cnt[0, :N_EXPERTS].astype(jnp.int32)
    padded = ((counts + MOE_BLOCK - 1) // MOE_BLOCK) * MOE_BLOCK
    pend = jnp.cumsum(padded)
    pstart = pend - padded
    dest = pstart[mi[:, 0:2]] + mi[:, 2:4]
    dest_flat = dest.reshape(-1).astype(jnp.int32)
    nb = -(-(2 * T + N_EXPERTS * (MOE_BLOCK - 1)) // MOE_BLOCK)
    block_start = jnp.arange(nb, dtype=jnp.int32) * MOE_BLOCK
    block_expert = jnp.minimum(jnp.searchsorted(pend, block_start, side='right'),
                               N_EXPERTS - 1).astype(jnp.int32)
    n_used = (pend[-1:] // MOE_BLOCK).astype(jnp.int32)
    buf = _dispatch(h2, dest_flat, nb * MOE_BLOCK)
    ybuf = _moe_ffn(buf, block_expert, n_used, w1, w3, w2)
    return _combine(x2, mf, dest_flat, ybuf, g_final, final_norm)


def _mixer_a(x, g, w_in, w_out, tabs):
    B, S, D = x.shape
    w = w_in.astype(BF16)
    qkvs = []
    for gi, dil in enumerate(DILATIONS):
        cos, sin = tabs[dil]
        qkvs.append(_norm_proj(x, g, w, cos, sin, dil=dil, col0=gi * 3 * D, ncols=3 * D, tn=D,
                               rope_period=3, rope_chunks=None))
    o = _attn_a(qkvs, B, S)
    return _out_proj(x.reshape(B * S, D), o.reshape(B * S, D), w_out.astype(BF16)).reshape(B, S, D)


def _mixer_b(x, g, w_in, lq1, lk1, lq2, lk2, subln, w_out, lambda_init, tabs):
    B, S, D = x.shape
    cos, sin = tabs[1]
    qkv = _norm_proj(x, g, w_in.astype(BF16), cos, sin, dil=1, col0=0, ncols=3 * D, tn=D,
                     rope_period=3, rope_chunks=None).reshape(B, S, 3 * D)
    lam_params = jnp.stack([lq1, lk1, lq2, lk2]).astype(F32)
    o = _attn_b(qkv, lam_params, subln.astype(F32), lambda_init, B, S)
    return _out_proj(x.reshape(B * S, D), o.reshape(B * S, D), w_out.astype(BF16)).reshape(B, S, D)


def _mixer_c(x, g, w_in, sink, w_out, tabs):
    B, S, D = x.shape
    cos, sin = tabs[1]
    head_order = jnp.arange(16).reshape(2, 8).T.reshape(-1)
    col_order = (head_order[:, None] * HEAD_DIM + jnp.arange(HEAD_DIM)[None, :]).reshape(-1)
    w_q = w_in[:, :D][:, col_order]
    w = jnp.concatenate([w_q, w_in[:, D:]], axis=1).astype(BF16)
    ncols = w.shape[1]
    chunks = tuple(c < (ncols // LANES - 1) for c in range(ncols // LANES))
    qkv = _norm_proj(x, g, w, cos, sin, dil=1, col0=0, ncols=ncols, tn=ncols,
                     rope_period=None, rope_chunks=chunks).reshape(B, S, ncols)
    sk = sink.astype(F32)
    sink_tab = jnp.concatenate([jnp.broadcast_to(sk[:8, None], (8, HEAD_DIM)),
                                jnp.broadcast_to(sk[8:, None], (8, HEAD_DIM))], axis=1)
    o = _attn_c(qkv, sink_tab, B, S)
    w_o = w_out[col_order, :].astype(BF16)
    return _out_proj(x.reshape(B * S, D), o.reshape(B * S, D), w_o).reshape(B, S, D)


def kernel(x, positions, l0_norm_mix, l0_a_w_in, l0_a_w_out, l0_norm_ffn, l0_ffn_w1, l0_ffn_w3, l0_ffn_w2, l1_norm_mix, l1_b_w_in, l1_b_lambda_q1, l1_b_lambda_k1, l1_b_lambda_q2, l1_b_lambda_k2, l1_b_subln, l1_b_w_out, l1_norm_ffn, l1_moe_router, l1_moe_w1, l1_moe_w3, l1_moe_w2, l2_norm_mix, l2_c_w_in, l2_c_sink, l2_c_w_out, l2_norm_ffn, l2_ffn_w1, l2_ffn_w3, l2_ffn_w2, l3_norm_mix, l3_a_w_in, l3_a_w_out, l3_norm_ffn, l3_moe_router, l3_moe_w1, l3_moe_w3, l3_moe_w2, final_norm):
    B, S, D = x.shape
    T = B * S
    cos, sin = _rope_tables(positions)
    tabs = {d: (_to_strided(cos, d), _to_strided(sin, d)) for d in DILATIONS}

    x = _mixer_a(x, l0_norm_mix, l0_a_w_in, l0_a_w_out, tabs)
    x = _ffn_dense(x.reshape(T, D), l0_norm_ffn, l0_ffn_w1.astype(BF16), l0_ffn_w3.astype(BF16),
                   l0_ffn_w2.astype(BF16)).reshape(B, S, D)
    lambda_init = 0.8 - 0.6 * math.exp(-0.3 * 1)
    x = _mixer_b(x, l1_norm_mix, l1_b_w_in, l1_b_lambda_q1, l1_b_lambda_k1, l1_b_lambda_q2,
                 l1_b_lambda_k2, l1_b_subln, l1_b_w_out, lambda_init, tabs)
    x = _moe_layer(x.reshape(T, D), l1_norm_ffn, l1_moe_router, l1_moe_w1.astype(BF16),
                   l1_moe_w3.astype(BF16), l1_moe_w2.astype(BF16), final_norm, False).reshape(B, S, D)
    x = _mixer_c(x, l2_norm_mix, l2_c_w_in, l2_c_sink, l2_c_w_out, tabs)
    x = _ffn_dense(x.reshape(T, D), l2_norm_ffn, l2_ffn_w1.astype(BF16), l2_ffn_w3.astype(BF16),
                   l2_ffn_w2.astype(BF16)).reshape(B, S, D)
    x = _mixer_a(x, l3_norm_mix, l3_a_w_in, l3_a_w_out, tabs)
    x = _moe_layer(x.reshape(T, D), l3_norm_ffn, l3_moe_router, l3_moe_w1.astype(BF16),
                   l3_moe_w3.astype(BF16), l3_moe_w2.astype(BF16), final_norm, True).reshape(B, S, D)
    return x
```

```python
import functools
import math

import jax
import jax.numpy as jnp
from jax import lax
from jax.experimental import pallas as pl
from jax.experimental.pallas import tpu as pltpu

F32 = jnp.float32
BF16 = jnp.bfloat16

D_MODEL = 1024
HEAD_DIM = 64
ROT_DIM = HEAD_DIM // 4
ROPE_THETA = 500000.0
NORM_EPS = 1e-6
SUBLN_EPS = 1e-5
DILATIONS = (1, 4, 16)
A_HALF = 64
C_HALF = 128
N_EXPERTS = 8
LANES = 128
VMEM_LIMIT = 56 * 1024 * 1024

MOE_BLOCK = 512
A_GROUP = 8


def _cparams(sem):
    return pltpu.CompilerParams(dimension_semantics=sem, vmem_limit_bytes=VMEM_LIMIT)


def _rope_tables(positions):
    lane = jnp.arange(LANES) % HEAD_DIM
    half = ROT_DIM // 2
    inv_freq = ROPE_THETA ** (-((2 * (lane % half)).astype(F32) / ROT_DIM))
    freq = jnp.where(lane < ROT_DIM, inv_freq, 0.0)
    sign = jnp.where(lane < half, -1.0, jnp.where(lane < ROT_DIM, 1.0, 0.0))
    ang = positions.astype(F32)[..., None] * freq
    return jnp.cos(ang), jnp.sin(ang) * sign


def _to_strided(t, d):
    B, S, C = t.shape
    return jnp.swapaxes(t.reshape(B, S // d, d, C), 1, 2)


def _rope_chunk(x, cos, sin, first_half):
    partner = jnp.where(first_half, pltpu.roll(x, LANES - ROT_DIM // 2, 1),
                        pltpu.roll(x, ROT_DIM // 2, 1))
    return x * cos + partner * sin


def _proj_kernel(*refs, dil, rope_period, rope_chunks):
    if dil > 1:
        x_ref, g_ref, w_ref, cos_ref, sin_ref, perm_ref, o_ref, h_scr = refs
    else:
        x_ref, g_ref, w_ref, cos_ref, sin_ref, o_ref, h_scr = refs
    j = pl.program_id(2)
    tm = x_ref.shape[0]
    tn = w_ref.shape[1]
    n = tm // dil

    @pl.when(j == 0)
    def _():
        x = x_ref[...]
        ms = jnp.mean(x * x, axis=-1, keepdims=True)
        h = (x * lax.rsqrt(ms + NORM_EPS) * g_ref[...]).astype(BF16)
        if dil > 1:
            h = jnp.dot(perm_ref[...], h, preferred_element_type=F32).astype(BF16)
        h_scr[...] = h

    res = jnp.dot(h_scr[...], w_ref[...], preferred_element_type=F32)

    def store(val):
        val = val.astype(o_ref.dtype)
        for r in range(dil):
            o_ref[r] = val[r * n:(r + 1) * n]

    def rope_all():
        lane = lax.broadcasted_iota(jnp.int32, (1, LANES), 1)
        first_half = (lane % HEAD_DIM) < (ROT_DIM // 2)
        cos = cos_ref[...].reshape(tm, LANES)
        sin = sin_ref[...].reshape(tm, LANES)
        parts = []
        for c in range(tn // LANES):
            xc = res[:, c * LANES:(c + 1) * LANES]
            if rope_chunks is None or rope_chunks[c]:
                xc = _rope_chunk(xc, cos, sin, first_half)
            parts.append(xc)
        return jnp.concatenate(parts, axis=1)

    if rope_period is None:
        store(rope_all())
    else:
        is_rope = (j % rope_period) != (rope_period - 1)

        @pl.when(is_rope)
        def _():
            store(rope_all())

        @pl.when(jnp.logical_not(is_rope))
        def _():
            store(res)


def _norm_proj(x, g, w, cos, sin, *, dil, col0, ncols, tn, rope_period, rope_chunks, tm=512):
    B, S, D = x.shape
    n = tm // dil
    L = S // dil
    grid = (B, S // tm, ncols // tn)
    cb0 = col0 // tn
    in_specs = [
        pl.BlockSpec((None, tm, D), lambda b, i, j: (b, i, 0)),
        pl.BlockSpec((1, D), lambda b, i, j: (0, 0)),
        pl.BlockSpec((D, tn), lambda b, i, j: (0, cb0 + j)),
        pl.BlockSpec((None, dil, n, LANES), lambda b, i, j: (b, 0, i, 0)),
        pl.BlockSpec((None, dil, n, LANES), lambda b, i, j: (b, 0, i, 0)),
    ]
    args = [x, g.reshape(1, D), w, cos, sin]
    if dil > 1:
        p = jnp.arange(tm)
        src = (p % n) * dil + p // n
        perm = (src[:, None] == jnp.arange(tm)[None, :]).astype(BF16)
        in_specs.append(pl.BlockSpec((tm, tm), lambda b, i, j: (0, 0)))
        args.append(perm)
    return pl.pallas_call(
        functools.partial(_proj_kernel, dil=dil, rope_period=rope_period, rope_chunks=rope_chunks),
        grid=grid,
        in_specs=in_specs,
        out_specs=pl.BlockSpec((None, dil, n, tn), lambda b, i, j: (b, 0, i, j)),
        out_shape=jax.ShapeDtypeStruct((B, dil, L, ncols), BF16),
        scratch_shapes=[pltpu.VMEM((tm, D), BF16)],
        compiler_params=_cparams(("parallel", "parallel", "arbitrary")),
        name=f"norm_proj_d{dil}",
    )(*args)


def _fill_band_bias(bias_scr, tq, W, half):
    kk = lax.broadcasted_iota(jnp.int32, (W, 2 * tq), 0)
    qq = lax.broadcasted_iota(jnp.int32, (W, 2 * tq), 1) % tq
    for i in range(3):
        ok = jnp.abs(qq + i * half - kk) <= half
        bias_scr[i] = jnp.where(ok, 0.0, -jnp.inf).astype(F32)


def _band_scores(q2, kw, bias):
    lane_lo = lax.broadcasted_iota(jnp.int32, (1, LANES), 1) < HEAD_DIM
    zero = jnp.zeros_like(q2)
    qq = jnp.concatenate([jnp.where(lane_lo, q2, zero), jnp.where(lane_lo, zero, q2)], axis=0)
    qq = qq * jnp.asarray(HEAD_DIM ** -0.5, q2.dtype)
    return lax.dot_general(kw, qq, (((1,), (1,)), ((), ())), preferred_element_type=F32) + bias


def _band_finish(st, vw, want_lse, sink_row=None):
    tq = st.shape[1] // 2
    row_lo = lax.broadcasted_iota(jnp.int32, (LANES, 1), 0) < HEAD_DIM
    m = jnp.max(st, axis=0, keepdims=True)
    if sink_row is not None:
        m = jnp.maximum(m, sink_row)
    p = jnp.exp(st - m)
    l = jnp.sum(p, axis=0, keepdims=True)
    if sink_row is not None:
        l = l + jnp.exp(sink_row - m)
    ot = lax.dot_general(vw, p.astype(BF16), (((0,), (0,)), ((), ())), preferred_element_type=F32) / l
    o = jnp.where(row_lo, ot[:, :tq], ot[:, tq:]).T
    if not want_lse:
        return o, None
    lse_row = m + jnp.log(l)
    lse = jnp.where(row_lo, jnp.broadcast_to(lse_row[:, :tq], (LANES, tq)),
                    jnp.broadcast_to(lse_row[:, tq:], (LANES, tq))).T
    return o, lse


def _attn_a_kernel(*refs, TB, S):
    qkv = refs[:9]
    o_ref = refs[9]
    scr = refs[10:16]
    bias_scr = refs[16]
    t = pl.program_id(2)
    W = 4 * A_HALF

    @pl.when(t == 0)
    def _():
        _fill_band_bias(bias_scr, 128, W, A_HALF)

    for g, dil in enumerate(DILATIONS):
        q_ref, k_ref, v_ref = qkv[3 * g:3 * g + 3]
        o_scr, l_scr = scr[2 * g], scr[2 * g + 1]
        nq = TB // dil
        tq = min(nq, 128)
        nsub = nq // tq
        L = S // dil

        def body(it, carry, q_ref=q_ref, k_ref=k_ref, v_ref=v_ref, o_scr=o_scr, l_scr=l_scr,
                 dil=dil, nq=nq, tq=tq, nsub=nsub, L=L):
            pend = []
            for u in range(A_GROUP):
                idx = it * A_GROUP + u
                r = idx // nsub
                jj = idx % nsub
                qs = t * nq + jj * tq
                ws = pl.multiple_of(jnp.clip(qs - A_HALF, 0, L - W), A_HALF)
                q2 = q_ref[r, pl.ds(pl.multiple_of(jj * tq, tq), tq), :]
                st = _band_scores(q2, k_ref[r, pl.ds(ws, W), :], bias_scr[(qs - ws) // A_HALF])
                pend.append((st, r, jj, ws))
            for st, r, jj, ws in pend:
                o, lse = _band_finish(st, v_ref[r, pl.ds(ws, W), :], True)
                row0 = jj * (tq * dil) + r
                if dil == 1:
                    o_scr[pl.ds(pl.multiple_of(row0, tq), tq), :] = o
                    l_scr[pl.ds(pl.multiple_of(row0, tq), tq), :] = lse
                else:
                    o_scr[pl.ds(row0, tq, stride=dil), :] = o
                    l_scr[pl.ds(row0, tq, stride=dil), :] = lse
            return carry

        lax.fori_loop(0, dil * nsub // A_GROUP, body, 0)

    l0, l1, l2 = scr[1][...], scr[3][...], scr[5][...]
    mx = jnp.maximum(jnp.maximum(l0, l1), l2)
    e0, e1, e2 = jnp.exp(l0 - mx), jnp.exp(l1 - mx), jnp.exp(l2 - mx)
    den = e0 + e1 + e2
    o = (e0 * scr[0][...] + e1 * scr[2][...] + e2 * scr[4][...]) / den
    o_ref[...] = o.astype(o_ref.dtype)


def _attn_a(qkvs, B, S, TB=2048):
    n_pairs = (16 * HEAD_DIM) // LANES
    in_specs, args = [], []
    for g, dil in enumerate(DILATIONS):
        L = S // dil
        in_specs += [
            pl.BlockSpec((None, dil, TB // dil, LANES), lambda b, hp, t: (b, 0, t, hp)),
            pl.BlockSpec((None, dil, L, LANES), lambda b, hp, t: (b, 0, 0, n_pairs + hp)),
            pl.BlockSpec((None, dil, L, LANES), lambda b, hp, t: (b, 0, 0, 2 * n_pairs + hp)),
        ]
        args += [qkvs[g]] * 3
    return pl.pallas_call(
        functools.partial(_attn_a_kernel, TB=TB, S=S),
        grid=(B, n_pairs, S // TB),
        in_specs=in_specs,
        out_specs=pl.BlockSpec((None, TB, LANES), lambda b, hp, t: (b, t, hp)),
        out_shape=jax.ShapeDtypeStruct((B, S, n_pairs * LANES), BF16),
        scratch_shapes=[pltpu.VMEM((TB, LANES), F32)] * 6 + [pltpu.VMEM((3, 4 * A_HALF, 256), F32)],
        compiler_params=_cparams(("parallel", "parallel", "arbitrary")),
        name="attn_dilated",
    )(*args)


def _attn_c_kernel(q_ref, k_ref, v_ref, sink_ref, o_ref, bias_scr, *, TB, S):
    t = pl.program_id(1)
    tq = 128
    W = tq + 2 * C_HALF
    n_pairs = q_ref.shape[1] // LANES

    @pl.when(t == 0)
    def _():
        _fill_band_bias(bias_scr, tq, W, C_HALF)

    def body(jj, carry):
        qs = t * TB + jj * tq
        ws = pl.multiple_of(jnp.clip(qs - C_HALF, 0, S - W), C_HALF)
        kw = k_ref[pl.ds(ws, W), :]
        vw = v_ref[pl.ds(ws, W), :]
        bias = bias_scr[(qs - ws) // C_HALF]
        rows = pl.ds(pl.multiple_of(jj * tq, tq), tq)
        for p0 in range(0, n_pairs, A_GROUP):
            sts = [_band_scores(q_ref[rows, p * LANES:(p + 1) * LANES], kw, bias)
                   for p in range(p0, p0 + A_GROUP)]
            for p, st in zip(range(p0, p0 + A_GROUP), sts):
                sink_row = sink_ref[p:p + 1, :]
                o, _ = _band_finish(st, vw, False, sink_row=sink_row)
                o_ref[rows, p * LANES:(p + 1) * LANES] = o.astype(o_ref.dtype)
        return carry

    lax.fori_loop(0, TB // tq, body, 0)


def _attn_c(qkv, sink_tab, B, S, TB=1024):
    nq = 16 * HEAD_DIM
    return pl.pallas_call(
        functools.partial(_attn_c_kernel, TB=TB, S=S),
        grid=(B, S // TB),
        in_specs=[
            pl.BlockSpec((None, TB, nq), lambda b, t: (b, t, 0)),
            pl.BlockSpec((None, S, LANES), lambda b, t: (b, 0, nq // LANES)),
            pl.BlockSpec((None, S, LANES), lambda b, t: (b, 0, nq // LANES + 1)),
            pl.BlockSpec((8, 2 * LANES), lambda b, t: (0, 0)),
        ],
        out_specs=pl.BlockSpec((None, TB, nq), lambda b, t: (b, t, 0)),
        out_shape=jax.ShapeDtypeStruct((B, S, nq), BF16),
        scratch_shapes=[pltpu.VMEM((3, LANES + 2 * C_HALF, 2 * LANES), F32)],
        compiler_params=_cparams(("parallel", "arbitrary")),
        name="attn_swa_sink",
    )(qkv, qkv, qkv, sink_tab)


def _attn_b_kernel(lam_ref, q_ref, qn_ref, k_ref, kn_ref, v_ref, subln_ref, o_ref,
                   s0_scr, s1_scr, m_scr, l_scr, acc_scr, *, lambda_init):
    i = pl.program_id(2)
    kv = pl.program_id(3)
    nkv = pl.num_programs(3)
    tk = k_ref.shape[0]
    lane = lax.broadcasted_iota(jnp.int32, (1, LANES), 1)
    lane_lo = lane < HEAD_DIM

    def scores(kref, qref, c):
        k = kref[...] * jnp.asarray(HEAD_DIM ** -0.5, kref.dtype)
        sel = lane_lo if c == 0 else jnp.logical_not(lane_lo)
        kc = jnp.where(sel, k, jnp.zeros_like(k))
        return lax.dot_general(kc, qref[...], (((1,), (1,)), ((), ())), preferred_element_type=F32)

    @pl.when(kv == 0)
    def _():
        m_scr[...] = jnp.full(m_scr.shape, -jnp.inf, F32)
        l_scr[...] = jnp.zeros(l_scr.shape, F32)
        acc_scr[...] = jnp.zeros(acc_scr.shape, F32)

    @pl.when(jnp.logical_and(i == 0, kv == 0))
    def _():
        for c in range(2):
            s0_scr[c * tk:(c + 1) * tk, :] = scores(k_ref, q_ref, c)

    def step(cur_scr, nxt_scr):
        v = v_ref[...]
        for c in range(2):
            nxt_scr[c * tk:(c + 1) * tk, :] = scores(kn_ref, qn_ref, c)
            s = cur_scr[c * tk:(c + 1) * tk, :]
            m_old = m_scr[c]
            m_new = jnp.maximum(m_old, jnp.max(s, axis=0, keepdims=True))
            alpha = jnp.exp(m_old - m_new)
            p = jnp.exp(s - m_new)
            l_scr[c] = alpha * l_scr[c] + jnp.sum(p, axis=0, keepdims=True)
            pv = lax.dot_general(v, p.astype(BF16), (((0,), (0,)), ((), ())), preferred_element_type=F32)
            acc_scr[c] = alpha * acc_scr[c] + pv
            m_scr[c] = m_new

    @pl.when(kv % 2 == 0)
    def _():
        step(s0_scr, s1_scr)

    @pl.when(kv % 2 == 1)
    def _():
        step(s1_scr, s0_scr)

    @pl.when(kv == nkv - 1)
    def _():
        lp = lam_ref[...]
        lam = (jnp.exp(jnp.sum(lp[0:1] * lp[1:2], axis=-1, keepdims=True))
               - jnp.exp(jnp.sum(lp[2:3] * lp[3:4], axis=-1, keepdims=True)) + lambda_init)
        o = acc_scr[0] / l_scr[0] - lam * (acc_scr[1] / l_scr[1])
        ms = jnp.mean(o * o, axis=0, keepdims=True)
        o = o * lax.rsqrt(ms + SUBLN_EPS) * subln_ref[...] * (1.0 - lambda_init)
        o_ref[...] = o.T.astype(o_ref.dtype)


def _attn_b(qkv, lam_params, subln, lambda_init, B, S, tq=512, tk=1024):
    H = 8
    nq, nkv = S // tq, S // tk
    assert nkv % 2 == 0

    def q_next(b, h, i, kv):
        return (b, jnp.minimum(i + (kv + 1) // nkv, nq - 1), h)

    def k_next(b, h, i, kv):
        return (b, (kv + 1) % nkv, H + h)

    return pl.pallas_call(
        functools.partial(_attn_b_kernel, lambda_init=lambda_init),
        grid=(B, H, nq, nkv),
        in_specs=[
            pl.BlockSpec((4, HEAD_DIM), lambda b, h, i, kv: (0, 0)),
            pl.BlockSpec((None, tq, LANES), lambda b, h, i, kv: (b, i, h)),
            pl.BlockSpec((None, tq, LANES), q_next),
            pl.BlockSpec((None, tk, LANES), lambda b, h, i, kv: (b, kv, H + h)),
            pl.BlockSpec((None, tk, LANES), k_next),
            pl.BlockSpec((None, tk, LANES), lambda b, h, i, kv: (b, kv, 2 * H + h)),
            pl.BlockSpec((LANES, 1), lambda b, h, i, kv: (0, 0)),
        ],
        out_specs=pl.BlockSpec((None, tq, LANES), lambda b, h, i, kv: (b, i, h)),
        out_shape=jax.ShapeDtypeStruct((B, S, H * LANES), BF16),
        scratch_shapes=[pltpu.VMEM((2 * tk, tq), F32), pltpu.VMEM((2 * tk, tq), F32),
                        pltpu.VMEM((2, 1, tq), F32), pltpu.VMEM((2, 1, tq), F32),
                        pltpu.VMEM((2, LANES, tq), F32)],
        compiler_params=_cparams(("parallel", "parallel", "arbitrary", "arbitrary")),
        name="attn_diff",
    )(lam_params, qkv, qkv, qkv, qkv, qkv, subln.reshape(LANES, 1))


def _out_proj_kernel(x_ref, o_ref, w_ref, y_ref):
    y_ref[...] = x_ref[...] + jnp.dot(o_ref[...], w_ref[...], preferred_element_type=F32)


def _out_proj(x2, o2, w, tm=1024):
    T, D = x2.shape
    K = o2.shape[1]
    return pl.pallas_call(
        _out_proj_kernel,
        grid=(T // tm,),
        in_specs=[pl.BlockSpec((tm, D), lambda i: (i, 0)),
                  pl.BlockSpec((tm, K), lambda i: (i, 0)),
                  pl.BlockSpec((K, D), lambda i: (0, 0))],
        out_specs=pl.BlockSpec((tm, D), lambda i: (i, 0)),
        out_shape=jax.ShapeDtypeStruct((T, D), F32),
        compiler_params=_cparams(("parallel",)),
        name="out_proj",
    )(x2, o2, w)


def _ffn_kernel(x_ref, g_ref, w1_ref, w3_ref, w2_ref, y_ref, h_scr, acc_scr):
    f = pl.program_id(1)

    @pl.when(f == 0)
    def _():
        x = x_ref[...]
        ms = jnp.mean(x * x, axis=-1, keepdims=True)
        h_scr[...] = (x * lax.rsqrt(ms + NORM_EPS) * g_ref[...]).astype(BF16)
        acc_scr[...] = x

    h = h_scr[...]
    u = jnp.dot(h, w1_ref[...], preferred_element_type=F32)
    v = jnp.dot(h, w3_ref[...], preferred_element_type=F32)
    a = (u * jax.nn.sigmoid(u) * v).astype(BF16)
    acc_scr[...] += jnp.dot(a, w2_ref[...], preferred_element_type=F32)

    @pl.when(f == pl.num_programs(1) - 1)
    def _():
        y_ref[...] = acc_scr[...]


def _ffn_dense(x2, g, w1, w3, w2, tm=1024, tf=256):
    T, D = x2.shape
    F = w1.shape[1]
    return pl.pallas_call(
        _ffn_kernel,
        grid=(T // tm, F // tf),
        in_specs=[pl.BlockSpec((tm, D), lambda i, f: (i, 0)),
                  pl.BlockSpec((1, D), lambda i, f: (0, 0)),
                  pl.BlockSpec((D, tf), lambda i, f: (0, f)),
                  pl.BlockSpec((D, tf), lambda i, f: (0, f)),
                  pl.BlockSpec((tf, D), lambda i, f: (f, 0))],
        out_specs=pl.BlockSpec((tm, D), lambda i, f: (i, 0)),
        out_shape=jax.ShapeDtypeStruct((T, D), F32),
        scratch_shapes=[pltpu.VMEM((tm, D), BF16), pltpu.VMEM((tm, D), F32)],
        compiler_params=_cparams(("parallel", "arbitrary")),
        name="ffn_dense",
    )(x2, g.reshape(1, D), w1, w3, w2)


def _router_kernel(x_ref, g_ref, r_ref, h_ref, mi_ref, mf_ref, cnt_ref, carry_scr):
    i = pl.program_id(0)
    tm = x_ref.shape[0]

    @pl.when(i == 0)
    def _():
        carry_scr[...] = jnp.zeros(carry_scr.shape, F32)

    x = x_ref[...]
    ms = jnp.mean(x * x, axis=-1, keepdims=True)
    h = x * lax.rsqrt(ms + NORM_EPS) * g_ref[...]
    h_ref[...] = h
    lane = lax.broadcasted_iota(jnp.int32, (tm, LANES), 1)
    logits = jnp.dot(h, r_ref[...], preferred_element_type=F32, precision=lax.Precision.HIGHEST)
    logits = jnp.where(lane < N_EXPERTS, logits, -jnp.inf)
    v0 = jnp.max(logits, axis=-1, keepdims=True)
    i0 = jnp.min(jnp.where(logits == v0, lane, LANES), axis=-1, keepdims=True)
    rest = jnp.where(lane == i0, -jnp.inf, logits)
    v1 = jnp.max(rest, axis=-1, keepdims=True)
    i1 = jnp.min(jnp.where(rest == v1, lane, LANES), axis=-1, keepdims=True)
    tt = jnp.exp(v1 - v0)
    g0 = 1.0 / (1.0 + tt)
    g1 = tt / (1.0 + tt)
    sel0 = lane == i0
    sel1 = lane == i1
    onehot = jnp.where(jnp.logical_or(sel0, sel1), 1.0, 0.0)
    row = lax.broadcasted_iota(jnp.int32, (tm, tm), 0)
    col = lax.broadcasted_iota(jnp.int32, (tm, tm), 1)
    tri = jnp.where(row > col, 1.0, 0.0).astype(BF16)
    before = carry_scr[...] + jnp.dot(tri, onehot.astype(BF16), preferred_element_type=F32)
    rank0 = jnp.sum(jnp.where(sel0, before, 0.0), axis=-1, keepdims=True)
    rank1 = jnp.sum(jnp.where(sel1, before, 0.0), axis=-1, keepdims=True)
    carry_scr[...] = carry_scr[...] + jnp.sum(onehot, axis=0, keepdims=True)
    mi = jnp.where(lane == 0, i0, jnp.where(lane == 1, i1, 0))
    mi = jnp.where(lane == 2, rank0.astype(jnp.int32), jnp.where(lane == 3, rank1.astype(jnp.int32), mi))
    mi_ref[...] = mi
    mf_ref[...] = jnp.where(lane == 0, g0, jnp.where(lane == 1, g1, 0.0))
    cnt_ref[...] = carry_scr[...]


def _router(x2, g, router, tm=512):
    T, D = x2.shape
    rpad = jnp.zeros((D, LANES), F32).at[:, :N_EXPERTS].set(router)
    return pl.pallas_call(
        _router_kernel,
        grid=(T // tm,),
        in_specs=[pl.BlockSpec((tm, D), lambda i: (i, 0)),
                  pl.BlockSpec((1, D), lambda i: (0, 0)),
                  pl.BlockSpec((D, LANES), lambda i: (0, 0))],
        out_specs=[pl.BlockSpec((tm, D), lambda i: (i, 0)),
                   pl.BlockSpec((tm, LANES), lambda i: (i, 0)),
                   pl.BlockSpec((tm, LANES), lambda i: (i, 0)),
                   pl.BlockSpec((1, LANES), lambda i: (0, 0))],
        out_shape=[jax.ShapeDtypeStruct((T, D), F32),
                   jax.ShapeDtypeStruct((T, LANES), jnp.int32),
                   jax.ShapeDtypeStruct((T, LANES), F32),
                   jax.ShapeDtypeStruct((1, LANES), F32)],
        scratch_shapes=[pltpu.VMEM((1, LANES), F32)],
        compiler_params=_cparams(("arbitrary",)),
        name="moe_router",
    )(x2, g.reshape(1, D), rpad)


def _dispatch_kernel(dest_ref, h_ref, buf_in, buf_hbm, sem, *, tm):
    del buf_in

    def copy(n, slot):
        return pltpu.make_async_copy(h_ref.at[pl.ds(n, 1)], buf_hbm.at[pl.ds(slot, 1)], sem)

    def issue(n, carry):
        copy(n, dest_ref[2 * n]).start()
        copy(n, dest_ref[2 * n + 1]).start()
        return carry

    lax.fori_loop(0, tm, issue, 0)

    def drain(n, carry):
        copy(0, 0).wait()
        copy(0, 0).wait()
        return carry

    lax.fori_loop(0, tm, drain, 0)


def _dispatch(h2, dest_flat, n_rows, tm=512):
    T, D = h2.shape
    buf0 = jnp.zeros((n_rows, D), F32)
    return pl.pallas_call(
        functools.partial(_dispatch_kernel, tm=tm),
        grid=(T // tm,),
        in_specs=[pl.BlockSpec((2 * tm,), lambda i: (i,), memory_space=pltpu.SMEM),
                  pl.BlockSpec((tm, D), lambda i: (i, 0)),
                  pl.BlockSpec(memory_space=pl.ANY)],
        out_specs=pl.BlockSpec(memory_space=pl.ANY),
        out_shape=jax.ShapeDtypeStruct((n_rows, D), F32),
        scratch_shapes=[pltpu.SemaphoreType.DMA(())],
        input_output_aliases={2: 0},
        compiler_params=_cparams(("arbitrary",)),
        name="moe_dispatch",
    )(dest_flat, h2, buf0)


def _moe_ffn_kernel(be_ref, nu_ref, x_ref, w1_ref, w3_ref, w2_ref, y_ref, acc_scr):
    b = pl.program_id(0)
    f = pl.program_id(1)

    @pl.when(b < nu_ref[0])
    def _():
        @pl.when(f == 0)
        def _():
            acc_scr[...] = jnp.zeros(acc_scr.shape, F32)

        h = x_ref[...].astype(BF16)
        u = jnp.dot(h, w1_ref[...], preferred_element_type=F32)
        v = jnp.dot(h, w3_ref[...], preferred_element_type=F32)
        a = (u * jax.nn.sigmoid(u) * v).astype(BF16)
        acc_scr[...] += jnp.dot(a, w2_ref[...], preferred_element_type=F32)

        @pl.when(f == pl.num_programs(1) - 1)
        def _():
            y_ref[...] = acc_scr[...]

    @pl.when(jnp.logical_and(b >= nu_ref[0], f == 0))
    def _():
        y_ref[...] = jnp.zeros(y_ref.shape, F32)


def _moe_ffn(buf, block_expert, n_used, w1, w3, w2, tf=512):
    R, D = buf.shape
    E, _, F = w1.shape
    nb = R // MOE_BLOCK

    def row_map(b, f, be, nu):
        return (jnp.minimum(b, nu[0] - 1), 0)

    def w_in_map(b, f, be, nu):
        live = b < nu[0]
        return (be[jnp.minimum(b, nu[0] - 1)], 0, jnp.where(live, f, F // tf - 1))

    def w_out_map(b, f, be, nu):
        live = b < nu[0]
        return (be[jnp.minimum(b, nu[0] - 1)], jnp.where(live, f, F // tf - 1), 0)

    return pl.pallas_call(
        _moe_ffn_kernel,
        grid_spec=pltpu.PrefetchScalarGridSpec(
            num_scalar_prefetch=2,
            grid=(nb, F // tf),
            in_specs=[pl.BlockSpec((MOE_BLOCK, D), row_map),
                      pl.BlockSpec((None, D, tf), w_in_map),
                      pl.BlockSpec((None, D, tf), w_in_map),
                      pl.BlockSpec((None, tf, D), w_out_map)],
            out_specs=pl.BlockSpec((MOE_BLOCK, D), lambda b, f, be, nu: (b, 0)),
            scratch_shapes=[pltpu.VMEM((MOE_BLOCK, D), F32)],
        ),
        out_shape=jax.ShapeDtypeStruct((R, D), F32),
        compiler_params=_cparams(("arbitrary", "arbitrary")),
        name="moe_ffn",
    )(block_expert, n_used, buf, w1, w3, w2)


def _combine_kernel(dest_ref, x_ref, mf_ref, g_ref, ybuf_hbm, y_ref, rows_scr, sem, *, tm, final_norm):
    def copy(slot, k, n):
        return pltpu.make_async_copy(ybuf_hbm.at[pl.ds(slot, 1)], rows_scr.at[k, pl.ds(n, 1)], sem)

    def issue(n, carry):
        copy(dest_ref[2 * n], 0, n).start()
        copy(dest_ref[2 * n + 1], 1, n).start()
        return carry

    lax.fori_loop(0, tm, issue, 0)

    def drain(n, carry):
        copy(0, 0, 0).wait()
        copy(0, 1, 0).wait()
        return carry

    lax.fori_loop(0, tm, drain, 0)

    mf = mf_ref[...]
    y = x_ref[...] + mf[:, 0:1] * rows_scr[0] + mf[:, 1:2] * rows_scr[1]
    if final_norm:
        ms = jnp.mean(y * y, axis=-1, keepdims=True)
        y = y * lax.rsqrt(ms + NORM_EPS) * g_ref[...]
    y_ref[...] = y


def _combine(x2, mf, dest_flat, ybuf, g_final, final_norm, tm=512):
    T, D = x2.shape
    return pl.pallas_call(
        functools.partial(_combine_kernel, tm=tm, final_norm=final_norm),
        grid=(T // tm,),
        in_specs=[pl.BlockSpec((2 * tm,), lambda i: (i,), memory_space=pltpu.SMEM),
                  pl.BlockSpec((tm, D), lambda i: (i, 0)),
                  pl.BlockSpec((tm, LANES), lambda i: (i, 0)),
                  pl.BlockSpec((1, D), lambda i: (0, 0)),
                  pl.BlockSpec(memory_space=pl.ANY)],
        out_specs=pl.BlockSpec((tm, D), lambda i: (i, 0)),
        out_shape=jax.ShapeDtypeStruct((T, D), F32),
        scratch_shapes=[pltpu.VMEM((2, tm, D), F32), pltpu.SemaphoreType.DMA(())],
        compiler_params=_cparams(("arbitrary",)),
        name="moe_combine",
    )(dest_flat, x2, mf, g_final.reshape(1, D), ybuf)


def _moe_layer(x2, g, router, w1, w3, w2, g_final, final_norm):
    T, D = x2.shape
    h2, mi, mf, cnt = _router(x2, g, router)
    counts = cnt[0, :N_EXPERTS].astype(jnp.int32)
    padded = ((counts + MOE_BLOCK - 1) // MOE_BLOCK) * MOE_BLOCK
    pend = jnp.cumsum(padded)
    pstart = pend - padded
    dest = pstart[mi[:, 0:2]] + mi[:, 2:4]
    dest_flat = dest.reshape(-1).astype(jnp.int32)
    nb = -(-(2 * T + N_EXPERTS * (MOE_BLOCK - 1)) // MOE_BLOCK)
    block_start = jnp.arange(nb, dtype=jnp.int32) * MOE_BLOCK
    block_expert = jnp.minimum(jnp.sum(pend[None, :] <= block_start[:, None], axis=1),
                               N_EXPERTS - 1).astype(jnp.int32)
    n_used = (pend[-1:] // MOE_BLOCK).astype(jnp.int32)
    buf = _dispatch(h2, dest_flat, nb * MOE_BLOCK)
    ybuf = _moe_ffn(buf, block_expert, n_used, w1, w3, w2)
    return _combine(x2, mf, dest_flat, ybuf, g_final, final_norm)


def _mixer_a(x, g, w_in, w_out, tabs):
    B, S, D = x.shape
    w = w_in.astype(BF16)
    qkvs = []
    for gi, dil in enumerate(DILATIONS):
        cos, sin = tabs[dil]
        qkvs.append(_norm_proj(x, g, w, cos, sin, dil=dil, col0=gi * 3 * D, ncols=3 * D, tn=D,
                               rope_period=3, rope_chunks=None))
    o = _attn_a(qkvs, B, S)
    return _out_proj(x.reshape(B * S, D), o.reshape(B * S, D), w_out.astype(BF16)).reshape(B, S, D)


def _mixer_b(x, g, w_in, lq1, lk1, lq2, lk2, subln, w_out, lambda_init, tabs):
    B, S, D = x.shape
    cos, sin = tabs[1]
    qkv = _norm_proj(x, g, w_in.astype(BF16), cos, sin, dil=1, col0=0, ncols=3 * D, tn=D,
                     rope_period=3, rope_chunks=None).reshape(B, S, 3 * D)
    lam_params = jnp.stack([lq1, lk1, lq2, lk2]).astype(F32)
    o = _attn_b(qkv, lam_params, subln.astype(F32), lambda_init, B, S)
    return _out_proj(x.reshape(B * S, D), o.reshape(B * S, D), w_out.astype(BF16)).reshape(B, S, D)


def _mixer_c(x, g, w_in, sink, w_out, tabs):
    B, S, D = x.shape
    cos, sin = tabs[1]
    head_order = jnp.arange(16).reshape(2, 8).T.reshape(-1)
    col_order = (head_order[:, None] * HEAD_DIM + jnp.arange(HEAD_DIM)[None, :]).reshape(-1)
    w_q = w_in[:, :D][:, col_order]
    w = jnp.concatenate([w_q, w_in[:, D:]], axis=1).astype(BF16)
    ncols = w.shape[1]
    chunks = tuple(c < (ncols // LANES - 1) for c in range(ncols // LANES))
    qkv = _norm_proj(x, g, w, cos, sin, dil=1, col0=0, ncols=ncols, tn=ncols,
                     rope_period=None, rope_chunks=chunks).reshape(B, S, ncols)
    sk = sink.astype(F32)
    sink_tab = jnp.concatenate([jnp.broadcast_to(sk[:8, None], (8, LANES)),
                                jnp.broadcast_to(sk[8:, None], (8, LANES))], axis=1)
    o = _attn_c(qkv, sink_tab, B, S)
    w_o = w_out[col_order, :].astype(BF16)
    return _out_proj(x.reshape(B * S, D), o.reshape(B * S, D), w_o).reshape(B, S, D)


def kernel(x, positions, l0_norm_mix, l0_a_w_in, l0_a_w_out, l0_norm_ffn, l0_ffn_w1, l0_ffn_w3, l0_ffn_w2, l1_norm_mix, l1_b_w_in, l1_b_lambda_q1, l1_b_lambda_k1, l1_b_lambda_q2, l1_b_lambda_k2, l1_b_subln, l1_b_w_out, l1_norm_ffn, l1_moe_router, l1_moe_w1, l1_moe_w3, l1_moe_w2, l2_norm_mix, l2_c_w_in, l2_c_sink, l2_c_w_out, l2_norm_ffn, l2_ffn_w1, l2_ffn_w3, l2_ffn_w2, l3_norm_mix, l3_a_w_in, l3_a_w_out, l3_norm_ffn, l3_moe_router, l3_moe_w1, l3_moe_w3, l3_moe_w2, final_norm):
    B, S, D = x.shape
    T = B * S
    cos, sin = _rope_tables(positions)
    tabs = {d: (_to_strided(cos, d), _to_strided(sin, d)) for d in DILATIONS}

    x = _mixer_a(x, l0_norm_mix, l0_a_w_in, l0_a_w_out, tabs)
    x = _ffn_dense(x.reshape(T, D), l0_norm_ffn, l0_ffn_w1.astype(BF16), l0_ffn_w3.astype(BF16),
                   l0_ffn_w2.astype(BF16)).reshape(B, S, D)
    lambda_init = 0.8 - 0.6 * math.exp(-0.3 * 1)
    x = _mixer_b(x, l1_norm_mix, l1_b_w_in, l1_b_lambda_q1, l1_b_lambda_k1, l1_b_lambda_q2,
                 l1_b_lambda_k2, l1_b_subln, l1_b_w_out, lambda_init, tabs)
    x = _moe_layer(x.reshape(T, D), l1_norm_ffn, l1_moe_router, l1_moe_w1.astype(BF16),
                   l1_moe_w3.astype(BF16), l1_moe_w2.astype(BF16), final_norm, False).reshape(B, S, D)
    x = _mixer_c(x, l2_norm_mix, l2_c_w_in, l2_c_sink, l2_c_w_out, tabs)
    x = _ffn_dense(x.reshape(T, D), l2_norm_ffn, l2_ffn_w1.astype(BF16), l2_ffn_w3.astype(BF16),
                   l2_ffn_w2.astype(BF16)).reshape(B, S, D)
    x = _mixer_a(x, l3_norm_mix, l3_a_w_in, l3_a_w_out, tabs)
    x = _moe_layer(x.reshape(T, D), l3_norm_ffn, l3_moe_router, l3_moe_w1.astype(BF16),
                   l3_moe_w3.astype(BF16), l3_moe_w2.astype(BF16), final_norm, True).reshape(B, S, D)
    return x
```

```python
import functools
import math

import jax
import jax.numpy as jnp
from jax import lax
from jax.experimental import pallas as pl
from jax.experimental.pallas import tpu as pltpu

F32 = jnp.float32
BF16 = jnp.bfloat16

D_MODEL = 1024
HEAD_DIM = 64
ROT_DIM = HEAD_DIM // 4
ROPE_THETA = 500000.0
NORM_EPS = 1e-6
SUBLN_EPS = 1e-5
DILATIONS = (1, 4, 16)
A_HALF = 64
C_HALF = 128
N_EXPERTS = 8
LANES = 128
VMEM_LIMIT = 56 * 1024 * 1024

MOE_BLOCK = 512
_QKV_ROPE_CHUNKS = (True,) * 16 + (False,) * 8
A_GROUP = 8


def _cparams(sem):
    return pltpu.CompilerParams(dimension_semantics=sem, vmem_limit_bytes=VMEM_LIMIT)


def _rope_tables(positions):
    lane = jnp.arange(LANES) % HEAD_DIM
    half = ROT_DIM // 2
    inv_freq = ROPE_THETA ** (-((2 * (lane % half)).astype(F32) / ROT_DIM))
    freq = jnp.where(lane < ROT_DIM, inv_freq, 0.0)
    sign = jnp.where(lane < half, -1.0, jnp.where(lane < ROT_DIM, 1.0, 0.0))
    ang = positions.astype(F32)[..., None] * freq
    return jnp.cos(ang), jnp.sin(ang) * sign


def _to_strided(t, d):
    B, S, C = t.shape
    return jnp.swapaxes(t.reshape(B, S // d, d, C), 1, 2)


def _rope_chunk(x, cos, sin, first_half):
    partner = jnp.where(first_half, pltpu.roll(x, LANES - ROT_DIM // 2, 1),
                        pltpu.roll(x, ROT_DIM // 2, 1))
    return x * cos + partner * sin


def _proj_kernel(*refs, dil, rope_chunks, tc):
    if dil > 1:
        x_ref, g_ref, w_ref, cos_ref, sin_ref, perm_ref, o_ref = refs
    else:
        x_ref, g_ref, w_ref, cos_ref, sin_ref, o_ref = refs
    tm = x_ref.shape[0]
    ncols = w_ref.shape[1]
    n = tm // dil
    x = x_ref[...]
    ms = jnp.mean(x * x, axis=-1, keepdims=True)
    h = (x * lax.rsqrt(ms + NORM_EPS) * g_ref[...]).astype(BF16)
    if dil > 1:
        h = jnp.dot(perm_ref[...], h, preferred_element_type=F32).astype(BF16)
    lane = lax.broadcasted_iota(jnp.int32, (1, LANES), 1)
    first_half = (lane % HEAD_DIM) < (ROT_DIM // 2)
    cos = cos_ref[...].reshape(tm, LANES)
    sin = sin_ref[...].reshape(tm, LANES)
    for c0 in range(0, ncols, tc):
        res = jnp.dot(h, w_ref[:, c0:c0 + tc], preferred_element_type=F32)
        parts = []
        for cc in range(tc // LANES):
            xc = res[:, cc * LANES:(cc + 1) * LANES]
            if rope_chunks[c0 // LANES + cc]:
                xc = _rope_chunk(xc, cos, sin, first_half)
            parts.append(xc.astype(o_ref.dtype))
        val = jnp.concatenate(parts, axis=1)
        for r in range(dil):
            o_ref[r, :, c0:c0 + tc] = val[r * n:(r + 1) * n]


def _norm_proj(x, g, w, cos, sin, *, dil, col_block, ncols, tc, rope_chunks, tm=512):
    B, S, D = x.shape
    n = tm // dil
    L = S // dil
    in_specs = [
        pl.BlockSpec((None, tm, D), lambda b, i: (b, i, 0)),
        pl.BlockSpec((1, D), lambda b, i: (0, 0)),
        pl.BlockSpec((D, ncols), lambda b, i: (0, col_block)),
        pl.BlockSpec((None, dil, n, LANES), lambda b, i: (b, 0, i, 0)),
        pl.BlockSpec((None, dil, n, LANES), lambda b, i: (b, 0, i, 0)),
    ]
    args = [x, g.reshape(1, D), w, cos, sin]
    if dil > 1:
        p = jnp.arange(tm)
        src = (p % n) * dil + p // n
        perm = (src[:, None] == jnp.arange(tm)[None, :]).astype(BF16)
        in_specs.append(pl.BlockSpec((tm, tm), lambda b, i: (0, 0)))
        args.append(perm)
    return pl.pallas_call(
        functools.partial(_proj_kernel, dil=dil, rope_chunks=rope_chunks, tc=tc),
        grid=(B, S // tm),
        in_specs=in_specs,
        out_specs=pl.BlockSpec((None, dil, n, ncols), lambda b, i: (b, 0, i, 0)),
        out_shape=jax.ShapeDtypeStruct((B, dil, L, ncols), BF16),
        compiler_params=_cparams(("parallel", "parallel")),
        name=f"norm_proj_d{dil}",
    )(*args)


def _fill_band_bias(bias_scr, tq, W, half):
    kk = lax.broadcasted_iota(jnp.int32, (W, 2 * tq), 0)
    qq = lax.broadcasted_iota(jnp.int32, (W, 2 * tq), 1) % tq
    for i in range(3):
        ok = jnp.abs(qq + i * half - kk) <= half
        bias_scr[i] = jnp.where(ok, 0.0, -jnp.inf).astype(F32)


def _band_scores(q2, kw, bias):
    lane_lo = lax.broadcasted_iota(jnp.int32, (1, LANES), 1) < HEAD_DIM
    zero = jnp.zeros_like(q2)
    qq = jnp.concatenate([jnp.where(lane_lo, q2, zero), jnp.where(lane_lo, zero, q2)], axis=0)
    qq = qq * jnp.asarray(HEAD_DIM ** -0.5, q2.dtype)
    return lax.dot_general(kw, qq, (((1,), (1,)), ((), ())), preferred_element_type=F32) + bias


def _band_finish(st, vw, want_lse, sink_row=None):
    tq = st.shape[1] // 2
    row_lo = lax.broadcasted_iota(jnp.int32, (LANES, 1), 0) < HEAD_DIM
    m = jnp.max(st, axis=0, keepdims=True)
    if sink_row is not None:
        m = jnp.maximum(m, sink_row)
    p = jnp.exp(st - m)
    l = jnp.sum(p, axis=0, keepdims=True)
    if sink_row is not None:
        l = l + jnp.exp(sink_row - m)
    ot = lax.dot_general(vw, p.astype(BF16), (((0,), (0,)), ((), ())), preferred_element_type=F32) / l
    o = jnp.where(row_lo, ot[:, :tq], ot[:, tq:]).T
    if not want_lse:
        return o, None
    lse_row = m + jnp.log(l)
    lse = jnp.where(row_lo, jnp.broadcast_to(lse_row[:, :tq], (LANES, tq)),
                    jnp.broadcast_to(lse_row[:, tq:], (LANES, tq))).T
    return o, lse


def _attn_a_kernel(*refs, TB, S):
    qkv = refs[:9]
    o_ref = refs[9]
    scr = refs[10:16]
    bias_scr = refs[16]
    t = pl.program_id(2)
    W = 4 * A_HALF

    @pl.when(t == 0)
    def _():
        _fill_band_bias(bias_scr, 128, W, A_HALF)

    for g, dil in enumerate(DILATIONS):
        q_ref, k_ref, v_ref = qkv[3 * g:3 * g + 3]
        o_scr, l_scr = scr[2 * g], scr[2 * g + 1]
        nq = TB // dil
        tq = min(nq, 128)
        nsub = nq // tq
        L = S // dil

        def body(it, carry, q_ref=q_ref, k_ref=k_ref, v_ref=v_ref, o_scr=o_scr, l_scr=l_scr,
                 dil=dil, nq=nq, tq=tq, nsub=nsub, L=L):
            pend = []
            for u in range(A_GROUP):
                idx = it * A_GROUP + u
                r = idx // nsub
                jj = idx % nsub
                qs = t * nq + jj * tq
                ws = pl.multiple_of(jnp.clip(qs - A_HALF, 0, L - W), A_HALF)
                q2 = q_ref[r, pl.ds(pl.multiple_of(jj * tq, tq), tq), :]
                st = _band_scores(q2, k_ref[r, pl.ds(ws, W), :], bias_scr[(qs - ws) // A_HALF])
                pend.append((st, r, jj, ws))
            for st, r, jj, ws in pend:
                o, lse = _band_finish(st, v_ref[r, pl.ds(ws, W), :], True)
                row0 = jj * (tq * dil) + r
                if dil == 1:
                    o_scr[pl.ds(pl.multiple_of(row0, tq), tq), :] = o
                    l_scr[pl.ds(pl.multiple_of(row0, tq), tq), :] = lse
                else:
                    o_scr[pl.ds(row0, tq, stride=dil), :] = o
                    l_scr[pl.ds(row0, tq, stride=dil), :] = lse
            return carry

        lax.fori_loop(0, dil * nsub // A_GROUP, body, 0)

    l0, l1, l2 = scr[1][...], scr[3][...], scr[5][...]
    mx = jnp.maximum(jnp.maximum(l0, l1), l2)
    e0, e1, e2 = jnp.exp(l0 - mx), jnp.exp(l1 - mx), jnp.exp(l2 - mx)
    den = e0 + e1 + e2
    o = (e0 * scr[0][...] + e1 * scr[2][...] + e2 * scr[4][...]) / den
    o_ref[...] = o.astype(o_ref.dtype)


def _attn_a(qkvs, B, S, TB=2048):
    n_pairs = (16 * HEAD_DIM) // LANES
    in_specs, args = [], []
    for g, dil in enumerate(DILATIONS):
        L = S // dil
        in_specs += [
            pl.BlockSpec((None, dil, TB // dil, LANES), lambda b, hp, t: (b, 0, t, hp)),
            pl.BlockSpec((None, dil, L, LANES), lambda b, hp, t: (b, 0, 0, n_pairs + hp)),
            pl.BlockSpec((None, dil, L, LANES), lambda b, hp, t: (b, 0, 0, 2 * n_pairs + hp)),
        ]
        args += [qkvs[g]] * 3
    return pl.pallas_call(
        functools.partial(_attn_a_kernel, TB=TB, S=S),
        grid=(B, n_pairs, S // TB),
        in_specs=in_specs,
        out_specs=pl.BlockSpec((None, TB, LANES), lambda b, hp, t: (b, t, hp)),
        out_shape=jax.ShapeDtypeStruct((B, S, n_pairs * LANES), BF16),
        scratch_shapes=[pltpu.VMEM((TB, LANES), F32)] * 6 + [pltpu.VMEM((3, 4 * A_HALF, 256), F32)],
        compiler_params=_cparams(("parallel", "parallel", "arbitrary")),
        name="attn_dilated",
    )(*args)


def _attn_c_kernel(q_ref, k_ref, v_ref, sink_ref, o_ref, bias_scr, *, TB, S):
    t = pl.program_id(1)
    tq = 128
    W = tq + 2 * C_HALF
    n_pairs = q_ref.shape[1] // LANES

    @pl.when(t == 0)
    def _():
        _fill_band_bias(bias_scr, tq, W, C_HALF)

    def body(jj, carry):
        qs = t * TB + jj * tq
        ws = pl.multiple_of(jnp.clip(qs - C_HALF, 0, S - W), C_HALF)
        kw = k_ref[pl.ds(ws, W), :]
        vw = v_ref[pl.ds(ws, W), :]
        bias = bias_scr[(qs - ws) // C_HALF]
        rows = pl.ds(pl.multiple_of(jj * tq, tq), tq)
        for p0 in range(0, n_pairs, A_GROUP):
            sts = [_band_scores(q_ref[rows, p * LANES:(p + 1) * LANES], kw, bias)
                   for p in range(p0, p0 + A_GROUP)]
            for p, st in zip(range(p0, p0 + A_GROUP), sts):
                sink_row = sink_ref[p:p + 1, :]
                o, _ = _band_finish(st, vw, False, sink_row=sink_row)
                o_ref[rows, p * LANES:(p + 1) * LANES] = o.astype(o_ref.dtype)
        return carry

    lax.fori_loop(0, TB // tq, body, 0)


def _attn_c(qkv, sink_tab, B, S, TB=1024):
    nq = 16 * HEAD_DIM
    return pl.pallas_call(
        functools.partial(_attn_c_kernel, TB=TB, S=S),
        grid=(B, S // TB),
        in_specs=[
            pl.BlockSpec((None, TB, nq), lambda b, t: (b, t, 0)),
            pl.BlockSpec((None, S, LANES), lambda b, t: (b, 0, nq // LANES)),
            pl.BlockSpec((None, S, LANES), lambda b, t: (b, 0, nq // LANES + 1)),
            pl.BlockSpec((8, 2 * LANES), lambda b, t: (0, 0)),
        ],
        out_specs=pl.BlockSpec((None, TB, nq), lambda b, t: (b, t, 0)),
        out_shape=jax.ShapeDtypeStruct((B, S, nq), BF16),
        scratch_shapes=[pltpu.VMEM((3, LANES + 2 * C_HALF, 2 * LANES), F32)],
        compiler_params=_cparams(("parallel", "arbitrary")),
        name="attn_swa_sink",
    )(qkv, qkv, qkv, sink_tab)


def _attn_b_kernel(lam_ref, q_ref, qn_ref, k_ref, kn_ref, v_ref, subln_ref, o_ref,
                   s0_scr, s1_scr, m_scr, l_scr, acc_scr, *, lambda_init):
    i = pl.program_id(2)
    kv = pl.program_id(3)
    nkv = pl.num_programs(3)
    tk = k_ref.shape[0]
    lane = lax.broadcasted_iota(jnp.int32, (1, LANES), 1)
    lane_lo = lane < HEAD_DIM

    def scores(kref, qref, c):
        k = kref[...] * jnp.asarray(HEAD_DIM ** -0.5, kref.dtype)
        sel = lane_lo if c == 0 else jnp.logical_not(lane_lo)
        kc = jnp.where(sel, k, jnp.zeros_like(k))
        return lax.dot_general(kc, qref[...], (((1,), (1,)), ((), ())), preferred_element_type=F32)

    @pl.when(kv == 0)
    def _():
        m_scr[...] = jnp.full(m_scr.shape, -jnp.inf, F32)
        l_scr[...] = jnp.zeros(l_scr.shape, F32)
        acc_scr[...] = jnp.zeros(acc_scr.shape, F32)

    @pl.when(jnp.logical_and(i == 0, kv == 0))
    def _():
        for c in range(2):
            s0_scr[c * tk:(c + 1) * tk, :] = scores(k_ref, q_ref, c)

    def step(cur_scr, nxt_scr):
        vt = jnp.concatenate([v_ref[...].T, jnp.ones((16, tk), BF16)], axis=0)
        for c in range(2):
            nxt_scr[c * tk:(c + 1) * tk, :] = scores(kn_ref, qn_ref, c)
            s = cur_scr[c * tk:(c + 1) * tk, :]
            m_old = m_scr[c]
            m_new = jnp.maximum(m_old, jnp.max(s, axis=0, keepdims=True))
            alpha = jnp.exp(m_old - m_new)
            p = jnp.exp(s - m_new).astype(BF16)
            pv = jnp.dot(vt, p, preferred_element_type=F32)
            l_scr[c] = alpha * l_scr[c] + pv[LANES:LANES + 1]
            acc_scr[c] = alpha * acc_scr[c] + pv[:LANES]
            m_scr[c] = m_new

    @pl.when(kv % 2 == 0)
    def _():
        step(s0_scr, s1_scr)

    @pl.when(kv % 2 == 1)
    def _():
        step(s1_scr, s0_scr)

    @pl.when(kv == nkv - 1)
    def _():
        lp = lam_ref[...]
        lam = (jnp.exp(jnp.sum(lp[0:1] * lp[1:2], axis=-1, keepdims=True))
               - jnp.exp(jnp.sum(lp[2:3] * lp[3:4], axis=-1, keepdims=True)) + lambda_init)
        o = acc_scr[0] / l_scr[0] - lam * (acc_scr[1] / l_scr[1])
        ms = jnp.mean(o * o, axis=0, keepdims=True)
        o = o * lax.rsqrt(ms + SUBLN_EPS) * subln_ref[...] * (1.0 - lambda_init)
        o_ref[...] = o.T.astype(o_ref.dtype)


def _attn_b(qkv, lam_params, subln, lambda_init, B, S, tq=1024, tk=2048):
    H = 8
    nq, nkv = S // tq, S // tk
    assert nkv % 2 == 0

    def q_next(b, h, i, kv):
        return (b, jnp.minimum(i + (kv + 1) // nkv, nq - 1), h)

    def k_next(b, h, i, kv):
        return (b, (kv + 1) % nkv, H + h)

    return pl.pallas_call(
        functools.partial(_attn_b_kernel, lambda_init=lambda_init),
        grid=(B, H, nq, nkv),
        in_specs=[
            pl.BlockSpec((4, HEAD_DIM), lambda b, h, i, kv: (0, 0)),
            pl.BlockSpec((None, tq, LANES), lambda b, h, i, kv: (b, i, h)),
            pl.BlockSpec((None, tq, LANES), q_next),
            pl.BlockSpec((None, tk, LANES), lambda b, h, i, kv: (b, kv, H + h)),
            pl.BlockSpec((None, tk, LANES), k_next),
            pl.BlockSpec((None, tk, LANES), lambda b, h, i, kv: (b, kv, 2 * H + h)),
            pl.BlockSpec((LANES, 1), lambda b, h, i, kv: (0, 0)),
        ],
        out_specs=pl.BlockSpec((None, tq, LANES), lambda b, h, i, kv: (b, i, h)),
        out_shape=jax.ShapeDtypeStruct((B, S, H * LANES), BF16),
        scratch_shapes=[pltpu.VMEM((2 * tk, tq), F32), pltpu.VMEM((2 * tk, tq), F32),
                        pltpu.VMEM((2, 1, tq), F32), pltpu.VMEM((2, 1, tq), F32),
                        pltpu.VMEM((2, LANES, tq), F32)],
        compiler_params=_cparams(("parallel", "parallel", "arbitrary", "arbitrary")),
        name="attn_diff",
    )(lam_params, qkv, qkv, qkv, qkv, qkv, subln.reshape(LANES, 1))


def _out_proj_kernel(x_ref, o_ref, w_ref, y_ref):
    y_ref[...] = x_ref[...] + jnp.dot(o_ref[...], w_ref[...], preferred_element_type=F32)


def _out_proj(x2, o2, w, tm=1024):
    T, D = x2.shape
    K = o2.shape[1]
    return pl.pallas_call(
        _out_proj_kernel,
        grid=(T // tm,),
        in_specs=[pl.BlockSpec((tm, D), lambda i: (i, 0)),
                  pl.BlockSpec((tm, K), lambda i: (i, 0)),
                  pl.BlockSpec((K, D), lambda i: (0, 0))],
        out_specs=pl.BlockSpec((tm, D), lambda i: (i, 0)),
        out_shape=jax.ShapeDtypeStruct((T, D), F32),
        compiler_params=_cparams(("parallel",)),
        name="out_proj",
    )(x2, o2, w)


def _ffn_kernel(x_ref, g_ref, w1_ref, w3_ref, w2_ref, y_ref, h_scr, acc_scr):
    f = pl.program_id(1)

    @pl.when(f == 0)
    def _():
        x = x_ref[...]
        ms = jnp.mean(x * x, axis=-1, keepdims=True)
        h_scr[...] = (x * lax.rsqrt(ms + NORM_EPS) * g_ref[...]).astype(BF16)
        acc_scr[...] = x

    h = h_scr[...]
    u = jnp.dot(h, w1_ref[...], preferred_element_type=F32)
    v = jnp.dot(h, w3_ref[...], preferred_element_type=F32)
    a = (u * jax.nn.sigmoid(u) * v).astype(BF16)
    acc_scr[...] += jnp.dot(a, w2_ref[...], preferred_element_type=F32)

    @pl.when(f == pl.num_programs(1) - 1)
    def _():
        y_ref[...] = acc_scr[...]


def _ffn_dense(x2, g, w1, w3, w2, tm=512):
    T, D = x2.shape
    F = w1.shape[1]
    tf = F
    return pl.pallas_call(
        _ffn_kernel,
        grid=(T // tm, F // tf),
        in_specs=[pl.BlockSpec((tm, D), lambda i, f: (i, 0)),
                  pl.BlockSpec((1, D), lambda i, f: (0, 0)),
                  pl.BlockSpec((D, tf), lambda i, f: (0, f)),
                  pl.BlockSpec((D, tf), lambda i, f: (0, f)),
                  pl.BlockSpec((tf, D), lambda i, f: (f, 0))],
        out_specs=pl.BlockSpec((tm, D), lambda i, f: (i, 0)),
        out_shape=jax.ShapeDtypeStruct((T, D), F32),
        scratch_shapes=[pltpu.VMEM((tm, D), BF16), pltpu.VMEM((tm, D), F32)],
        compiler_params=_cparams(("parallel", "arbitrary")),
        name="ffn_dense",
    )(x2, g.reshape(1, D), w1, w3, w2)


def _router_kernel(x_ref, g_ref, r_ref, h_ref, mi_ref, mf_ref, cnt_ref, carry_scr):
    i = pl.program_id(0)
    tm = x_ref.shape[0]

    @pl.when(i == 0)
    def _():
        carry_scr[...] = jnp.zeros(carry_scr.shape, F32)

    x = x_ref[...]
    ms = jnp.mean(x * x, axis=-1, keepdims=True)
    h = x * lax.rsqrt(ms + NORM_EPS) * g_ref[...]
    h_ref[...] = h
    lane = lax.broadcasted_iota(jnp.int32, (tm, LANES), 1)
    logits = jnp.dot(h, r_ref[...], preferred_element_type=F32, precision=lax.Precision.HIGHEST)
    logits = jnp.where(lane < N_EXPERTS, logits, -jnp.inf)
    v0 = jnp.max(logits, axis=-1, keepdims=True)
    i0 = jnp.min(jnp.where(logits == v0, lane, LANES), axis=-1, keepdims=True)
    rest = jnp.where(lane == i0, -jnp.inf, logits)
    v1 = jnp.max(rest, axis=-1, keepdims=True)
    i1 = jnp.min(jnp.where(rest == v1, lane, LANES), axis=-1, keepdims=True)
    tt = jnp.exp(v1 - v0)
    g0 = 1.0 / (1.0 + tt)
    g1 = tt / (1.0 + tt)
    sel0 = lane == i0
    sel1 = lane == i1
    onehot = jnp.where(jnp.logical_or(sel0, sel1), 1.0, 0.0)
    row = lax.broadcasted_iota(jnp.int32, (tm, tm), 0)
    col = lax.broadcasted_iota(jnp.int32, (tm, tm), 1)
    tri = jnp.where(row > col, 1.0, 0.0).astype(BF16)
    before = carry_scr[...] + jnp.dot(tri, onehot.astype(BF16), preferred_element_type=F32)
    rank0 = jnp.sum(jnp.where(sel0, before, 0.0), axis=-1, keepdims=True)
    rank1 = jnp.sum(jnp.where(sel1, before, 0.0), axis=-1, keepdims=True)
    carry_scr[...] = carry_scr[...] + jnp.sum(onehot, axis=0, keepdims=True)
    mi = jnp.where(lane == 0, i0, jnp.where(lane == 1, i1, 0))
    mi = jnp.where(lane == 2, rank0.astype(jnp.int32), jnp.where(lane == 3, rank1.astype(jnp.int32), mi))
    mi_ref[...] = mi
    mf_ref[...] = jnp.where(lane == 0, g0, jnp.where(lane == 1, g1, 0.0))
    cnt_ref[...] = carry_scr[...]


def _router(x2, g, router, tm=512):
    T, D = x2.shape
    rpad = jnp.zeros((D, LANES), F32).at[:, :N_EXPERTS].set(router)
    return pl.pallas_call(
        _router_kernel,
        grid=(T // tm,),
        in_specs=[pl.BlockSpec((tm, D), lambda i: (i, 0)),
                  pl.BlockSpec((1, D), lambda i: (0, 0)),
                  pl.BlockSpec((D, LANES), lambda i: (0, 0))],
        out_specs=[pl.BlockSpec((tm, D), lambda i: (i, 0)),
                   pl.BlockSpec((tm, LANES), lambda i: (i, 0)),
                   pl.BlockSpec((tm, LANES), lambda i: (i, 0)),
                   pl.BlockSpec((1, LANES), lambda i: (0, 0))],
        out_shape=[jax.ShapeDtypeStruct((T, D), F32),
                   jax.ShapeDtypeStruct((T, LANES), jnp.int32),
                   jax.ShapeDtypeStruct((T, LANES), F32),
                   jax.ShapeDtypeStruct((1, LANES), F32)],
        scratch_shapes=[pltpu.VMEM((1, LANES), F32)],
        compiler_params=_cparams(("arbitrary",)),
        name="moe_router",
    )(x2, g.reshape(1, D), rpad)


def _dispatch_kernel(dest_ref, h_ref, buf_in, buf_hbm, sem, *, tm):
    del buf_in

    def copy(n, slot):
        return pltpu.make_async_copy(h_ref.at[pl.ds(n, 1)], buf_hbm.at[pl.ds(slot, 1)], sem)

    def issue(n, carry):
        copy(n, dest_ref[2 * n]).start(priority=0)
        copy(n, dest_ref[2 * n + 1]).start(priority=1)
        return carry

    lax.fori_loop(0, tm, issue, 0)

    def drain(n, carry):
        copy(0, 0).wait()
        copy(0, 0).wait()
        return carry

    lax.fori_loop(0, tm, drain, 0)


def _dispatch(h2, dest_flat, n_rows, tm=512):
    T, D = h2.shape
    buf0 = jnp.zeros((n_rows, D), F32)
    return pl.pallas_call(
        functools.partial(_dispatch_kernel, tm=tm),
        grid=(T // tm,),
        in_specs=[pl.BlockSpec((2 * tm,), lambda i: (i,), memory_space=pltpu.SMEM),
                  pl.BlockSpec((tm, D), lambda i: (i, 0)),
                  pl.BlockSpec(memory_space=pl.ANY)],
        out_specs=pl.BlockSpec(memory_space=pl.ANY),
        out_shape=jax.ShapeDtypeStruct((n_rows, D), F32),
        scratch_shapes=[pltpu.SemaphoreType.DMA(())],
        input_output_aliases={2: 0},
        compiler_params=_cparams(("arbitrary",)),
        name="moe_dispatch",
    )(dest_flat, h2, buf0)


def _moe_ffn_kernel(be_ref, nu_ref, x_ref, w1_ref, w3_ref, w2_ref, y_ref, acc_scr):
    b = pl.program_id(0)
    f = pl.program_id(1)

    @pl.when(b < nu_ref[0])
    def _():
        @pl.when(f == 0)
        def _():
            acc_scr[...] = jnp.zeros(acc_scr.shape, F32)

        h = x_ref[...].astype(BF16)
        u = jnp.dot(h, w1_ref[...], preferred_element_type=F32)
        v = jnp.dot(h, w3_ref[...], preferred_element_type=F32)
        a = (u * jax.nn.sigmoid(u) * v).astype(BF16)
        acc_scr[...] += jnp.dot(a, w2_ref[...], preferred_element_type=F32)

        @pl.when(f == pl.num_programs(1) - 1)
        def _():
            y_ref[...] = acc_scr[...]

    @pl.when(jnp.logical_and(b >= nu_ref[0], f == 0))
    def _():
        y_ref[...] = jnp.zeros(y_ref.shape, F32)


def _moe_ffn(buf, block_expert, n_used, w1, w3, w2, tf=1792):
    R, D = buf.shape
    E, _, F = w1.shape
    nb = R // MOE_BLOCK

    def row_map(b, f, be, nu):
        return (jnp.minimum(b, nu[0] - 1), 0)

    def w_in_map(b, f, be, nu):
        live = b < nu[0]
        return (be[jnp.minimum(b, nu[0] - 1)], 0, jnp.where(live, f, F // tf - 1))

    def w_out_map(b, f, be, nu):
        live = b < nu[0]
        return (be[jnp.minimum(b, nu[0] - 1)], jnp.where(live, f, F // tf - 1), 0)

    return pl.pallas_call(
        _moe_ffn_kernel,
        grid_spec=pltpu.PrefetchScalarGridSpec(
            num_scalar_prefetch=2,
            grid=(nb, F // tf),
            in_specs=[pl.BlockSpec((MOE_BLOCK, D), row_map),
                      pl.BlockSpec((None, D, tf), w_in_map),
                      pl.BlockSpec((None, D, tf), w_in_map),
                      pl.BlockSpec((None, tf, D), w_out_map)],
            out_specs=pl.BlockSpec((MOE_BLOCK, D), lambda b, f, be, nu: (b, 0)),
            scratch_shapes=[pltpu.VMEM((MOE_BLOCK, D), F32)],
        ),
        out_shape=jax.ShapeDtypeStruct((R, D), F32),
        compiler_params=_cparams(("arbitrary", "arbitrary")),
        name="moe_ffn",
    )(block_expert, n_used, buf, w1, w3, w2)


def _combine_kernel(dest_ref, x_ref, mf_ref, g_ref, ybuf_hbm, y_ref, rows_scr, sem, *, tm, final_norm):
    def copy(slot, k, n):
        return pltpu.make_async_copy(ybuf_hbm.at[pl.ds(slot, 1)], rows_scr.at[k, pl.ds(n, 1)], sem)

    def issue(n, carry):
        copy(dest_ref[2 * n], 0, n).start(priority=0)
        copy(dest_ref[2 * n + 1], 1, n).start(priority=1)
        return carry

    lax.fori_loop(0, tm, issue, 0)

    def drain(n, carry):
        copy(0, 0, 0).wait()
        copy(0, 1, 0).wait()
        return carry

    lax.fori_loop(0, tm, drain, 0)

    mf = mf_ref[...]
    y = x_ref[...] + mf[:, 0:1] * rows_scr[0] + mf[:, 1:2] * rows_scr[1]
    if final_norm:
        ms = jnp.mean(y * y, axis=-1, keepdims=True)
        y = y * lax.rsqrt(ms + NORM_EPS) * g_ref[...]
    y_ref[...] = y


def _combine(x2, mf, dest_flat, ybuf, g_final, final_norm, tm=512):
    T, D = x2.shape
    return pl.pallas_call(
        functools.partial(_combine_kernel, tm=tm, final_norm=final_norm),
        grid=(T // tm,),
        in_specs=[pl.BlockSpec((2 * tm,), lambda i: (i,), memory_space=pltpu.SMEM),
                  pl.BlockSpec((tm, D), lambda i: (i, 0)),
                  pl.BlockSpec((tm, LANES), lambda i: (i, 0)),
                  pl.BlockSpec((1, D), lambda i: (0, 0)),
                  pl.BlockSpec(memory_space=pl.ANY)],
        out_specs=pl.BlockSpec((tm, D), lambda i: (i, 0)),
        out_shape=jax.ShapeDtypeStruct((T, D), F32),
        scratch_shapes=[pltpu.VMEM((2, tm, D), F32), pltpu.SemaphoreType.DMA(())],
        compiler_params=_cparams(("arbitrary",)),
        name="moe_combine",
    )(dest_flat, x2, mf, g_final.reshape(1, D), ybuf)


def _moe_layer(x2, g, router, w1, w3, w2, g_final, final_norm):
    T, D = x2.shape
    h2, mi, mf, cnt = _router(x2, g, router)
    counts = cnt[0, :N_EXPERTS].astype(jnp.int32)
    padded = ((counts + MOE_BLOCK - 1) // MOE_BLOCK) * MOE_BLOCK
    pend = jnp.cumsum(padded)
    pstart = pend - padded
    dest = pstart[mi[:, 0:2]] + mi[:, 2:4]
    dest_flat = dest.reshape(-1).astype(jnp.int32)
    nb = -(-(2 * T + N_EXPERTS * (MOE_BLOCK - 1)) // MOE_BLOCK)
    block_start = jnp.arange(nb, dtype=jnp.int32) * MOE_BLOCK
    block_expert = jnp.minimum(jnp.sum(pend[None, :] <= block_start[:, None], axis=1),
                               N_EXPERTS - 1).astype(jnp.int32)
    n_used = (pend[-1:] // MOE_BLOCK).astype(jnp.int32)
    buf = _dispatch(h2, dest_flat, nb * MOE_BLOCK)
    ybuf = _moe_ffn(buf, block_expert, n_used, w1, w3, w2)
    return _combine(x2, mf, dest_flat, ybuf, g_final, final_norm)


def _mixer_a(x, g, w_in, w_out, tabs):
    B, S, D = x.shape
    w = w_in.astype(BF16)
    qkvs = []
    for gi, dil in enumerate(DILATIONS):
        cos, sin = tabs[dil]
        qkvs.append(_norm_proj(x, g, w, cos, sin, dil=dil, col_block=gi, ncols=3 * D, tc=D,
                               rope_chunks=_QKV_ROPE_CHUNKS))
    o = _attn_a(qkvs, B, S)
    return _out_proj(x.reshape(B * S, D), o.reshape(B * S, D), w_out.astype(BF16)).reshape(B, S, D)


def _mixer_b(x, g, w_in, lq1, lk1, lq2, lk2, subln, w_out, lambda_init, tabs):
    B, S, D = x.shape
    cos, sin = tabs[1]
    qkv = _norm_proj(x, g, w_in.astype(BF16), cos, sin, dil=1, col_block=0, ncols=3 * D, tc=D,
                     rope_chunks=_QKV_ROPE_CHUNKS).reshape(B, S, 3 * D)
    lam_params = jnp.stack([lq1, lk1, lq2, lk2]).astype(F32)
    o = _attn_b(qkv, lam_params, subln.astype(F32), lambda_init, B, S)
    return _out_proj(x.reshape(B * S, D), o.reshape(B * S, D), w_out.astype(BF16)).reshape(B, S, D)


def _mixer_c(x, g, w_in, sink, w_out, tabs):
    B, S, D = x.shape
    cos, sin = tabs[1]
    head_order = jnp.arange(16).reshape(2, 8).T.reshape(-1)
    col_order = (head_order[:, None] * HEAD_DIM + jnp.arange(HEAD_DIM)[None, :]).reshape(-1)
    w_q = w_in[:, :D][:, col_order]
    w = jnp.concatenate([w_q, w_in[:, D:]], axis=1).astype(BF16)
    ncols = w.shape[1]
    chunks = tuple(c < (ncols // LANES - 1) for c in range(ncols // LANES))
    qkv = _norm_proj(x, g, w, cos, sin, dil=1, col_block=0, ncols=ncols, tc=ncols,
                     rope_chunks=chunks).reshape(B, S, ncols)
    sk = sink.astype(F32)
    sink_tab = jnp.concatenate([jnp.broadcast_to(sk[:8, None], (8, LANES)),
                                jnp.broadcast_to(sk[8:, None], (8, LANES))], axis=1)
    o = _attn_c(qkv, sink_tab, B, S)
    w_o = w_out[col_order, :].astype(BF16)
    return _out_proj(x.reshape(B * S, D), o.reshape(B * S, D), w_o).reshape(B, S, D)


def kernel(x, positions, l0_norm_mix, l0_a_w_in, l0_a_w_out, l0_norm_ffn, l0_ffn_w1, l0_ffn_w3, l0_ffn_w2, l1_norm_mix, l1_b_w_in, l1_b_lambda_q1, l1_b_lambda_k1, l1_b_lambda_q2, l1_b_lambda_k2, l1_b_subln, l1_b_w_out, l1_norm_ffn, l1_moe_router, l1_moe_w1, l1_moe_w3, l1_moe_w2, l2_norm_mix, l2_c_w_in, l2_c_sink, l2_c_w_out, l2_norm_ffn, l2_ffn_w1, l2_ffn_w3, l2_ffn_w2, l3_norm_mix, l3_a_w_in, l3_a_w_out, l3_norm_ffn, l3_moe_router, l3_moe_w1, l3_moe_w3, l3_moe_w2, final_norm):
    B, S, D = x.shape
    T = B * S
    cos, sin = _rope_tables(positions)
    tabs = {d: (_to_strided(cos, d), _to_strided(sin, d)) for d in DILATIONS}

    x = _mixer_a(x, l0_norm_mix, l0_a_w_in, l0_a_w_out, tabs)
    x = _ffn_dense(x.reshape(T, D), l0_norm_ffn, l0_ffn_w1.astype(BF16), l0_ffn_w3.astype(BF16),
                   l0_ffn_w2.astype(BF16)).reshape(B, S, D)
    lambda_init = 0.8 - 0.6 * math.exp(-0.3 * 1)
    x = _mixer_b(x, l1_norm_mix, l1_b_w_in, l1_b_lambda_q1, l1_b_lambda_k1, l1_b_lambda_q2,
                 l1_b_lambda_k2, l1_b_subln, l1_b_w_out, lambda_init, tabs)
    x = _moe_layer(x.reshape(T, D), l1_norm_ffn, l1_moe_router, l1_moe_w1.astype(BF16),
                   l1_moe_w3.astype(BF16), l1_moe_w2.astype(BF16), final_norm, False).reshape(B, S, D)
    x = _mixer_c(x, l2_norm_mix, l2_c_w_in, l2_c_sink, l2_c_w_out, tabs)
    x = _ffn_dense(x.reshape(T, D), l2_norm_ffn, l2_ffn_w1.astype(BF16), l2_ffn_w3.astype(BF16),
                   l2_ffn_w2.astype(BF16)).reshape(B, S, D)
    x = _mixer_a(x, l3_norm_mix, l3_a_w_in, l3_a_w_out, tabs)
    x = _moe_layer(x.reshape(T, D), l3_norm_ffn, l3_moe_router, l3_moe_w1.astype(BF16),
                   l3_moe_w3.astype(BF16), l3_moe_w2.astype(BF16), final_norm, True).reshape(B, S, D)
    return x
```

```python
import functools
import math

import jax
import jax.numpy as jnp
from jax import lax
from jax.experimental import pallas as pl
from jax.experimental.pallas import tpu as pltpu

F32 = jnp.float32
BF16 = jnp.bfloat16

D_MODEL = 1024
HEAD_DIM = 64
ROT_DIM = HEAD_DIM // 4
ROPE_THETA = 500000.0
NORM_EPS = 1e-6
SUBLN_EPS = 1e-5
DILATIONS = (1, 4, 16)
A_HALF = 64
C_HALF = 128
N_EXPERTS = 8
LANES = 128
VMEM_LIMIT = 56 * 1024 * 1024

MOE_BLOCK = 512
_QKV_ROPE_CHUNKS = (True,) * 16 + (False,) * 8
A_GROUP = 8


def _cparams(sem):
    return pltpu.CompilerParams(dimension_semantics=sem, vmem_limit_bytes=VMEM_LIMIT)


def _rope_tables(positions):
    lane = jnp.arange(LANES) % HEAD_DIM
    half = ROT_DIM // 2
    inv_freq = ROPE_THETA ** (-((2 * (lane % half)).astype(F32) / ROT_DIM))
    freq = jnp.where(lane < ROT_DIM, inv_freq, 0.0)
    sign = jnp.where(lane < half, -1.0, jnp.where(lane < ROT_DIM, 1.0, 0.0))
    ang = positions.astype(F32)[..., None] * freq
    return jnp.cos(ang), jnp.sin(ang) * sign


def _to_strided(t, d):
    B, S, C = t.shape
    return jnp.swapaxes(t.reshape(B, S // d, d, C), 1, 2)


def _rope_chunk(x, cos, sin, first_half):
    partner = jnp.where(first_half, pltpu.roll(x, LANES - ROT_DIM // 2, 1),
                        pltpu.roll(x, ROT_DIM // 2, 1))
    return x * cos + partner * sin


def _proj_kernel(*refs, dil, rope_chunks, tc):
    if dil > 1:
        x_ref, g_ref, w_ref, cos_ref, sin_ref, perm_ref, o_ref = refs
    else:
        x_ref, g_ref, w_ref, cos_ref, sin_ref, o_ref = refs
    tm = x_ref.shape[0]
    ncols = w_ref.shape[1]
    n = tm // dil
    x = x_ref[...]
    ms = jnp.mean(x * x, axis=-1, keepdims=True)
    h = (x * lax.rsqrt(ms + NORM_EPS) * g_ref[...]).astype(BF16)
    if dil > 1:
        h = jnp.dot(perm_ref[...], h, preferred_element_type=F32).astype(BF16)
    lane = lax.broadcasted_iota(jnp.int32, (1, LANES), 1)
    first_half = (lane % HEAD_DIM) < (ROT_DIM // 2)
    cos = cos_ref[...].reshape(tm, LANES)
    sin = sin_ref[...].reshape(tm, LANES)
    for c0 in range(0, ncols, tc):
        res = jnp.dot(h, w_ref[:, c0:c0 + tc], preferred_element_type=F32)
        parts = []
        for cc in range(tc // LANES):
            xc = res[:, cc * LANES:(cc + 1) * LANES]
            if rope_chunks[c0 // LANES + cc]:
                xc = _rope_chunk(xc, cos, sin, first_half)
            parts.append(xc.astype(o_ref.dtype))
        val = jnp.concatenate(parts, axis=1)
        for r in range(dil):
            o_ref[r, :, c0:c0 + tc] = val[r * n:(r + 1) * n]


def _norm_proj(x, g, w, cos, sin, *, dil, col_block, ncols, tc, rope_chunks, tm=512):
    B, S, D = x.shape
    n = tm // dil
    L = S // dil
    in_specs = [
        pl.BlockSpec((None, tm, D), lambda b, i: (b, i, 0)),
        pl.BlockSpec((1, D), lambda b, i: (0, 0)),
        pl.BlockSpec((D, ncols), lambda b, i: (0, col_block)),
        pl.BlockSpec((None, dil, n, LANES), lambda b, i: (b, 0, i, 0)),
        pl.BlockSpec((None, dil, n, LANES), lambda b, i: (b, 0, i, 0)),
    ]
    args = [x, g.reshape(1, D), w, cos, sin]
    if dil > 1:
        p = jnp.arange(tm)
        src = (p % n) * dil + p // n
        perm = (src[:, None] == jnp.arange(tm)[None, :]).astype(BF16)
        in_specs.append(pl.BlockSpec((tm, tm), lambda b, i: (0, 0)))
        args.append(perm)
    return pl.pallas_call(
        functools.partial(_proj_kernel, dil=dil, rope_chunks=rope_chunks, tc=tc),
        grid=(B, S // tm),
        in_specs=in_specs,
        out_specs=pl.BlockSpec((None, dil, n, ncols), lambda b, i: (b, 0, i, 0)),
        out_shape=jax.ShapeDtypeStruct((B, dil, L, ncols), BF16),
        compiler_params=_cparams(("parallel", "parallel")),
        name=f"norm_proj_d{dil}",
    )(*args)


def _fill_band_bias(bias_scr, tq, W, half):
    kk = lax.broadcasted_iota(jnp.int32, (W, 2 * tq), 0)
    qq = lax.broadcasted_iota(jnp.int32, (W, 2 * tq), 1) % tq
    for i in range(3):
        ok = jnp.abs(qq + i * half - kk) <= half
        bias_scr[i] = jnp.where(ok, 0.0, -jnp.inf).astype(F32)


def _band_scores(q2, kw, bias):
    lane_lo = lax.broadcasted_iota(jnp.int32, (1, LANES), 1) < HEAD_DIM
    zero = jnp.zeros_like(q2)
    qq = jnp.concatenate([jnp.where(lane_lo, q2, zero), jnp.where(lane_lo, zero, q2)], axis=0)
    qq = qq * jnp.asarray(HEAD_DIM ** -0.5, q2.dtype)
    return lax.dot_general(kw, qq, (((1,), (1,)), ((), ())), preferred_element_type=F32) + bias


def _band_finish(st, vw, want_lse, sink_row=None):
    tq = st.shape[1] // 2
    row_lo = lax.broadcasted_iota(jnp.int32, (LANES, 1), 0) < HEAD_DIM
    m = jnp.max(st, axis=0, keepdims=True)
    if sink_row is not None:
        m = jnp.maximum(m, sink_row)
    p = jnp.exp(st - m)
    l = jnp.sum(p, axis=0, keepdims=True)
    if sink_row is not None:
        l = l + jnp.exp(sink_row - m)
    ot = lax.dot_general(vw, p.astype(BF16), (((0,), (0,)), ((), ())), preferred_element_type=F32) / l
    o = jnp.where(row_lo, ot[:, :tq], ot[:, tq:]).T
    if not want_lse:
        return o, None
    lse_row = m + jnp.log(l)
    lse = jnp.where(row_lo, jnp.broadcast_to(lse_row[:, :tq], (LANES, tq)),
                    jnp.broadcast_to(lse_row[:, tq:], (LANES, tq))).T
    return o, lse


def _attn_a_kernel(*refs, TB, S):
    qkv = refs[:9]
    o_ref = refs[9]
    scr = refs[10:16]
    bias_scr = refs[16]
    t = pl.program_id(2)
    W = 4 * A_HALF

    @pl.when(t == 0)
    def _():
        _fill_band_bias(bias_scr, 128, W, A_HALF)

    for g, dil in enumerate(DILATIONS):
        q_ref, k_ref, v_ref = qkv[3 * g:3 * g + 3]
        o_scr, l_scr = scr[2 * g], scr[2 * g + 1]
        nq = TB // dil
        tq = min(nq, 128)
        nsub = nq // tq
        L = S // dil

        def body(it, carry, q_ref=q_ref, k_ref=k_ref, v_ref=v_ref, o_scr=o_scr, l_scr=l_scr,
                 dil=dil, nq=nq, tq=tq, nsub=nsub, L=L):
            pend = []
            for u in range(A_GROUP):
                idx = it * A_GROUP + u
                r = idx // nsub
                jj = idx % nsub
                qs = t * nq + jj * tq
                ws = pl.multiple_of(jnp.clip(qs - A_HALF, 0, L - W), A_HALF)
                q2 = q_ref[r, pl.ds(pl.multiple_of(jj * tq, tq), tq), :]
                st = _band_scores(q2, k_ref[r, pl.ds(ws, W), :], bias_scr[(qs - ws) // A_HALF])
                pend.append((st, r, jj, ws))
            for st, r, jj, ws in pend:
                o, lse = _band_finish(st, v_ref[r, pl.ds(ws, W), :], True)
                row0 = jj * (tq * dil) + r
                if dil == 1:
                    o_scr[pl.ds(pl.multiple_of(row0, tq), tq), :] = o
                    l_scr[pl.ds(pl.multiple_of(row0, tq), tq), :] = lse
                else:
                    o_scr[pl.ds(row0, tq, stride=dil), :] = o
                    l_scr[pl.ds(row0, tq, stride=dil), :] = lse
            return carry

        lax.fori_loop(0, dil * nsub // A_GROUP, body, 0)

    l0, l1, l2 = scr[1][...], scr[3][...], scr[5][...]
    mx = jnp.maximum(jnp.maximum(l0, l1), l2)
    e0, e1, e2 = jnp.exp(l0 - mx), jnp.exp(l1 - mx), jnp.exp(l2 - mx)
    den = e0 + e1 + e2
    o = (e0 * scr[0][...] + e1 * scr[2][...] + e2 * scr[4][...]) / den
    o_ref[...] = o.astype(o_ref.dtype)


def _attn_a(qkvs, B, S, TB=2048):
    n_pairs = (16 * HEAD_DIM) // LANES
    in_specs, args = [], []
    for g, dil in enumerate(DILATIONS):
        L = S // dil
        in_specs += [
            pl.BlockSpec((None, dil, TB // dil, LANES), lambda b, hp, t: (b, 0, t, hp)),
            pl.BlockSpec((None, dil, L, LANES), lambda b, hp, t: (b, 0, 0, n_pairs + hp)),
            pl.BlockSpec((None, dil, L, LANES), lambda b, hp, t: (b, 0, 0, 2 * n_pairs + hp)),
        ]
        args += [qkvs[g]] * 3
    return pl.pallas_call(
        functools.partial(_attn_a_kernel, TB=TB, S=S),
        grid=(B, n_pairs, S // TB),
        in_specs=in_specs,
        out_specs=pl.BlockSpec((None, TB, LANES), lambda b, hp, t: (b, t, hp)),
        out_shape=jax.ShapeDtypeStruct((B, S, n_pairs * LANES), BF16),
        scratch_shapes=[pltpu.VMEM((TB, LANES), F32)] * 6 + [pltpu.VMEM((3, 4 * A_HALF, 256), F32)],
        compiler_params=_cparams(("parallel", "parallel", "arbitrary")),
        name="attn_dilated",
    )(*args)


def _attn_c_kernel(q_ref, k_ref, v_ref, sink_ref, o_ref, bias_scr, *, TB, S):
    t = pl.program_id(1)
    tq = 128
    W = tq + 2 * C_HALF
    n_pairs = q_ref.shape[1] // LANES

    @pl.when(t == 0)
    def _():
        _fill_band_bias(bias_scr, tq, W, C_HALF)

    def body(jj, carry):
        qs = t * TB + jj * tq
        ws = pl.multiple_of(jnp.clip(qs - C_HALF, 0, S - W), C_HALF)
        kw = k_ref[pl.ds(ws, W), :]
        vw = v_ref[pl.ds(ws, W), :]
        bias = bias_scr[(qs - ws) // C_HALF]
        rows = pl.ds(pl.multiple_of(jj * tq, tq), tq)
        for p0 in range(0, n_pairs, A_GROUP):
            sts = [_band_scores(q_ref[rows, p * LANES:(p + 1) * LANES], kw, bias)
                   for p in range(p0, p0 + A_GROUP)]
            for p, st in zip(range(p0, p0 + A_GROUP), sts):
                sink_row = sink_ref[p:p + 1, :]
                o, _ = _band_finish(st, vw, False, sink_row=sink_row)
                o_ref[rows, p * LANES:(p + 1) * LANES] = o.astype(o_ref.dtype)
        return carry

    lax.fori_loop(0, TB // tq, body, 0)


def _attn_c(qkv, sink_tab, B, S, TB=1024):
    nq = 16 * HEAD_DIM
    return pl.pallas_call(
        functools.partial(_attn_c_kernel, TB=TB, S=S),
        grid=(B, S // TB),
        in_specs=[
            pl.BlockSpec((None, TB, nq), lambda b, t: (b, t, 0)),
            pl.BlockSpec((None, S, LANES), lambda b, t: (b, 0, nq // LANES)),
            pl.BlockSpec((None, S, LANES), lambda b, t: (b, 0, nq // LANES + 1)),
            pl.BlockSpec((8, 2 * LANES), lambda b, t: (0, 0)),
        ],
        out_specs=pl.BlockSpec((None, TB, nq), lambda b, t: (b, t, 0)),
        out_shape=jax.ShapeDtypeStruct((B, S, nq), BF16),
        scratch_shapes=[pltpu.VMEM((3, LANES + 2 * C_HALF, 2 * LANES), F32)],
        compiler_params=_cparams(("parallel", "arbitrary")),
        name="attn_swa_sink",
    )(qkv, qkv, qkv, sink_tab)


def _attn_b_kernel(*refs, lambda_init, n_cast):
    lam_ref, q_ref, qn_ref, k_ref, kn_ref, v_ref, subln_ref = refs[:7]
    cast_in = refs[7:7 + n_cast]
    o_ref = refs[7 + n_cast]
    cast_out = refs[8 + n_cast:8 + 2 * n_cast]
    s0_scr, s1_scr, m_scr, l_scr, acc_scr = refs[8 + 2 * n_cast:]
    _attn_b_body(lam_ref, q_ref, qn_ref, k_ref, kn_ref, v_ref, subln_ref, o_ref,
                 s0_scr, s1_scr, m_scr, l_scr, acc_scr, lambda_init=lambda_init)
    for src_ref, dst_ref in zip(cast_in, cast_out):
        dst_ref[...] = src_ref[...].astype(dst_ref.dtype)


def _attn_b_body(lam_ref, q_ref, qn_ref, k_ref, kn_ref, v_ref, subln_ref, o_ref,
                 s0_scr, s1_scr, m_scr, l_scr, acc_scr, *, lambda_init):
    i = pl.program_id(2)
    kv = pl.program_id(3)
    nkv = pl.num_programs(3)
    tk = k_ref.shape[0]
    lane = lax.broadcasted_iota(jnp.int32, (1, LANES), 1)
    lane_lo = lane < HEAD_DIM

    def scores(kref, qref, c):
        k = kref[...] * jnp.asarray(HEAD_DIM ** -0.5, kref.dtype)
        sel = lane_lo if c == 0 else jnp.logical_not(lane_lo)
        kc = jnp.where(sel, k, jnp.zeros_like(k))
        return lax.dot_general(kc, qref[...], (((1,), (1,)), ((), ())), preferred_element_type=F32)

    @pl.when(kv == 0)
    def _():
        m_scr[...] = jnp.full(m_scr.shape, -jnp.inf, F32)
        l_scr[...] = jnp.zeros(l_scr.shape, F32)
        acc_scr[...] = jnp.zeros(acc_scr.shape, F32)

    @pl.when(jnp.logical_and(i == 0, kv == 0))
    def _():
        for c in range(2):
            s0_scr[c * tk:(c + 1) * tk, :] = scores(k_ref, q_ref, c)

    def step(cur_scr, nxt_scr):
        vt = jnp.concatenate([v_ref[...].T, jnp.ones((16, tk), BF16)], axis=0)
        for c in range(2):
            nxt_scr[c * tk:(c + 1) * tk, :] = scores(kn_ref, qn_ref, c)
            s = cur_scr[c * tk:(c + 1) * tk, :]
            m_old = m_scr[c]
            m_new = jnp.maximum(m_old, jnp.max(s, axis=0, keepdims=True))
            alpha = jnp.exp(m_old - m_new)
            p = jnp.exp(s - m_new).astype(BF16)
            pv = jnp.dot(vt, p, preferred_element_type=F32)
            l_scr[c] = alpha * l_scr[c] + pv[LANES:LANES + 1]
            acc_scr[c] = alpha * acc_scr[c] + pv[:LANES]
            m_scr[c] = m_new

    @pl.when(kv % 2 == 0)
    def _():
        step(s0_scr, s1_scr)

    @pl.when(kv % 2 == 1)
    def _():
        step(s1_scr, s0_scr)

    @pl.when(kv == nkv - 1)
    def _():
        lp = lam_ref[...]
        lam = (jnp.exp(jnp.sum(lp[0:1] * lp[1:2], axis=-1, keepdims=True))
               - jnp.exp(jnp.sum(lp[2:3] * lp[3:4], axis=-1, keepdims=True)) + lambda_init)
        o = acc_scr[0] / l_scr[0] - lam * (acc_scr[1] / l_scr[1])
        ms = jnp.mean(o * o, axis=0, keepdims=True)
        o = o * lax.rsqrt(ms + SUBLN_EPS) * subln_ref[...] * (1.0 - lambda_init)
        o_ref[...] = o.T.astype(o_ref.dtype)


def _attn_b(qkv, lam_params, subln, lambda_init, B, S, side_casts=(), tq=1024, tk=2048):
    H = 8
    nq, nkv = S // tq, S // tk
    assert nkv % 2 == 0
    nsteps = B * H * nq * nkv
    cast_specs, cast_shapes = [], []
    for w in side_casts:
        rows, cols = w.shape
        nblk = nsteps
        while rows % nblk or (rows // nblk) % 16:
            nblk //= 2
        per = nsteps // nblk

        def w_map(b, h, i, kv, per=per):
            return ((((b * H + h) * nq + i) * nkv + kv) // per, 0)

        cast_specs.append(pl.BlockSpec((rows // nblk, cols), w_map))
        cast_shapes.append(jax.ShapeDtypeStruct((rows, cols), BF16))

    def q_next(b, h, i, kv):
        return (b, jnp.minimum(i + (kv + 1) // nkv, nq - 1), h)

    def k_next(b, h, i, kv):
        return (b, (kv + 1) % nkv, H + h)

    outs = pl.pallas_call(
        functools.partial(_attn_b_kernel, lambda_init=lambda_init, n_cast=len(side_casts)),
        grid=(B, H, nq, nkv),
        in_specs=[
            pl.BlockSpec((4, HEAD_DIM), lambda b, h, i, kv: (0, 0)),
            pl.BlockSpec((None, tq, LANES), lambda b, h, i, kv: (b, i, h)),
            pl.BlockSpec((None, tq, LANES), q_next),
            pl.BlockSpec((None, tk, LANES), lambda b, h, i, kv: (b, kv, H + h)),
            pl.BlockSpec((None, tk, LANES), k_next),
            pl.BlockSpec((None, tk, LANES), lambda b, h, i, kv: (b, kv, 2 * H + h)),
            pl.BlockSpec((LANES, 1), lambda b, h, i, kv: (0, 0)),
        ] + cast_specs,
        out_specs=[pl.BlockSpec((None, tq, LANES), lambda b, h, i, kv: (b, i, h))] + cast_specs,
        out_shape=[jax.ShapeDtypeStruct((B, S, H * LANES), BF16)] + cast_shapes,
        scratch_shapes=[pltpu.VMEM((2 * tk, tq), F32), pltpu.VMEM((2 * tk, tq), F32),
                        pltpu.VMEM((2, 1, tq), F32), pltpu.VMEM((2, 1, tq), F32),
                        pltpu.VMEM((2, LANES, tq), F32)],
        compiler_params=_cparams(("parallel", "parallel", "arbitrary", "arbitrary")),
        name="attn_diff",
    )(lam_params, qkv, qkv, qkv, qkv, qkv, subln.reshape(LANES, 1), *side_casts)
    return outs[0], outs[1:]


def _out_proj_kernel(x_ref, o_ref, w_ref, y_ref):
    y_ref[...] = x_ref[...] + jnp.dot(o_ref[...], w_ref[...], preferred_element_type=F32)


def _out_proj(x2, o2, w, tm=1024):
    T, D = x2.shape
    K = o2.shape[1]
    return pl.pallas_call(
        _out_proj_kernel,
        grid=(T // tm,),
        in_specs=[pl.BlockSpec((tm, D), lambda i: (i, 0)),
                  pl.BlockSpec((tm, K), lambda i: (i, 0)),
                  pl.BlockSpec((K, D), lambda i: (0, 0))],
        out_specs=pl.BlockSpec((tm, D), lambda i: (i, 0)),
        out_shape=jax.ShapeDtypeStruct((T, D), F32),
        compiler_params=_cparams(("parallel",)),
        name="out_proj",
    )(x2, o2, w)


def _ffn_kernel(x_ref, g_ref, w1_ref, w3_ref, w2_ref, y_ref, h_scr, acc_scr):
    f = pl.program_id(1)

    @pl.when(f == 0)
    def _():
        x = x_ref[...]
        ms = jnp.mean(x * x, axis=-1, keepdims=True)
        h_scr[...] = (x * lax.rsqrt(ms + NORM_EPS) * g_ref[...]).astype(BF16)
        acc_scr[...] = x

    h = h_scr[...]
    u = jnp.dot(h, w1_ref[...], preferred_element_type=F32)
    v = jnp.dot(h, w3_ref[...], preferred_element_type=F32)
    a = (u * jax.nn.sigmoid(u) * v).astype(BF16)
    acc_scr[...] += jnp.dot(a, w2_ref[...], preferred_element_type=F32)

    @pl.when(f == pl.num_programs(1) - 1)
    def _():
        y_ref[...] = acc_scr[...]


def _ffn_dense(x2, g, w1, w3, w2, tm=512):
    T, D = x2.shape
    F = w1.shape[1]
    tf = F
    return pl.pallas_call(
        _ffn_kernel,
        grid=(T // tm, F // tf),
        in_specs=[pl.BlockSpec((tm, D), lambda i, f: (i, 0)),
                  pl.BlockSpec((1, D), lambda i, f: (0, 0)),
                  pl.BlockSpec((D, tf), lambda i, f: (0, f)),
                  pl.BlockSpec((D, tf), lambda i, f: (0, f)),
                  pl.BlockSpec((tf, D), lambda i, f: (f, 0))],
        out_specs=pl.BlockSpec((tm, D), lambda i, f: (i, 0)),
        out_shape=jax.ShapeDtypeStruct((T, D), F32),
        scratch_shapes=[pltpu.VMEM((tm, D), BF16), pltpu.VMEM((tm, D), F32)],
        compiler_params=_cparams(("parallel", "arbitrary")),
        name="ffn_dense",
    )(x2, g.reshape(1, D), w1, w3, w2)


ROW_TILE = 8


def _rows_to_tiles(ref, val):
    n = val.shape[0]
    for a in range(ROW_TILE):
        ref[pl.ds(a, n, stride=ROW_TILE), :] = val[:, a * LANES:(a + 1) * LANES]


def _tiles_to_rows(ref, n):
    return jnp.concatenate([ref[pl.ds(a, n, stride=ROW_TILE), :] for a in range(ROW_TILE)], axis=1)


def _router_kernel(x_ref, g_ref, r_ref, h_ref, mi_ref, mf_ref, cnt_ref, carry_scr):
    i = pl.program_id(0)
    tm = x_ref.shape[0]

    @pl.when(i == 0)
    def _():
        carry_scr[...] = jnp.zeros(carry_scr.shape, F32)

    x = x_ref[...]
    ms = jnp.mean(x * x, axis=-1, keepdims=True)
    h = x * lax.rsqrt(ms + NORM_EPS) * g_ref[...]
    _rows_to_tiles(h_ref, h)
    lane = lax.broadcasted_iota(jnp.int32, (tm, LANES), 1)
    logits = jnp.dot(h, r_ref[...], preferred_element_type=F32, precision=lax.Precision.HIGHEST)
    logits = jnp.where(lane < N_EXPERTS, logits, -jnp.inf)
    v0 = jnp.max(logits, axis=-1, keepdims=True)
    i0 = jnp.min(jnp.where(logits == v0, lane, LANES), axis=-1, keepdims=True)
    rest = jnp.where(lane == i0, -jnp.inf, logits)
    v1 = jnp.max(rest, axis=-1, keepdims=True)
    i1 = jnp.min(jnp.where(rest == v1, lane, LANES), axis=-1, keepdims=True)
    tt = jnp.exp(v1 - v0)
    g0 = 1.0 / (1.0 + tt)
    g1 = tt / (1.0 + tt)
    sel0 = lane == i0
    sel1 = lane == i1
    onehot = jnp.where(jnp.logical_or(sel0, sel1), 1.0, 0.0)
    row = lax.broadcasted_iota(jnp.int32, (tm, tm), 0)
    col = lax.broadcasted_iota(jnp.int32, (tm, tm), 1)
    tri = jnp.where(row > col, 1.0, 0.0).astype(BF16)
    before = carry_scr[...] + jnp.dot(tri, onehot.astype(BF16), preferred_element_type=F32)
    rank0 = jnp.sum(jnp.where(sel0, before, 0.0), axis=-1, keepdims=True)
    rank1 = jnp.sum(jnp.where(sel1, before, 0.0), axis=-1, keepdims=True)
    carry_scr[...] = carry_scr[...] + jnp.sum(onehot, axis=0, keepdims=True)
    mi = jnp.where(lane == 0, i0, jnp.where(lane == 1, i1, 0))
    mi = jnp.where(lane == 2, rank0.astype(jnp.int32), jnp.where(lane == 3, rank1.astype(jnp.int32), mi))
    mi_ref[...] = mi
    mf_ref[...] = jnp.where(lane == 0, g0, jnp.where(lane == 1, g1, 0.0))
    cnt_ref[...] = carry_scr[...]


def _router(x2, g, router, tm=512):
    T, D = x2.shape
    rpad = jnp.zeros((D, LANES), F32).at[:, :N_EXPERTS].set(router)
    return pl.pallas_call(
        _router_kernel,
        grid=(T // tm,),
        in_specs=[pl.BlockSpec((tm, D), lambda i: (i, 0)),
                  pl.BlockSpec((1, D), lambda i: (0, 0)),
                  pl.BlockSpec((D, LANES), lambda i: (0, 0))],
        out_specs=[pl.BlockSpec((tm * ROW_TILE, LANES), lambda i: (i, 0)),
                   pl.BlockSpec((tm, LANES), lambda i: (i, 0)),
                   pl.BlockSpec((tm, LANES), lambda i: (i, 0)),
                   pl.BlockSpec((1, LANES), lambda i: (0, 0))],
        out_shape=[jax.ShapeDtypeStruct((T * ROW_TILE, LANES), F32),
                   jax.ShapeDtypeStruct((T, LANES), jnp.int32),
                   jax.ShapeDtypeStruct((T, LANES), F32),
                   jax.ShapeDtypeStruct((1, LANES), F32)],
        scratch_shapes=[pltpu.VMEM((1, LANES), F32)],
        compiler_params=_cparams(("arbitrary",)),
        name="moe_router",
    )(x2, g.reshape(1, D), rpad)


def _dispatch_kernel(dest_ref, h_ref, buf_in, buf_hbm, sem, *, tm):
    del buf_in

    def copy(n, slot):
        return pltpu.make_async_copy(h_ref.at[pl.ds(pl.multiple_of(n * ROW_TILE, ROW_TILE), ROW_TILE)],
                                     buf_hbm.at[pl.ds(pl.multiple_of(slot * ROW_TILE, ROW_TILE), ROW_TILE)], sem)

    def issue(n, carry):
        copy(n, dest_ref[2 * n]).start(priority=0)
        copy(n, dest_ref[2 * n + 1]).start(priority=1)
        return carry

    lax.fori_loop(0, tm, issue, 0)

    def drain(n, carry):
        copy(0, 0).wait()
        copy(0, 0).wait()
        return carry

    lax.fori_loop(0, tm, drain, 0)


def _dispatch(h2, dest_flat, n_rows, tm=512):
    T = h2.shape[0] // ROW_TILE
    buf0 = jnp.zeros((n_rows * ROW_TILE, LANES), F32)
    return pl.pallas_call(
        functools.partial(_dispatch_kernel, tm=tm),
        grid=(T // tm,),
        in_specs=[pl.BlockSpec((2 * tm,), lambda i: (i,), memory_space=pltpu.SMEM),
                  pl.BlockSpec((tm * ROW_TILE, LANES), lambda i: (i, 0)),
                  pl.BlockSpec(memory_space=pl.ANY)],
        out_specs=pl.BlockSpec(memory_space=pl.ANY),
        out_shape=jax.ShapeDtypeStruct((n_rows * ROW_TILE, LANES), F32),
        scratch_shapes=[pltpu.SemaphoreType.DMA(())],
        input_output_aliases={2: 0},
        compiler_params=_cparams(("arbitrary",)),
        name="moe_dispatch",
    )(dest_flat, h2, buf0)


def _moe_ffn_kernel(be_ref, nu_ref, x_ref, w1_ref, w3_ref, w2_ref, y_ref, acc_scr):
    b = pl.program_id(0)
    f = pl.program_id(1)

    @pl.when(b < nu_ref[0])
    def _():
        @pl.when(f == 0)
        def _():
            acc_scr[...] = jnp.zeros(acc_scr.shape, F32)

        h = _tiles_to_rows(x_ref, MOE_BLOCK).astype(BF16)
        u = jnp.dot(h, w1_ref[...], preferred_element_type=F32)
        v = jnp.dot(h, w3_ref[...], preferred_element_type=F32)
        a = (u * jax.nn.sigmoid(u) * v).astype(BF16)
        acc_scr[...] += jnp.dot(a, w2_ref[...], preferred_element_type=F32)

        @pl.when(f == pl.num_programs(1) - 1)
        def _():
            _rows_to_tiles(y_ref, acc_scr[...])

    @pl.when(jnp.logical_and(b >= nu_ref[0], f == 0))
    def _():
        y_ref[...] = jnp.zeros(y_ref.shape, F32)


def _moe_ffn(buf, block_expert, n_used, w1, w3, w2, tf=1792):
    R = buf.shape[0] // ROW_TILE
    E, D, F = w1.shape
    nb = R // MOE_BLOCK

    def row_map(b, f, be, nu):
        return (jnp.minimum(b, nu[0] - 1), 0)

    def w_in_map(b, f, be, nu):
        live = b < nu[0]
        return (be[jnp.minimum(b, nu[0] - 1)], 0, jnp.where(live, f, F // tf - 1))

    def w_out_map(b, f, be, nu):
        live = b < nu[0]
        return (be[jnp.minimum(b, nu[0] - 1)], jnp.where(live, f, F // tf - 1), 0)

    return pl.pallas_call(
        _moe_ffn_kernel,
        grid_spec=pltpu.PrefetchScalarGridSpec(
            num_scalar_prefetch=2,
            grid=(nb, F // tf),
            in_specs=[pl.BlockSpec((MOE_BLOCK * ROW_TILE, LANES), row_map),
                      pl.BlockSpec((None, D, tf), w_in_map),
                      pl.BlockSpec((None, D, tf), w_in_map),
                      pl.BlockSpec((None, tf, D), w_out_map)],
            out_specs=pl.BlockSpec((MOE_BLOCK * ROW_TILE, LANES), lambda b, f, be, nu: (b, 0)),
            scratch_shapes=[pltpu.VMEM((MOE_BLOCK, D), F32)],
        ),
        out_shape=jax.ShapeDtypeStruct((R * ROW_TILE, LANES), F32),
        compiler_params=_cparams(("arbitrary", "arbitrary")),
        name="moe_ffn",
    )(block_expert, n_used, buf, w1, w3, w2)


def _combine_kernel(dest_ref, x_ref, mf_ref, g_ref, ybuf_hbm, y_ref, rows0_scr, rows1_scr, sem, *, tm, final_norm):
    def copy(slot, k, n):
        dst = rows0_scr if k == 0 else rows1_scr
        return pltpu.make_async_copy(ybuf_hbm.at[pl.ds(pl.multiple_of(slot * ROW_TILE, ROW_TILE), ROW_TILE)],
                                     dst.at[pl.ds(pl.multiple_of(n * ROW_TILE, ROW_TILE), ROW_TILE)], sem)

    def issue(n, carry):
        copy(dest_ref[2 * n], 0, n).start(priority=0)
        copy(dest_ref[2 * n + 1], 1, n).start(priority=1)
        return carry

    lax.fori_loop(0, tm, issue, 0)

    def drain(n, carry):
        copy(0, 0, 0).wait()
        copy(0, 1, 0).wait()
        return carry

    lax.fori_loop(0, tm, drain, 0)

    mf = mf_ref[...]
    y = x_ref[...] + mf[:, 0:1] * _tiles_to_rows(rows0_scr, tm) + mf[:, 1:2] * _tiles_to_rows(rows1_scr, tm)
    if final_norm:
        ms = jnp.mean(y * y, axis=-1, keepdims=True)
        y = y * lax.rsqrt(ms + NORM_EPS) * g_ref[...]
    y_ref[...] = y


def _combine(x2, mf, dest_flat, ybuf, g_final, final_norm, tm=512):
    T, D = x2.shape
    return pl.pallas_call(
        functools.partial(_combine_kernel, tm=tm, final_norm=final_norm),
        grid=(T // tm,),
        in_specs=[pl.BlockSpec((2 * tm,), lambda i: (i,), memory_space=pltpu.SMEM),
                  pl.BlockSpec((tm, D), lambda i: (i, 0)),
                  pl.BlockSpec((tm, LANES), lambda i: (i, 0)),
                  pl.BlockSpec((1, D), lambda i: (0, 0)),
                  pl.BlockSpec(memory_space=pl.ANY)],
        out_specs=pl.BlockSpec((tm, D), lambda i: (i, 0)),
        out_shape=jax.ShapeDtypeStruct((T, D), F32),
        scratch_shapes=[pltpu.VMEM((tm * ROW_TILE, LANES), F32), pltpu.VMEM((tm * ROW_TILE, LANES), F32),
                        pltpu.SemaphoreType.DMA(())],
        compiler_params=_cparams(("arbitrary",)),
        name="moe_combine",
    )(dest_flat, x2, mf, g_final.reshape(1, D), ybuf)


def _moe_layer(x2, g, router, w1, w3, w2, g_final, final_norm):
    T, D = x2.shape
    h2, mi, mf, cnt = _router(x2, g, router)
    counts = cnt[0, :N_EXPERTS].astype(jnp.int32)
    padded = ((counts + MOE_BLOCK - 1) // MOE_BLOCK) * MOE_BLOCK
    pend = jnp.cumsum(padded)
    pstart = pend - padded
    dest = pstart[mi[:, 0:2]] + mi[:, 2:4]
    dest_flat = dest.reshape(-1).astype(jnp.int32)
    nb = -(-(2 * T + N_EXPERTS * (MOE_BLOCK - 1)) // MOE_BLOCK)
    block_start = jnp.arange(nb, dtype=jnp.int32) * MOE_BLOCK
    block_expert = jnp.minimum(jnp.sum(pend[None, :] <= block_start[:, None], axis=1),
                               N_EXPERTS - 1).astype(jnp.int32)
    n_used = (pend[-1:] // MOE_BLOCK).astype(jnp.int32)
    buf = _dispatch(h2, dest_flat, nb * MOE_BLOCK)
    ybuf = _moe_ffn(buf, block_expert, n_used, w1, w3, w2)
    return _combine(x2, mf, dest_flat, ybuf, g_final, final_norm)


def _mixer_a(x, g, w_in, w_out, tabs):
    B, S, D = x.shape
    w = w_in.astype(BF16)
    qkvs = []
    for gi, dil in enumerate(DILATIONS):
        cos, sin = tabs[dil]
        qkvs.append(_norm_proj(x, g, w, cos, sin, dil=dil, col_block=gi, ncols=3 * D, tc=D,
                               rope_chunks=_QKV_ROPE_CHUNKS))
    o = _attn_a(qkvs, B, S)
    return _out_proj(x.reshape(B * S, D), o.reshape(B * S, D), w_out.astype(BF16)).reshape(B, S, D)


def _mixer_b(x, g, w_in, lq1, lk1, lq2, lk2, subln, w_out, lambda_init, tabs, side_casts=()):
    B, S, D = x.shape
    cos, sin = tabs[1]
    qkv = _norm_proj(x, g, w_in.astype(BF16), cos, sin, dil=1, col_block=0, ncols=3 * D, tc=D,
                     rope_chunks=_QKV_ROPE_CHUNKS).reshape(B, S, 3 * D)
    lam_params = jnp.stack([lq1, lk1, lq2, lk2]).astype(F32)
    o, casted = _attn_b(qkv, lam_params, subln.astype(F32), lambda_init, B, S, side_casts)
    return _out_proj(x.reshape(B * S, D), o.reshape(B * S, D), w_out.astype(BF16)).reshape(B, S, D), casted


def _mixer_c(x, g, w_in, sink, w_out, tabs):
    B, S, D = x.shape
    cos, sin = tabs[1]
    head_order = jnp.arange(16).reshape(2, 8).T.reshape(-1)
    col_order = (head_order[:, None] * HEAD_DIM + jnp.arange(HEAD_DIM)[None, :]).reshape(-1)
    w_q = w_in[:, :D][:, col_order]
    w = jnp.concatenate([w_q, w_in[:, D:]], axis=1).astype(BF16)
    ncols = w.shape[1]
    chunks = tuple(c < (ncols // LANES - 1) for c in range(ncols // LANES))
    qkv = _norm_proj(x, g, w, cos, sin, dil=1, col_block=0, ncols=ncols, tc=ncols,
                     rope_chunks=chunks).reshape(B, S, ncols)
    sk = sink.astype(F32)
    sink_tab = jnp.concatenate([jnp.broadcast_to(sk[:8, None], (8, LANES)),
                                jnp.broadcast_to(sk[8:, None], (8, LANES))], axis=1)
    o = _attn_c(qkv, sink_tab, B, S)
    w_o = w_out[col_order, :].astype(BF16)
    return _out_proj(x.reshape(B * S, D), o.reshape(B * S, D), w_o).reshape(B, S, D)


def kernel(x, positions, l0_norm_mix, l0_a_w_in, l0_a_w_out, l0_norm_ffn, l0_ffn_w1, l0_ffn_w3, l0_ffn_w2, l1_norm_mix, l1_b_w_in, l1_b_lambda_q1, l1_b_lambda_k1, l1_b_lambda_q2, l1_b_lambda_k2, l1_b_subln, l1_b_w_out, l1_norm_ffn, l1_moe_router, l1_moe_w1, l1_moe_w3, l1_moe_w2, l2_norm_mix, l2_c_w_in, l2_c_sink, l2_c_w_out, l2_norm_ffn, l2_ffn_w1, l2_ffn_w3, l2_ffn_w2, l3_norm_mix, l3_a_w_in, l3_a_w_out, l3_norm_ffn, l3_moe_router, l3_moe_w1, l3_moe_w3, l3_moe_w2, final_norm):
    B, S, D = x.shape
    T = B * S
    cos, sin = lax.optimization_barrier(_rope_tables(positions))
    tabs = {d: (_to_strided(cos, d), _to_strided(sin, d)) for d in DILATIONS}

    x = _mixer_a(x, l0_norm_mix, l0_a_w_in, l0_a_w_out, tabs)
    x = _ffn_dense(x.reshape(T, D), l0_norm_ffn, l0_ffn_w1.astype(BF16), l0_ffn_w3.astype(BF16),
                   l0_ffn_w2.astype(BF16)).reshape(B, S, D)
    lambda_init = 0.8 - 0.6 * math.exp(-0.3 * 1)
    moe_w = (l1_moe_w1, l1_moe_w3, l1_moe_w2, l3_moe_w1, l3_moe_w3, l3_moe_w2)
    x, moe_bf = _mixer_b(x, l1_norm_mix, l1_b_w_in, l1_b_lambda_q1, l1_b_lambda_k1, l1_b_lambda_q2,
                         l1_b_lambda_k2, l1_b_subln, l1_b_w_out, lambda_init, tabs,
                         side_casts=tuple(w.reshape(-1, w.shape[-1]) for w in moe_w))
    moe_bf = [wb.reshape(w.shape) for wb, w in zip(moe_bf, moe_w)]
    x = _moe_layer(x.reshape(T, D), l1_norm_ffn, l1_moe_router, moe_bf[0], moe_bf[1], moe_bf[2],
                   final_norm, False).reshape(B, S, D)
    x = _mixer_c(x, l2_norm_mix, l2_c_w_in, l2_c_sink, l2_c_w_out, tabs)
    x = _ffn_dense(x.reshape(T, D), l2_norm_ffn, l2_ffn_w1.astype(BF16), l2_ffn_w3.astype(BF16),
                   l2_ffn_w2.astype(BF16)).reshape(B, S, D)
    x = _mixer_a(x, l3_norm_mix, l3_a_w_in, l3_a_w_out, tabs)
    x = _moe_layer(x.reshape(T, D), l3_norm_ffn, l3_moe_router, moe_bf[3], moe_bf[4], moe_bf[5],
                   final_norm, True).reshape(B, S, D)
    return x
```

```python
import functools
import math

import jax
import jax.numpy as jnp
from jax import lax
from jax.experimental import pallas as pl
from jax.experimental.pallas import tpu as pltpu

F32 = jnp.float32
BF16 = jnp.bfloat16

D_MODEL = 1024
HEAD_DIM = 64
ROT_DIM = HEAD_DIM // 4
ROPE_THETA = 500000.0
NORM_EPS = 1e-6
SUBLN_EPS = 1e-5
DILATIONS = (1, 4, 16)
A_HALF = 64
C_HALF = 128
N_EXPERTS = 8
LANES = 128
VMEM_LIMIT = 56 * 1024 * 1024

MOE_BLOCK = 512
_QKV_ROPE_CHUNKS = (True,) * 16 + (False,) * 8
CK = 512
A_GROUP = 3


def _cparams(sem):
    return pltpu.CompilerParams(dimension_semantics=sem, vmem_limit_bytes=VMEM_LIMIT)


def _rope_tables(positions):
    lane = jnp.arange(LANES) % HEAD_DIM
    half = ROT_DIM // 2
    inv_freq = ROPE_THETA ** (-((2 * (lane % half)).astype(F32) / ROT_DIM))
    freq = jnp.where(lane < ROT_DIM, inv_freq, 0.0)
    sign = jnp.where(lane < half, -1.0, jnp.where(lane < ROT_DIM, 1.0, 0.0))
    ang = positions.astype(F32)[..., None] * freq
    return jnp.cos(ang), jnp.sin(ang) * sign


def _to_strided(t, d):
    B, S, C = t.shape
    return jnp.swapaxes(t.reshape(B, S // d, d, C), 1, 2)


def _rope_chunk(x, cos, sin, first_half):
    partner = jnp.where(first_half, pltpu.roll(x, LANES - ROT_DIM // 2, 1),
                        pltpu.roll(x, ROT_DIM // 2, 1))
    return x * cos + partner * sin


def _proj_kernel(*refs, dil, rope_chunks, tc):
    if dil > 1:
        x_ref, g_ref, w_ref, cos_ref, sin_ref, perm_ref, o_ref = refs
    else:
        x_ref, g_ref, w_ref, cos_ref, sin_ref, o_ref = refs
    tm = x_ref.shape[0]
    ncols = w_ref.shape[1]
    n = tm // dil
    x = x_ref[...]
    ms = jnp.mean(x * x, axis=-1, keepdims=True)
    h = (x * lax.rsqrt(ms + NORM_EPS) * g_ref[...]).astype(BF16)
    if dil > 1:
        h = jnp.dot(perm_ref[...], h, preferred_element_type=F32).astype(BF16)
    lane = lax.broadcasted_iota(jnp.int32, (1, LANES), 1)
    first_half = (lane % HEAD_DIM) < (ROT_DIM // 2)
    cos = cos_ref[...].reshape(tm, LANES)
    sin = sin_ref[...].reshape(tm, LANES)
    for c0 in range(0, ncols, tc):
        res = jnp.dot(h, w_ref[:, c0:c0 + tc], preferred_element_type=F32)
        parts = []
        for cc in range(tc // LANES):
            xc = res[:, cc * LANES:(cc + 1) * LANES]
            if rope_chunks[c0 // LANES + cc]:
                xc = _rope_chunk(xc, cos, sin, first_half)
            parts.append(xc.astype(o_ref.dtype))
        val = jnp.concatenate(parts, axis=1)
        for r in range(dil):
            o_ref[r, :, c0:c0 + tc] = val[r * n:(r + 1) * n]


def _norm_proj(x, g, w, cos, sin, *, dil, col_block, ncols, tc, rope_chunks, tm=512):
    B, S, D = x.shape
    n = tm // dil
    L = S // dil
    in_specs = [
        pl.BlockSpec((None, tm, D), lambda b, i: (b, i, 0)),
        pl.BlockSpec((1, D), lambda b, i: (0, 0)),
        pl.BlockSpec((D, ncols), lambda b, i: (0, col_block)),
        pl.BlockSpec((None, dil, n, LANES), lambda b, i: (b, 0, i, 0)),
        pl.BlockSpec((None, dil, n, LANES), lambda b, i: (b, 0, i, 0)),
    ]
    args = [x, g.reshape(1, D), w, cos, sin]
    if dil > 1:
        p = jnp.arange(tm)
        src = (p % n) * dil + p // n
        perm = (src[:, None] == jnp.arange(tm)[None, :]).astype(BF16)
        in_specs.append(pl.BlockSpec((tm, tm), lambda b, i: (0, 0)))
        args.append(perm)
    return pl.pallas_call(
        functools.partial(_proj_kernel, dil=dil, rope_chunks=rope_chunks, tc=tc),
        grid=(B, S // tm),
        in_specs=in_specs,
        out_specs=pl.BlockSpec((None, dil, n, ncols), lambda b, i: (b, 0, i, 0)),
        out_shape=jax.ShapeDtypeStruct((B, dil, L, ncols), BF16),
        compiler_params=_cparams(("parallel", "parallel")),
        name=f"norm_proj_d{dil}",
    )(*args)


def _fill_band_bias(bias_scr, tq, W, half):
    kk = lax.broadcasted_iota(jnp.int32, (W, 2 * tq), 0)
    qq = lax.broadcasted_iota(jnp.int32, (W, 2 * tq), 1) % tq
    for i in range(3):
        ok = jnp.abs(qq + i * half - kk) <= half
        bias_scr[i] = jnp.where(ok, 0.0, -jnp.inf).astype(F32)


def _band_scores(q2, kw, bias):
    lane_lo = lax.broadcasted_iota(jnp.int32, (1, LANES), 1) < HEAD_DIM
    zero = jnp.zeros_like(q2)
    qq = jnp.concatenate([jnp.where(lane_lo, q2, zero), jnp.where(lane_lo, zero, q2)], axis=0)
    qq = qq * jnp.asarray(HEAD_DIM ** -0.5, q2.dtype)
    return lax.dot_general(kw, qq, (((1,), (1,)), ((), ())), preferred_element_type=F32) + bias


def _band_finish(st, vw, want_lse, sink_row=None):
    tq = st.shape[1] // 2
    row_lo = lax.broadcasted_iota(jnp.int32, (LANES, 1), 0) < HEAD_DIM
    m = jnp.max(st, axis=0, keepdims=True)
    if sink_row is not None:
        m = jnp.maximum(m, sink_row)
    p = jnp.exp(st - m)
    l = jnp.sum(p, axis=0, keepdims=True)
    if sink_row is not None:
        l = l + jnp.exp(sink_row - m)
    ot = lax.dot_general(vw, p.astype(BF16), (((0,), (0,)), ((), ())), preferred_element_type=F32) / l
    o = jnp.where(row_lo, ot[:, :tq], ot[:, tq:]).T
    if not want_lse:
        return o, None
    lse_row = m + jnp.log(l)
    lse = jnp.where(row_lo, jnp.broadcast_to(lse_row[:, :tq], (LANES, tq)),
                    jnp.broadcast_to(lse_row[:, tq:], (LANES, tq))).T
    return o, lse


def _attn_a_kernel(*refs, TB, S):
    qkv = refs[:9]
    o_ref = refs[9]
    scr = refs[10:16]
    bias_scr = refs[16]
    t = pl.program_id(2)
    W = 4 * A_HALF

    @pl.when(t == 0)
    def _():
        _fill_band_bias(bias_scr, 128, W, A_HALF)

    subs = []
    for g, dil in enumerate(DILATIONS):
        nq = TB // dil
        tq = min(nq, 128)
        for r in range(dil):
            for jj in range(nq // tq):
                subs.append((g, dil, nq, tq, r, jj))

    def scores(sub):
        g, dil, nq, tq, r, jj = sub
        q_ref, k_ref = qkv[3 * g], qkv[3 * g + 1]
        L = S // dil
        qs = t * nq + jj * tq
        ws = pl.multiple_of(jnp.clip(qs - A_HALF, 0, L - W), A_HALF)
        q2 = q_ref[r, jj * tq:(jj + 1) * tq, :]
        return _band_scores(q2, k_ref[r, pl.ds(ws, W), :], bias_scr[(qs - ws) // A_HALF]), ws

    def finish(sub, st, ws):
        g, dil, nq, tq, r, jj = sub
        o_scr, l_scr = scr[2 * g], scr[2 * g + 1]
        o, lse = _band_finish(st, qkv[3 * g + 2][r, pl.ds(ws, W), :], True)
        row0 = jj * (tq * dil) + r
        if dil == 1:
            o_scr[row0:row0 + tq, :] = o
            l_scr[row0:row0 + tq, :] = lse
        else:
            o_scr[pl.ds(row0, tq, stride=dil), :] = o
            l_scr[pl.ds(row0, tq, stride=dil), :] = lse

    pend = [scores(sub) for sub in subs[:A_GROUP]]
    for n, sub in enumerate(subs):
        if n + A_GROUP < len(subs):
            pend.append(scores(subs[n + A_GROUP]))
        st, ws = pend[n]
        finish(sub, st, ws)

    l0, l1, l2 = scr[1][...], scr[3][...], scr[5][...]
    mx = jnp.maximum(jnp.maximum(l0, l1), l2)
    e0, e1, e2 = jnp.exp(l0 - mx), jnp.exp(l1 - mx), jnp.exp(l2 - mx)
    den = e0 + e1 + e2
    o = (e0 * scr[0][...] + e1 * scr[2][...] + e2 * scr[4][...]) / den
    o_ref[...] = o.astype(o_ref.dtype)


def _attn_a(qkvs, B, S, TB=2048):
    n_pairs = (16 * HEAD_DIM) // LANES
    in_specs, args = [], []
    for g, dil in enumerate(DILATIONS):
        L = S // dil
        in_specs += [
            pl.BlockSpec((None, dil, TB // dil, LANES), lambda b, hp, t: (b, 0, t, hp)),
            pl.BlockSpec((None, dil, L, LANES), lambda b, hp, t: (b, 0, 0, n_pairs + hp)),
            pl.BlockSpec((None, dil, L, LANES), lambda b, hp, t: (b, 0, 0, 2 * n_pairs + hp)),
        ]
        args += [qkvs[g]] * 3
    return pl.pallas_call(
        functools.partial(_attn_a_kernel, TB=TB, S=S),
        grid=(B, n_pairs, S // TB),
        in_specs=in_specs,
        out_specs=pl.BlockSpec((None, TB, LANES), lambda b, hp, t: (b, t, hp)),
        out_shape=jax.ShapeDtypeStruct((B, S, n_pairs * LANES), BF16),
        scratch_shapes=[pltpu.VMEM((TB, LANES), F32)] * 6 + [pltpu.VMEM((3, 4 * A_HALF, 256), F32)],
        compiler_params=_cparams(("parallel", "parallel", "arbitrary")),
        name="attn_dilated",
    )(*args)


def _attn_c_kernel(q_ref, k_ref, v_ref, sink_ref, o_ref, bias_scr, *, TB, S):
    t = pl.program_id(1)
    tq = 128
    W = tq + 2 * C_HALF
    n_pairs = q_ref.shape[1] // LANES

    @pl.when(t == 0)
    def _():
        _fill_band_bias(bias_scr, tq, W, C_HALF)

    def body(jj, carry):
        qs = t * TB + jj * tq
        ws = pl.multiple_of(jnp.clip(qs - C_HALF, 0, S - W), C_HALF)
        kw = k_ref[pl.ds(ws, W), :]
        vw = v_ref[pl.ds(ws, W), :]
        bias = bias_scr[(qs - ws) // C_HALF]
        rows = pl.ds(pl.multiple_of(jj * tq, tq), tq)
        def scores(p):
            return _band_scores(q_ref[rows, p * LANES:(p + 1) * LANES], kw, bias)

        pend = [scores(p) for p in range(A_GROUP)]
        for p in range(n_pairs):
            if p + A_GROUP < n_pairs:
                pend.append(scores(p + A_GROUP))
            sink_row = sink_ref[p:p + 1, :]
            o, _ = _band_finish(pend[p], vw, False, sink_row=sink_row)
            o_ref[rows, p * LANES:(p + 1) * LANES] = o.astype(o_ref.dtype)
        return carry

    lax.fori_loop(0, TB // tq, body, 0)


def _attn_c(qkv, sink_tab, B, S, TB=1024):
    nq = 16 * HEAD_DIM
    return pl.pallas_call(
        functools.partial(_attn_c_kernel, TB=TB, S=S),
        grid=(B, S // TB),
        in_specs=[
            pl.BlockSpec((None, TB, nq), lambda b, t: (b, t, 0)),
            pl.BlockSpec((None, S, LANES), lambda b, t: (b, 0, nq // LANES)),
            pl.BlockSpec((None, S, LANES), lambda b, t: (b, 0, nq // LANES + 1)),
            pl.BlockSpec((8, 2 * LANES), lambda b, t: (0, 0)),
        ],
        out_specs=pl.BlockSpec((None, TB, nq), lambda b, t: (b, t, 0)),
        out_shape=jax.ShapeDtypeStruct((B, S, nq), BF16),
        scratch_shapes=[pltpu.VMEM((3, LANES + 2 * C_HALF, 2 * LANES), F32)],
        compiler_params=_cparams(("parallel", "arbitrary")),
        name="attn_swa_sink",
    )(qkv, qkv, qkv, sink_tab)


def _attn_b_kernel(*refs, lambda_init, n_cast):
    lam_ref, q_ref, qn_ref, k_ref, kn_ref, v_ref, subln_ref = refs[:7]
    cast_in = refs[7:7 + n_cast]
    o_ref = refs[7 + n_cast]
    cast_out = refs[8 + n_cast:8 + 2 * n_cast]
    s0_scr, s1_scr, m_scr, l_scr, acc_scr = refs[8 + 2 * n_cast:]
    _attn_b_body(lam_ref, q_ref, qn_ref, k_ref, kn_ref, v_ref, subln_ref, o_ref,
                 s0_scr, s1_scr, m_scr, l_scr, acc_scr, lambda_init=lambda_init)
    for src_ref, dst_ref in zip(cast_in, cast_out):
        dst_ref[...] = src_ref[...].astype(dst_ref.dtype)


def _attn_b_body(lam_ref, q_ref, qn_ref, k_ref, kn_ref, v_ref, subln_ref, o_ref,
                 s0_scr, s1_scr, m_scr, l_scr, acc_scr, *, lambda_init):
    i = pl.program_id(2)
    kv = pl.program_id(3)
    nkv = pl.num_programs(3)
    tk = k_ref.shape[0]
    lane = lax.broadcasted_iota(jnp.int32, (1, LANES), 1)
    lane_lo = lane < HEAD_DIM

    def scores(kref, qref, c, j):
        k = kref[j * CK:(j + 1) * CK, :] * jnp.asarray(HEAD_DIM ** -0.5, kref.dtype)
        sel = lane_lo if c == 0 else jnp.logical_not(lane_lo)
        kc = jnp.where(sel, k, jnp.zeros_like(k))
        return lax.dot_general(kc, qref[...], (((1,), (1,)), ((), ())), preferred_element_type=F32)

    @pl.when(kv == 0)
    def _():
        m_scr[...] = jnp.full(m_scr.shape, -jnp.inf, F32)
        l_scr[...] = jnp.zeros(l_scr.shape, F32)
        acc_scr[...] = jnp.zeros(acc_scr.shape, F32)

    @pl.when(jnp.logical_and(i == 0, kv == 0))
    def _():
        for c in range(2):
            for j in range(tk // CK):
                s0_scr[c * tk + j * CK:c * tk + (j + 1) * CK, :] = scores(k_ref, q_ref, c, j)

    def step(cur_scr, nxt_scr):
        vt = jnp.concatenate([v_ref[...].T, jnp.ones((16, tk), BF16)], axis=0)
        for c in range(2):
            m, l, acc = m_scr[c], l_scr[c], acc_scr[c]
            for j in range(tk // CK):
                rows = slice(c * tk + j * CK, c * tk + (j + 1) * CK)
                nxt_scr[rows, :] = scores(kn_ref, qn_ref, c, j)
                s = cur_scr[rows, :]
                m_new = jnp.maximum(m, jnp.max(s, axis=0, keepdims=True))
                alpha = jnp.exp(m - m_new)
                p = jnp.exp(s - m_new).astype(BF16)
                pv = jnp.dot(vt[:, j * CK:(j + 1) * CK], p, preferred_element_type=F32)
                l = alpha * l + pv[LANES:LANES + 1]
                acc = alpha * acc + pv[:LANES]
                m = m_new
            m_scr[c], l_scr[c], acc_scr[c] = m, l, acc

    @pl.when(kv % 2 == 0)
    def _():
        step(s0_scr, s1_scr)

    @pl.when(kv % 2 == 1)
    def _():
        step(s1_scr, s0_scr)

    @pl.when(kv == nkv - 1)
    def _():
        lp = lam_ref[...]
        lam = (jnp.exp(jnp.sum(lp[0:1] * lp[1:2], axis=-1, keepdims=True))
               - jnp.exp(jnp.sum(lp[2:3] * lp[3:4], axis=-1, keepdims=True)) + lambda_init)
        o = acc_scr[0] / l_scr[0] - lam * (acc_scr[1] / l_scr[1])
        ms = jnp.mean(o * o, axis=0, keepdims=True)
        o = o * lax.rsqrt(ms + SUBLN_EPS) * subln_ref[...] * (1.0 - lambda_init)
        o_ref[...] = o.T.astype(o_ref.dtype)


def _attn_b(qkv, lam_params, subln, lambda_init, B, S, side_casts=(), tq=1024, tk=2048):
    H = 8
    nq, nkv = S // tq, S // tk
    assert nkv % 2 == 0
    nsteps = B * H * nq * nkv
    cast_specs, cast_shapes = [], []
    for w in side_casts:
        rows, cols = w.shape
        nblk = nsteps
        while rows % nblk or (rows // nblk) % 16:
            nblk //= 2
        per = nsteps // nblk

        def w_map(b, h, i, kv, per=per):
            return ((((b * H + h) * nq + i) * nkv + kv) // per, 0)

        cast_specs.append(pl.BlockSpec((rows // nblk, cols), w_map))
        cast_shapes.append(jax.ShapeDtypeStruct((rows, cols), BF16))

    def q_next(b, h, i, kv):
        return (b, jnp.minimum(i + (kv + 1) // nkv, nq - 1), h)

    def k_next(b, h, i, kv):
        return (b, (kv + 1) % nkv, H + h)

    outs = pl.pallas_call(
        functools.partial(_attn_b_kernel, lambda_init=lambda_init, n_cast=len(side_casts)),
        grid=(B, H, nq, nkv),
        in_specs=[
            pl.BlockSpec((4, HEAD_DIM), lambda b, h, i, kv: (0, 0)),
            pl.BlockSpec((None, tq, LANES), lambda b, h, i, kv: (b, i, h)),
            pl.BlockSpec((None, tq, LANES), q_next),
            pl.BlockSpec((None, tk, LANES), lambda b, h, i, kv: (b, kv, H + h)),
            pl.BlockSpec((None, tk, LANES), k_next),
            pl.BlockSpec((None, tk, LANES), lambda b, h, i, kv: (b, kv, 2 * H + h)),
            pl.BlockSpec((LANES, 1), lambda b, h, i, kv: (0, 0)),
        ] + cast_specs,
        out_specs=[pl.BlockSpec((None, tq, LANES), lambda b, h, i, kv: (b, i, h))] + cast_specs,
        out_shape=[jax.ShapeDtypeStruct((B, S, H * LANES), BF16)] + cast_shapes,
        scratch_shapes=[pltpu.VMEM((2 * tk, tq), F32), pltpu.VMEM((2 * tk, tq), F32),
                        pltpu.VMEM((2, 1, tq), F32), pltpu.VMEM((2, 1, tq), F32),
                        pltpu.VMEM((2, LANES, tq), F32)],
        compiler_params=_cparams(("parallel", "parallel", "arbitrary", "arbitrary")),
        name="attn_diff",
    )(lam_params, qkv, qkv, qkv, qkv, qkv, subln.reshape(LANES, 1), *side_casts)
    return outs[0], outs[1:]


def _out_proj_kernel(x_ref, o_ref, w_ref, y_ref):
    y_ref[...] = x_ref[...] + jnp.dot(o_ref[...], w_ref[...], preferred_element_type=F32)


def _out_proj(x2, o2, w, tm=1024):
    T, D = x2.shape
    K = o2.shape[1]
    return pl.pallas_call(
        _out_proj_kernel,
        grid=(T // tm,),
        in_specs=[pl.BlockSpec((tm, D), lambda i: (i, 0)),
                  pl.BlockSpec((tm, K), lambda i: (i, 0)),
                  pl.BlockSpec((K, D), lambda i: (0, 0))],
        out_specs=pl.BlockSpec((tm, D), lambda i: (i, 0)),
        out_shape=jax.ShapeDtypeStruct((T, D), F32),
        compiler_params=_cparams(("parallel",)),
        name="out_proj",
    )(x2, o2, w)


def _ffn_kernel(x_ref, g_ref, w1_ref, w3_ref, w2_ref, y_ref, h_scr, acc_scr):
    f = pl.program_id(1)

    @pl.when(f == 0)
    def _():
        x = x_ref[...]
        ms = jnp.mean(x * x, axis=-1, keepdims=True)
        h_scr[...] = (x * lax.rsqrt(ms + NORM_EPS) * g_ref[...]).astype(BF16)
        acc_scr[...] = x

    h = h_scr[...]
    u = jnp.dot(h, w1_ref[...], preferred_element_type=F32)
    v = jnp.dot(h, w3_ref[...], preferred_element_type=F32)
    a = (u * jax.nn.sigmoid(u) * v).astype(BF16)
    acc_scr[...] += jnp.dot(a, w2_ref[...], preferred_element_type=F32)

    @pl.when(f == pl.num_programs(1) - 1)
    def _():
        y_ref[...] = acc_scr[...]


def _ffn_dense(x2, g, w1, w3, w2, tm=512):
    T, D = x2.shape
    F = w1.shape[1]
    tf = F
    return pl.pallas_call(
        _ffn_kernel,
        grid=(T // tm, F // tf),
        in_specs=[pl.BlockSpec((tm, D), lambda i, f: (i, 0)),
                  pl.BlockSpec((1, D), lambda i, f: (0, 0)),
                  pl.BlockSpec((D, tf), lambda i, f: (0, f)),
                  pl.BlockSpec((D, tf), lambda i, f: (0, f)),
                  pl.BlockSpec((tf, D), lambda i, f: (f, 0))],
        out_specs=pl.BlockSpec((tm, D), lambda i, f: (i, 0)),
        out_shape=jax.ShapeDtypeStruct((T, D), F32),
        scratch_shapes=[pltpu.VMEM((tm, D), BF16), pltpu.VMEM((tm, D), F32)],
        compiler_params=_cparams(("parallel", "arbitrary")),
        name="ffn_dense",
    )(x2, g.reshape(1, D), w1, w3, w2)


ROW_TILE = 8


def _rows_to_tiles(ref, val):
    n = val.shape[0]
    for a in range(ROW_TILE):
        ref[pl.ds(a, n, stride=ROW_TILE), :] = val[:, a * LANES:(a + 1) * LANES]


def _tiles_to_rows(ref, n):
    return jnp.concatenate([ref[pl.ds(a, n, stride=ROW_TILE), :] for a in range(ROW_TILE)], axis=1)


def _router_kernel(x_ref, g_ref, r_ref, h_ref, mi_ref, mf_ref, cnt_ref, carry_scr):
    i = pl.program_id(0)
    tm = x_ref.shape[0]

    @pl.when(i == 0)
    def _():
        carry_scr[...] = jnp.zeros(carry_scr.shape, F32)

    x = x_ref[...]
    ms = jnp.mean(x * x, axis=-1, keepdims=True)
    h = x * lax.rsqrt(ms + NORM_EPS) * g_ref[...]
    _rows_to_tiles(h_ref, h)
    lane = lax.broadcasted_iota(jnp.int32, (tm, LANES), 1)
    logits = jnp.dot(h, r_ref[...], preferred_element_type=F32, precision=lax.Precision.HIGHEST)
    logits = jnp.where(lane < N_EXPERTS, logits, -jnp.inf)
    v0 = jnp.max(logits, axis=-1, keepdims=True)
    i0 = jnp.min(jnp.where(logits == v0, lane, LANES), axis=-1, keepdims=True)
    rest = jnp.where(lane == i0, -jnp.inf, logits)
    v1 = jnp.max(rest, axis=-1, keepdims=True)
    i1 = jnp.min(jnp.where(rest == v1, lane, LANES), axis=-1, keepdims=True)
    tt = jnp.exp(v1 - v0)
    g0 = 1.0 / (1.0 + tt)
    g1 = tt / (1.0 + tt)
    sel0 = lane == i0
    sel1 = lane == i1
    onehot = jnp.where(jnp.logical_or(sel0, sel1), 1.0, 0.0)
    row = lax.broadcasted_iota(jnp.int32, (tm, tm), 0)
    col = lax.broadcasted_iota(jnp.int32, (tm, tm), 1)
    tri = jnp.where(row > col, 1.0, 0.0).astype(BF16)
    before = carry_scr[...] + jnp.dot(tri, onehot.astype(BF16), preferred_element_type=F32)
    rank0 = jnp.sum(jnp.where(sel0, before, 0.0), axis=-1, keepdims=True)
    rank1 = jnp.sum(jnp.where(sel1, before, 0.0), axis=-1, keepdims=True)
    carry_scr[...] = carry_scr[...] + jnp.sum(onehot, axis=0, keepdims=True)
    mi = jnp.where(lane == 0, i0, jnp.where(lane == 1, i1, 0))
    mi = jnp.where(lane == 2, rank0.astype(jnp.int32), jnp.where(lane == 3, rank1.astype(jnp.int32), mi))
    mi_ref[...] = mi
    mf_ref[...] = jnp.where(lane == 0, g0, jnp.where(lane == 1, g1, 0.0))
    cnt_ref[...] = carry_scr[...]


def _router(x2, g, router, tm=512):
    T, D = x2.shape
    rpad = jnp.zeros((D, LANES), F32).at[:, :N_EXPERTS].set(router)
    return pl.pallas_call(
        _router_kernel,
        grid=(T // tm,),
        in_specs=[pl.BlockSpec((tm, D), lambda i: (i, 0)),
                  pl.BlockSpec((1, D), lambda i: (0, 0)),
                  pl.BlockSpec((D, LANES), lambda i: (0, 0))],
        out_specs=[pl.BlockSpec((tm * ROW_TILE, LANES), lambda i: (i, 0)),
                   pl.BlockSpec((tm, LANES), lambda i: (i, 0)),
                   pl.BlockSpec((tm, LANES), lambda i: (i, 0)),
                   pl.BlockSpec((1, LANES), lambda i: (0, 0))],
        out_shape=[jax.ShapeDtypeStruct((T * ROW_TILE, LANES), F32),
                   jax.ShapeDtypeStruct((T, LANES), jnp.int32),
                   jax.ShapeDtypeStruct((T, LANES), F32),
                   jax.ShapeDtypeStruct((1, LANES), F32)],
        scratch_shapes=[pltpu.VMEM((1, LANES), F32)],
        compiler_params=_cparams(("arbitrary",)),
        name="moe_router",
    )(x2, g.reshape(1, D), rpad)


def _dispatch_kernel(dest_ref, h_ref, buf_in, buf_hbm, sem, *, tm):
    del buf_in

    def copy(n, slot):
        return pltpu.make_async_copy(h_ref.at[pl.ds(pl.multiple_of(n * ROW_TILE, ROW_TILE), ROW_TILE)],
                                     buf_hbm.at[pl.ds(pl.multiple_of(slot * ROW_TILE, ROW_TILE), ROW_TILE)], sem)

    def issue(n, carry):
        copy(n, dest_ref[2 * n]).start(priority=0)
        copy(n, dest_ref[2 * n + 1]).start(priority=1)
        return carry

    lax.fori_loop(0, tm, issue, 0)

    def drain(n, carry):
        copy(0, 0).wait()
        copy(0, 0).wait()
        return carry

    lax.fori_loop(0, tm, drain, 0)


def _dispatch(h2, dest_flat, n_rows, tm=512):
    T = h2.shape[0] // ROW_TILE
    buf0 = jnp.zeros((n_rows * ROW_TILE, LANES), F32)
    return pl.pallas_call(
        functools.partial(_dispatch_kernel, tm=tm),
        grid=(T // tm,),
        in_specs=[pl.BlockSpec((2 * tm,), lambda i: (i,), memory_space=pltpu.SMEM),
                  pl.BlockSpec((tm * ROW_TILE, LANES), lambda i: (i, 0)),
                  pl.BlockSpec(memory_space=pl.ANY)],
        out_specs=pl.BlockSpec(memory_space=pl.ANY),
        out_shape=jax.ShapeDtypeStruct((n_rows * ROW_TILE, LANES), F32),
        scratch_shapes=[pltpu.SemaphoreType.DMA(())],
        input_output_aliases={2: 0},
        compiler_params=_cparams(("arbitrary",)),
        name="moe_dispatch",
    )(dest_flat, h2, buf0)


def _moe_ffn_kernel(be_ref, nu_ref, x_ref, w1_ref, w3_ref, w2_ref, y_ref, acc_scr):
    b = pl.program_id(0)
    f = pl.program_id(1)

    @pl.when(b < nu_ref[0])
    def _():
        @pl.when(f == 0)
        def _():
            acc_scr[...] = jnp.zeros(acc_scr.shape, F32)

        h = _tiles_to_rows(x_ref, MOE_BLOCK).astype(BF16)
        u = jnp.dot(h, w1_ref[...], preferred_element_type=F32)
        v = jnp.dot(h, w3_ref[...], preferred_element_type=F32)
        a = (u * jax.nn.sigmoid(u) * v).astype(BF16)
        acc_scr[...] += jnp.dot(a, w2_ref[...], preferred_element_type=F32)

        @pl.when(f == pl.num_programs(1) - 1)
        def _():
            _rows_to_tiles(y_ref, acc_scr[...])

    @pl.when(jnp.logical_and(b >= nu_ref[0], f == 0))
    def _():
        y_ref[...] = jnp.zeros(y_ref.shape, F32)


def _moe_ffn(buf, block_expert, n_used, w1, w3, w2, tf=1792):
    R = buf.shape[0] // ROW_TILE
    E, D, F = w1.shape
    nb = R // MOE_BLOCK

    def last_live(b, nu):
        return jnp.maximum(jnp.minimum(b, nu[0] - 1), 0)

    def row_map(b, f, be, nu):
        return (last_live(b, nu), 0)

    def w_in_map(b, f, be, nu):
        live = b < nu[0]
        return (be[last_live(b, nu)], 0, jnp.where(live, f, F // tf - 1))

    def w_out_map(b, f, be, nu):
        live = b < nu[0]
        return (be[last_live(b, nu)], jnp.where(live, f, F // tf - 1), 0)

    return pl.pallas_call(
        _moe_ffn_kernel,
        grid_spec=pltpu.PrefetchScalarGridSpec(
            num_scalar_prefetch=2,
            grid=(nb, F // tf),
            in_specs=[pl.BlockSpec((MOE_BLOCK * ROW_TILE, LANES), row_map),
                      pl.BlockSpec((None, D, tf), w_in_map),
                      pl.BlockSpec((None, D, tf), w_in_map),
                      pl.BlockSpec((None, tf, D), w_out_map)],
            out_specs=pl.BlockSpec((MOE_BLOCK * ROW_TILE, LANES), lambda b, f, be, nu: (b, 0)),
            scratch_shapes=[pltpu.VMEM((MOE_BLOCK, D), F32)],
        ),
        out_shape=jax.ShapeDtypeStruct((R * ROW_TILE, LANES), F32),
        compiler_params=_cparams(("arbitrary", "arbitrary")),
        name="moe_ffn",
    )(block_expert, n_used, buf, w1, w3, w2)


def _combine_kernel(dest_ref, x_ref, mf_ref, g_ref, ybuf_hbm, y_ref, rows0_scr, rows1_scr, sem, *, tm, final_norm):
    def copy(slot, k, n):
        dst = rows0_scr if k == 0 else rows1_scr
        return pltpu.make_async_copy(ybuf_hbm.at[pl.ds(pl.multiple_of(slot * ROW_TILE, ROW_TILE), ROW_TILE)],
                                     dst.at[pl.ds(pl.multiple_of(n * ROW_TILE, ROW_TILE), ROW_TILE)], sem)

    def issue(n, carry):
        copy(dest_ref[2 * n], 0, n).start(priority=0)
        copy(dest_ref[2 * n + 1], 1, n).start(priority=1)
        return carry

    lax.fori_loop(0, tm, issue, 0)

    def drain(n, carry):
        copy(0, 0, 0).wait()
        copy(0, 1, 0).wait()
        return carry

    lax.fori_loop(0, tm, drain, 0)

    mf = mf_ref[...]
    y = x_ref[...] + mf[:, 0:1] * _tiles_to_rows(rows0_scr, tm) + mf[:, 1:2] * _tiles_to_rows(rows1_scr, tm)
    if final_norm:
        ms = jnp.mean(y * y, axis=-1, keepdims=True)
        y = y * lax.rsqrt(ms + NORM_EPS) * g_ref[...]
    y_ref[...] = y


def _combine(x2, mf, dest_flat, ybuf, g_final, final_norm, tm=512):
    T, D = x2.shape
    return pl.pallas_call(
        functools.partial(_combine_kernel, tm=tm, final_norm=final_norm),
        grid=(T // tm,),
        in_specs=[pl.BlockSpec((2 * tm,), lambda i: (i,), memory_space=pltpu.SMEM),
                  pl.BlockSpec((tm, D), lambda i: (i, 0)),
                  pl.BlockSpec((tm, LANES), lambda i: (i, 0)),
                  pl.BlockSpec((1, D), lambda i: (0, 0)),
                  pl.BlockSpec(memory_space=pl.ANY)],
        out_specs=pl.BlockSpec((tm, D), lambda i: (i, 0)),
        out_shape=jax.ShapeDtypeStruct((T, D), F32),
        scratch_shapes=[pltpu.VMEM((tm * ROW_TILE, LANES), F32), pltpu.VMEM((tm * ROW_TILE, LANES), F32),
                        pltpu.SemaphoreType.DMA(())],
        compiler_params=_cparams(("arbitrary",)),
        name="moe_combine",
    )(dest_flat, x2, mf, g_final.reshape(1, D), ybuf)


def _moe_layer(x2, g, router, w1, w3, w2, g_final, final_norm):
    T, D = x2.shape
    h2, mi, mf, cnt = _router(x2, g, router)
    counts = cnt[0, :N_EXPERTS].astype(jnp.int32)
    padded = ((counts + MOE_BLOCK - 1) // MOE_BLOCK) * MOE_BLOCK
    pend = jnp.cumsum(padded)
    pstart = pend - padded
    dest = pstart[mi[:, 0:2]] + mi[:, 2:4]
    dest_flat = dest.reshape(-1).astype(jnp.int32)
    nb = -(-(2 * T + N_EXPERTS * (MOE_BLOCK - 1)) // MOE_BLOCK)
    block_start = jnp.arange(nb, dtype=jnp.int32) * MOE_BLOCK
    block_expert = jnp.minimum(jnp.sum(pend[None, :] <= block_start[:, None], axis=1),
                               N_EXPERTS - 1).astype(jnp.int32)
    n_used = (pend[-1:] // MOE_BLOCK).astype(jnp.int32)
    buf = _dispatch(h2, dest_flat, nb * MOE_BLOCK)
    ybuf = _moe_ffn(buf, block_expert, n_used, w1, w3, w2)
    return _combine(x2, mf, dest_flat, ybuf, g_final, final_norm)


def _mixer_a(x, g, w_in, w_out, tabs):
    B, S, D = x.shape
    w = w_in.astype(BF16)
    qkvs = []
    for gi, dil in enumerate(DILATIONS):
        cos, sin = tabs[dil]
        qkvs.append(_norm_proj(x, g, w, cos, sin, dil=dil, col_block=gi, ncols=3 * D, tc=D,
                               rope_chunks=_QKV_ROPE_CHUNKS))
    o = _attn_a(qkvs, B, S)
    return _out_proj(x.reshape(B * S, D), o.reshape(B * S, D), w_out.astype(BF16)).reshape(B, S, D)


def _mixer_b(x, g, w_in, lq1, lk1, lq2, lk2, subln, w_out, lambda_init, tabs, side_casts=()):
    B, S, D = x.shape
    cos, sin = tabs[1]
    qkv = _norm_proj(x, g, w_in.astype(BF16), cos, sin, dil=1, col_block=0, ncols=3 * D, tc=D,
                     rope_chunks=_QKV_ROPE_CHUNKS).reshape(B, S, 3 * D)
    lam_params = jnp.stack([lq1, lk1, lq2, lk2]).astype(F32)
    o, casted = _attn_b(qkv, lam_params, subln.astype(F32), lambda_init, B, S, side_casts)
    return _out_proj(x.reshape(B * S, D), o.reshape(B * S, D), w_out.astype(BF16)).reshape(B, S, D), casted


def _mixer_c(x, g, w_in, sink, w_out, tabs):
    B, S, D = x.shape
    cos, sin = tabs[1]
    head_order = jnp.arange(16).reshape(2, 8).T.reshape(-1)
    col_order = (head_order[:, None] * HEAD_DIM + jnp.arange(HEAD_DIM)[None, :]).reshape(-1)
    w_q = w_in[:, :D][:, col_order]
    w = jnp.concatenate([w_q, w_in[:, D:]], axis=1).astype(BF16)
    ncols = w.shape[1]
    chunks = tuple(c < (ncols // LANES - 1) for c in range(ncols // LANES))
    qkv = _norm_proj(x, g, w, cos, sin, dil=1, col_block=0, ncols=ncols, tc=ncols,
                     rope_chunks=chunks).reshape(B, S, ncols)
    sk = sink.astype(F32)
    sink_tab = jnp.concatenate([jnp.broadcast_to(sk[:8, None], (8, LANES)),
                                jnp.broadcast_to(sk[8:, None], (8, LANES))], axis=1)
    o = _attn_c(qkv, sink_tab, B, S)
    w_o = w_out[col_order, :].astype(BF16)
    return _out_proj(x.reshape(B * S, D), o.reshape(B * S, D), w_o).reshape(B, S, D)


def kernel(x, positions, l0_norm_mix, l0_a_w_in, l0_a_w_out, l0_norm_ffn, l0_ffn_w1, l0_ffn_w3, l0_ffn_w2, l1_norm_mix, l1_b_w_in, l1_b_lambda_q1, l1_b_lambda_k1, l1_b_lambda_q2, l1_b_lambda_k2, l1_b_subln, l1_b_w_out, l1_norm_ffn, l1_moe_router, l1_moe_w1, l1_moe_w3, l1_moe_w2, l2_norm_mix, l2_c_w_in, l2_c_sink, l2_c_w_out, l2_norm_ffn, l2_ffn_w1, l2_ffn_w3, l2_ffn_w2, l3_norm_mix, l3_a_w_in, l3_a_w_out, l3_norm_ffn, l3_moe_router, l3_moe_w1, l3_moe_w3, l3_moe_w2, final_norm):
    B, S, D = x.shape
    T = B * S
    cos, sin = lax.optimization_barrier(_rope_tables(positions))
    tabs = {d: (_to_strided(cos, d), _to_strided(sin, d)) for d in DILATIONS}

    x = _mixer_a(x, l0_norm_mix, l0_a_w_in, l0_a_w_out, tabs)
    x = _ffn_dense(x.reshape(T, D), l0_norm_ffn, l0_ffn_w1.astype(BF16), l0_ffn_w3.astype(BF16),
                   l0_ffn_w2.astype(BF16)).reshape(B, S, D)
    lambda_init = 0.8 - 0.6 * math.exp(-0.3 * 1)
    moe_w = (l1_moe_w1, l1_moe_w3, l1_moe_w2, l3_moe_w1, l3_moe_w3, l3_moe_w2)
    x, moe_bf = _mixer_b(x, l1_norm_mix, l1_b_w_in, l1_b_lambda_q1, l1_b_lambda_k1, l1_b_lambda_q2,
                         l1_b_lambda_k2, l1_b_subln, l1_b_w_out, lambda_init, tabs,
                         side_casts=tuple(w.reshape(-1, w.shape[-1]) for w in moe_w))
    moe_bf = [wb.reshape(w.shape) for wb, w in zip(moe_bf, moe_w)]
    x = _moe_layer(x.reshape(T, D), l1_norm_ffn, l1_moe_router, moe_bf[0], moe_bf[1], moe_bf[2],
                   final_norm, False).reshape(B, S, D)
    x = _mixer_c(x, l2_norm_mix, l2_c_w_in, l2_c_sink, l2_c_w_out, tabs)
    x = _ffn_dense(x.reshape(T, D), l2_norm_ffn, l2_ffn_w1.astype(BF16), l2_ffn_w3.astype(BF16),
                   l2_ffn_w2.astype(BF16)).reshape(B, S, D)
    x = _mixer_a(x, l3_norm_mix, l3_a_w_in, l3_a_w_out, tabs)
    x = _moe_layer(x.reshape(T, D), l3_norm_ffn, l3_moe_router, moe_bf[3], moe_bf[4], moe_bf[5],
                   final_norm, True).reshape(B, S, D)
    return x
```

```python
import functools
import math

import jax
import jax.numpy as jnp
from jax import lax
from jax.experimental import pallas as pl
from jax.experimental.pallas import tpu as pltpu

F32 = jnp.float32
BF16 = jnp.bfloat16

D_MODEL = 1024
HEAD_DIM = 64
ROT_DIM = HEAD_DIM // 4
ROPE_THETA = 500000.0
NORM_EPS = 1e-6
SUBLN_EPS = 1e-5
DILATIONS = (1, 4, 16)
A_HALF = 64
C_HALF = 128
N_EXPERTS = 8
LANES = 128
VMEM_LIMIT = 56 * 1024 * 1024

MOE_BLOCK = 512
_QKV_ROPE_CHUNKS = (True,) * 16 + (False,) * 8
CK = 512
A_GROUP = 3


def _cparams(sem):
    return pltpu.CompilerParams(dimension_semantics=sem, vmem_limit_bytes=VMEM_LIMIT)


def _rope_tables(positions):
    lane = jnp.arange(LANES) % HEAD_DIM
    half = ROT_DIM // 2
    inv_freq = ROPE_THETA ** (-((2 * (lane % half)).astype(F32) / ROT_DIM))
    freq = jnp.where(lane < ROT_DIM, inv_freq, 0.0)
    sign = jnp.where(lane < half, -1.0, jnp.where(lane < ROT_DIM, 1.0, 0.0))
    ang = positions.astype(F32)[..., None] * freq
    return jnp.cos(ang), jnp.sin(ang) * sign


def _to_strided(t, d):
    B, S, C = t.shape
    return jnp.swapaxes(t.reshape(B, S // d, d, C), 1, 2)


def _rope_chunk(x, cos, sin, first_half):
    partner = jnp.where(first_half, pltpu.roll(x, LANES - ROT_DIM // 2, 1),
                        pltpu.roll(x, ROT_DIM // 2, 1))
    return x * cos + partner * sin


def _proj_kernel(*refs, dil, rope_chunks, tc):
    if dil > 1:
        x_ref, g_ref, w_ref, cos_ref, sin_ref, perm_ref, o_ref = refs
    else:
        x_ref, g_ref, w_ref, cos_ref, sin_ref, o_ref = refs
    tm = x_ref.shape[0]
    ncols = w_ref.shape[1]
    n = tm // dil
    x = x_ref[...]
    ms = jnp.mean(x * x, axis=-1, keepdims=True)
    h = (x * lax.rsqrt(ms + NORM_EPS) * g_ref[...]).astype(BF16)
    if dil > 1:
        h = jnp.dot(perm_ref[...], h, preferred_element_type=F32).astype(BF16)
    lane = lax.broadcasted_iota(jnp.int32, (1, LANES), 1)
    first_half = (lane % HEAD_DIM) < (ROT_DIM // 2)
    cos = cos_ref[...].reshape(tm, LANES)
    sin = sin_ref[...].reshape(tm, LANES)
    for c0 in range(0, ncols, tc):
        res = jnp.dot(h, w_ref[:, c0:c0 + tc], preferred_element_type=F32)
        parts = []
        for cc in range(tc // LANES):
            xc = res[:, cc * LANES:(cc + 1) * LANES]
            if rope_chunks[c0 // LANES + cc]:
                xc = _rope_chunk(xc, cos, sin, first_half)
            parts.append(xc.astype(o_ref.dtype))
        val = jnp.concatenate(parts, axis=1)
        for r in range(dil):
            o_ref[r, :, c0:c0 + tc] = val[r * n:(r + 1) * n]


def _norm_proj(x, g, w, cos, sin, *, dil, col_block, ncols, tc, rope_chunks, tm=512):
    B, S, D = x.shape
    n = tm // dil
    L = S // dil
    in_specs = [
        pl.BlockSpec((None, tm, D), lambda b, i: (b, i, 0)),
        pl.BlockSpec((1, D), lambda b, i: (0, 0)),
        pl.BlockSpec((D, ncols), lambda b, i: (0, col_block)),
        pl.BlockSpec((None, dil, n, LANES), lambda b, i: (b, 0, i, 0)),
        pl.BlockSpec((None, dil, n, LANES), lambda b, i: (b, 0, i, 0)),
    ]
    args = [x, g.reshape(1, D), w, cos, sin]
    if dil > 1:
        p = jnp.arange(tm)
        src = (p % n) * dil + p // n
        perm = (src[:, None] == jnp.arange(tm)[None, :]).astype(BF16)
        in_specs.append(pl.BlockSpec((tm, tm), lambda b, i: (0, 0)))
        args.append(perm)
    return pl.pallas_call(
        functools.partial(_proj_kernel, dil=dil, rope_chunks=rope_chunks, tc=tc),
        grid=(B, S // tm),
        in_specs=in_specs,
        out_specs=pl.BlockSpec((None, dil, n, ncols), lambda b, i: (b, 0, i, 0)),
        out_shape=jax.ShapeDtypeStruct((B, dil, L, ncols), BF16),
        compiler_params=_cparams(("parallel", "parallel")),
        name=f"norm_proj_d{dil}",
    )(*args)


def _fill_band_bias(bias_scr, tq, W, half):
    kk = lax.broadcasted_iota(jnp.int32, (W, 2 * tq), 0)
    qq = lax.broadcasted_iota(jnp.int32, (W, 2 * tq), 1) % tq
    for i in range(3):
        ok = jnp.abs(qq + i * half - kk) <= half
        bias_scr[i] = jnp.where(ok, 0.0, -jnp.inf).astype(F32)


def _band_scores(q2, kw, bias):
    lane_lo = lax.broadcasted_iota(jnp.int32, (1, LANES), 1) < HEAD_DIM
    zero = jnp.zeros_like(q2)
    qq = jnp.concatenate([jnp.where(lane_lo, q2, zero), jnp.where(lane_lo, zero, q2)], axis=0)
    qq = qq * jnp.asarray(HEAD_DIM ** -0.5, q2.dtype)
    return lax.dot_general(kw, qq, (((1,), (1,)), ((), ())), preferred_element_type=F32) + bias


def _band_finish(st, vw, want_lse, sink_row=None):
    tq = st.shape[1] // 2
    row_lo = lax.broadcasted_iota(jnp.int32, (LANES, 1), 0) < HEAD_DIM
    m = jnp.max(st, axis=0, keepdims=True)
    if sink_row is not None:
        m = jnp.maximum(m, sink_row)
    p = jnp.exp(st - m)
    l = jnp.sum(p, axis=0, keepdims=True)
    if sink_row is not None:
        l = l + jnp.exp(sink_row - m)
    ot = lax.dot_general(vw, p.astype(BF16), (((0,), (0,)), ((), ())), preferred_element_type=F32) / l
    o = jnp.where(row_lo, ot[:, :tq], ot[:, tq:]).T
    if not want_lse:
        return o, None
    lse_row = m + jnp.log(l)
    lse = jnp.where(row_lo, jnp.broadcast_to(lse_row[:, :tq], (LANES, tq)),
                    jnp.broadcast_to(lse_row[:, tq:], (LANES, tq))).T
    return o, lse


def _attn_a_kernel(*refs, TB, S):
    qkv = refs[:9]
    o_ref = refs[9]
    scr = refs[10:16]
    bias_scr = refs[16]
    t = pl.program_id(2)
    W = 4 * A_HALF

    @pl.when(t == 0)
    def _():
        _fill_band_bias(bias_scr, 128, W, A_HALF)

    subs = []
    for g, dil in enumerate(DILATIONS):
        nq = TB // dil
        tq = min(nq, 128)
        for r in range(dil):
            for jj in range(nq // tq):
                subs.append((g, dil, nq, tq, r, jj))

    def scores(sub):
        g, dil, nq, tq, r, jj = sub
        q_ref, k_ref = qkv[3 * g], qkv[3 * g + 1]
        L = S // dil
        qs = t * nq + jj * tq
        ws = pl.multiple_of(jnp.clip(qs - A_HALF, 0, L - W), A_HALF)
        q2 = q_ref[r, jj * tq:(jj + 1) * tq, :]
        return _band_scores(q2, k_ref[r, pl.ds(ws, W), :], bias_scr[(qs - ws) // A_HALF]), ws

    def finish(sub, st, ws):
        g, dil, nq, tq, r, jj = sub
        o_scr, l_scr = scr[2 * g], scr[2 * g + 1]
        o, lse = _band_finish(st, qkv[3 * g + 2][r, pl.ds(ws, W), :], True)
        row0 = jj * (tq * dil) + r
        if dil == 1:
            o_scr[row0:row0 + tq, :] = o
            l_scr[row0:row0 + tq, :] = lse
        else:
            o_scr[pl.ds(row0, tq, stride=dil), :] = o
            l_scr[pl.ds(row0, tq, stride=dil), :] = lse

    pend = [scores(sub) for sub in subs[:A_GROUP]]
    for n, sub in enumerate(subs):
        if n + A_GROUP < len(subs):
            pend.append(scores(subs[n + A_GROUP]))
        st, ws = pend[n]
        finish(sub, st, ws)

    l0, l1, l2 = scr[1][...], scr[3][...], scr[5][...]
    mx = jnp.maximum(jnp.maximum(l0, l1), l2)
    e0, e1, e2 = jnp.exp(l0 - mx), jnp.exp(l1 - mx), jnp.exp(l2 - mx)
    den = e0 + e1 + e2
    o = (e0 * scr[0][...] + e1 * scr[2][...] + e2 * scr[4][...]) / den
    o_ref[...] = o.astype(o_ref.dtype)


def _attn_a(qkvs, B, S, TB=2048):
    n_pairs = (16 * HEAD_DIM) // LANES
    in_specs, args = [], []
    for g, dil in enumerate(DILATIONS):
        L = S // dil
        in_specs += [
            pl.BlockSpec((None, dil, TB // dil, LANES), lambda b, hp, t: (b, 0, t, hp)),
            pl.BlockSpec((None, dil, L, LANES), lambda b, hp, t: (b, 0, 0, n_pairs + hp)),
            pl.BlockSpec((None, dil, L, LANES), lambda b, hp, t: (b, 0, 0, 2 * n_pairs + hp)),
        ]
        args += [qkvs[g]] * 3
    return pl.pallas_call(
        functools.partial(_attn_a_kernel, TB=TB, S=S),
        grid=(B, n_pairs, S // TB),
        in_specs=in_specs,
        out_specs=pl.BlockSpec((None, TB, LANES), lambda b, hp, t: (b, t, hp)),
        out_shape=jax.ShapeDtypeStruct((B, S, n_pairs * LANES), BF16),
        scratch_shapes=[pltpu.VMEM((TB, LANES), F32)] * 6 + [pltpu.VMEM((3, 4 * A_HALF, 256), F32)],
        compiler_params=_cparams(("parallel", "parallel", "arbitrary")),
        name="attn_dilated",
    )(*args)


def _attn_c_kernel(q_ref, k_ref, v_ref, sink_ref, o_ref, bias_scr, *, TB, S):
    t = pl.program_id(1)
    tq = 128
    W = tq + 2 * C_HALF
    n_pairs = q_ref.shape[1] // LANES

    @pl.when(t == 0)
    def _():
        _fill_band_bias(bias_scr, tq, W, C_HALF)

    def body(jj, carry):
        qs = t * TB + jj * tq
        ws = pl.multiple_of(jnp.clip(qs - C_HALF, 0, S - W), C_HALF)
        kw = k_ref[pl.ds(ws, W), :]
        vw = v_ref[pl.ds(ws, W), :]
        bias = bias_scr[(qs - ws) // C_HALF]
        rows = pl.ds(pl.multiple_of(jj * tq, tq), tq)
        def scores(p):
            return _band_scores(q_ref[rows, p * LANES:(p + 1) * LANES], kw, bias)

        pend = [scores(p) for p in range(A_GROUP)]
        for p in range(n_pairs):
            if p + A_GROUP < n_pairs:
                pend.append(scores(p + A_GROUP))
            sink_row = sink_ref[p:p + 1, :]
            o, _ = _band_finish(pend[p], vw, False, sink_row=sink_row)
            o_ref[rows, p * LANES:(p + 1) * LANES] = o.astype(o_ref.dtype)
        return carry

    lax.fori_loop(0, TB // tq, body, 0)


def _attn_c(qkv, sink_tab, B, S, TB=1024):
    nq = 16 * HEAD_DIM
    return pl.pallas_call(
        functools.partial(_attn_c_kernel, TB=TB, S=S),
        grid=(B, S // TB),
        in_specs=[
            pl.BlockSpec((None, TB, nq), lambda b, t: (b, t, 0)),
            pl.BlockSpec((None, S, LANES), lambda b, t: (b, 0, nq // LANES)),
            pl.BlockSpec((None, S, LANES), lambda b, t: (b, 0, nq // LANES + 1)),
            pl.BlockSpec((8, 2 * LANES), lambda b, t: (0, 0)),
        ],
        out_specs=pl.BlockSpec((None, TB, nq), lambda b, t: (b, t, 0)),
        out_shape=jax.ShapeDtypeStruct((B, S, nq), BF16),
        scratch_shapes=[pltpu.VMEM((3, LANES + 2 * C_HALF, 2 * LANES), F32)],
        compiler_params=_cparams(("parallel", "arbitrary")),
        name="attn_swa_sink",
    )(qkv, qkv, qkv, sink_tab)


def _attn_b_kernel(*refs, lambda_init, n_cast):
    lam_ref, q_ref, qn_ref, k_ref, kn_ref, v_ref, subln_ref = refs[:7]
    cast_in = refs[7:7 + n_cast]
    o_ref = refs[7 + n_cast]
    cast_out = refs[8 + n_cast:8 + 2 * n_cast]
    s0_scr, s1_scr, m_scr, l_scr, acc_scr = refs[8 + 2 * n_cast:]
    _attn_b_body(lam_ref, q_ref, qn_ref, k_ref, kn_ref, v_ref, subln_ref, o_ref,
                 s0_scr, s1_scr, m_scr, l_scr, acc_scr, lambda_init=lambda_init)
    for src_ref, dst_ref in zip(cast_in, cast_out):
        dst_ref[...] = src_ref[...].astype(dst_ref.dtype)


def _attn_b_body(lam_ref, q_ref, qn_ref, k_ref, kn_ref, v_ref, subln_ref, o_ref,
                 s0_scr, s1_scr, m_scr, l_scr, acc_scr, *, lambda_init):
    i = pl.program_id(2)
    kv = pl.program_id(3)
    nkv = pl.num_programs(3)
    tk = k_ref.shape[0]
    lane = lax.broadcasted_iota(jnp.int32, (1, LANES), 1)
    lane_lo = lane < HEAD_DIM

    def scores(kref, qref, c, j):
        k = kref[j * CK:(j + 1) * CK, :] * jnp.asarray(HEAD_DIM ** -0.5, kref.dtype)
        sel = lane_lo if c == 0 else jnp.logical_not(lane_lo)
        kc = jnp.where(sel, k, jnp.zeros_like(k))
        return lax.dot_general(kc, qref[...], (((1,), (1,)), ((), ())), preferred_element_type=F32)

    @pl.when(kv == 0)
    def _():
        m_scr[...] = jnp.full(m_scr.shape, -jnp.inf, F32)
        l_scr[...] = jnp.zeros(l_scr.shape, F32)
        acc_scr[...] = jnp.zeros(acc_scr.shape, F32)

    @pl.when(jnp.logical_and(i == 0, kv == 0))
    def _():
        for c in range(2):
            for j in range(tk // CK):
                s0_scr[c * tk + j * CK:c * tk + (j + 1) * CK, :] = scores(k_ref, q_ref, c, j)

    def step(cur_scr, nxt_scr):
        vt = jnp.concatenate([v_ref[...].T, jnp.ones((16, tk), BF16)], axis=0)
        for c in range(2):
            m, l, acc = m_scr[c], l_scr[c], acc_scr[c]
            for j in range(tk // CK):
                rows = slice(c * tk + j * CK, c * tk + (j + 1) * CK)
                nxt_scr[rows, :] = scores(kn_ref, qn_ref, c, j)
                s = cur_scr[rows, :]
                m_new = jnp.maximum(m, jnp.max(s, axis=0, keepdims=True))
                alpha = jnp.exp(m - m_new)
                p = jnp.exp(s - m_new).astype(BF16)
                pv = jnp.dot(vt[:, j * CK:(j + 1) * CK], p, preferred_element_type=F32)
                l = alpha * l + pv[LANES:LANES + 1]
                acc = alpha * acc + pv[:LANES]
                m = m_new
            m_scr[c], l_scr[c], acc_scr[c] = m, l, acc

    @pl.when(kv % 2 == 0)
    def _():
        step(s0_scr, s1_scr)

    @pl.when(kv % 2 == 1)
    def _():
        step(s1_scr, s0_scr)

    @pl.when(kv == nkv - 1)
    def _():
        lp = lam_ref[...]
        lam = (jnp.exp(jnp.sum(lp[0:1] * lp[1:2], axis=-1, keepdims=True))
               - jnp.exp(jnp.sum(lp[2:3] * lp[3:4], axis=-1, keepdims=True)) + lambda_init)
        o = acc_scr[0] / l_scr[0] - lam * (acc_scr[1] / l_scr[1])
        ms = jnp.mean(o * o, axis=0, keepdims=True)
        o = o * lax.rsqrt(ms + SUBLN_EPS) * subln_ref[...] * (1.0 - lambda_init)
        o_ref[...] = o.T.astype(o_ref.dtype)


def _attn_b(qkv, lam_params, subln, lambda_init, B, S, side_casts=(), tq=1024, tk=2048):
    H = 8
    nq, nkv = S // tq, S // tk
    assert nkv % 2 == 0
    nsteps = B * H * nq * nkv
    cast_specs, cast_shapes = [], []
    for w in side_casts:
        rows, cols = w.shape
        nblk = nsteps
        while rows % nblk or (rows // nblk) % 16:
            nblk //= 2
        per = nsteps // nblk

        def w_map(b, h, i, kv, per=per):
            return ((((b * H + h) * nq + i) * nkv + kv) // per, 0)

        cast_specs.append(pl.BlockSpec((rows // nblk, cols), w_map))
        cast_shapes.append(jax.ShapeDtypeStruct((rows, cols), BF16))

    def q_next(b, h, i, kv):
        return (b, jnp.minimum(i + (kv + 1) // nkv, nq - 1), h)

    def k_next(b, h, i, kv):
        return (b, (kv + 1) % nkv, H + h)

    outs = pl.pallas_call(
        functools.partial(_attn_b_kernel, lambda_init=lambda_init, n_cast=len(side_casts)),
        grid=(B, H, nq, nkv),
        in_specs=[
            pl.BlockSpec((4, HEAD_DIM), lambda b, h, i, kv: (0, 0)),
            pl.BlockSpec((None, tq, LANES), lambda b, h, i, kv: (b, i, h)),
            pl.BlockSpec((None, tq, LANES), q_next),
            pl.BlockSpec((None, tk, LANES), lambda b, h, i, kv: (b, kv, H + h)),
            pl.BlockSpec((None, tk, LANES), k_next),
            pl.BlockSpec((None, tk, LANES), lambda b, h, i, kv: (b, kv, 2 * H + h)),
            pl.BlockSpec((LANES, 1), lambda b, h, i, kv: (0, 0)),
        ] + cast_specs,
        out_specs=[pl.BlockSpec((None, tq, LANES), lambda b, h, i, kv: (b, i, h))] + cast_specs,
        out_shape=[jax.ShapeDtypeStruct((B, S, H * LANES), BF16)] + cast_shapes,
        scratch_shapes=[pltpu.VMEM((2 * tk, tq), F32), pltpu.VMEM((2 * tk, tq), F32),
                        pltpu.VMEM((2, 1, tq), F32), pltpu.VMEM((2, 1, tq), F32),
                        pltpu.VMEM((2, LANES, tq), F32)],
        compiler_params=_cparams(("parallel", "parallel", "arbitrary", "arbitrary")),
        name="attn_diff",
    )(lam_params, qkv, qkv, qkv, qkv, qkv, subln.reshape(LANES, 1), *side_casts)
    return outs[0], outs[1:]


def _ffn_kernel(x_ref, o_ref, wo_ref, g_ref, w1_ref, w3_ref, w2_ref, y_ref):
    x = x_ref[...] + jnp.dot(o_ref[...], wo_ref[...], preferred_element_type=F32)
    ms = jnp.mean(x * x, axis=-1, keepdims=True)
    h = (x * lax.rsqrt(ms + NORM_EPS) * g_ref[...]).astype(BF16)
    u = jnp.dot(h, w1_ref[...], preferred_element_type=F32)
    v = jnp.dot(h, w3_ref[...], preferred_element_type=F32)
    a = (u * jax.nn.sigmoid(u) * v).astype(BF16)
    y_ref[...] = x + jnp.dot(a, w2_ref[...], preferred_element_type=F32)


def _ffn_dense(x2, o2, wo, g, w1, w3, w2, tm=512):
    T, D = x2.shape
    K = o2.shape[1]
    F = w1.shape[1]
    return pl.pallas_call(
        _ffn_kernel,
        grid=(T // tm,),
        in_specs=[pl.BlockSpec((tm, D), lambda i: (i, 0)),
                  pl.BlockSpec((tm, K), lambda i: (i, 0)),
                  pl.BlockSpec((K, D), lambda i: (0, 0)),
                  pl.BlockSpec((1, D), lambda i: (0, 0)),
                  pl.BlockSpec((D, F), lambda i: (0, 0)),
                  pl.BlockSpec((D, F), lambda i: (0, 0)),
                  pl.BlockSpec((F, D), lambda i: (0, 0))],
        out_specs=pl.BlockSpec((tm, D), lambda i: (i, 0)),
        out_shape=jax.ShapeDtypeStruct((T, D), F32),
        compiler_params=_cparams(("parallel",)),
        name="ffn_dense",
    )(x2, o2, wo, g.reshape(1, D), w1, w3, w2)


ROW_TILE = 8


def _rows_to_tiles(ref, val):
    n = val.shape[0]
    for a in range(ROW_TILE):
        ref[pl.ds(a, n, stride=ROW_TILE), :] = val[:, a * LANES:(a + 1) * LANES]


def _tiles_to_rows(ref, n):
    return jnp.concatenate([ref[pl.ds(a, n, stride=ROW_TILE), :] for a in range(ROW_TILE)], axis=1)


def _router_kernel(x_ref, o_ref, wo_ref, g_ref, r_ref, x1_ref, h_ref, mi_ref, mf_ref, cnt_ref, carry_scr):
    i = pl.program_id(0)
    tm = x_ref.shape[0]

    @pl.when(i == 0)
    def _():
        carry_scr[...] = jnp.zeros(carry_scr.shape, F32)

    x = x_ref[...] + jnp.dot(o_ref[...], wo_ref[...], preferred_element_type=F32)
    x1_ref[...] = x
    ms = jnp.mean(x * x, axis=-1, keepdims=True)
    h = x * lax.rsqrt(ms + NORM_EPS) * g_ref[...]
    _rows_to_tiles(h_ref, h)
    lane = lax.broadcasted_iota(jnp.int32, (tm, LANES), 1)
    logits = jnp.full((tm, LANES), -jnp.inf, F32)
    for e in range(N_EXPERTS):
        col = jnp.sum(h * r_ref[e:e + 1, :], axis=-1, keepdims=True)
        logits = jnp.where(lane == e, col, logits)
    v0 = jnp.max(logits, axis=-1, keepdims=True)
    i0 = jnp.min(jnp.where(logits == v0, lane, LANES), axis=-1, keepdims=True)
    rest = jnp.where(lane == i0, -jnp.inf, logits)
    v1 = jnp.max(rest, axis=-1, keepdims=True)
    i1 = jnp.min(jnp.where(rest == v1, lane, LANES), axis=-1, keepdims=True)
    tt = jnp.exp(v1 - v0)
    g0 = 1.0 / (1.0 + tt)
    g1 = tt / (1.0 + tt)
    sel0 = lane == i0
    sel1 = lane == i1
    onehot = jnp.where(jnp.logical_or(sel0, sel1), 1.0, 0.0)
    row = lax.broadcasted_iota(jnp.int32, (tm, tm), 0)
    col = lax.broadcasted_iota(jnp.int32, (tm, tm), 1)
    tri = jnp.where(row > col, 1.0, 0.0).astype(BF16)
    before = carry_scr[...] + jnp.dot(tri, onehot.astype(BF16), preferred_element_type=F32)
    rank0 = jnp.sum(jnp.where(sel0, before, 0.0), axis=-1, keepdims=True)
    rank1 = jnp.sum(jnp.where(sel1, before, 0.0), axis=-1, keepdims=True)
    carry_scr[...] = carry_scr[...] + jnp.sum(onehot, axis=0, keepdims=True)
    mi = jnp.where(lane == 0, i0, jnp.where(lane == 1, i1, 0))
    mi = jnp.where(lane == 2, rank0.astype(jnp.int32), jnp.where(lane == 3, rank1.astype(jnp.int32), mi))
    mi_ref[...] = mi
    mf_ref[...] = jnp.where(lane == 0, g0, jnp.where(lane == 1, g1, 0.0))
    cnt_ref[...] = carry_scr[...]


def _router(x2, o2, wo, g, router, tm=512):
    T, D = x2.shape
    K = o2.shape[1]
    rt = router.astype(F32).T
    return pl.pallas_call(
        _router_kernel,
        grid=(T // tm,),
        in_specs=[pl.BlockSpec((tm, D), lambda i: (i, 0)),
                  pl.BlockSpec((tm, K), lambda i: (i, 0)),
                  pl.BlockSpec((K, D), lambda i: (0, 0)),
                  pl.BlockSpec((1, D), lambda i: (0, 0)),
                  pl.BlockSpec((N_EXPERTS, D), lambda i: (0, 0))],
        out_specs=[pl.BlockSpec((tm, D), lambda i: (i, 0)),
                   pl.BlockSpec((tm * ROW_TILE, LANES), lambda i: (i, 0)),
                   pl.BlockSpec((tm, LANES), lambda i: (i, 0)),
                   pl.BlockSpec((tm, LANES), lambda i: (i, 0)),
                   pl.BlockSpec((1, LANES), lambda i: (0, 0))],
        out_shape=[jax.ShapeDtypeStruct((T, D), F32),
                   jax.ShapeDtypeStruct((T * ROW_TILE, LANES), F32),
                   jax.ShapeDtypeStruct((T, LANES), jnp.int32),
                   jax.ShapeDtypeStruct((T, LANES), F32),
                   jax.ShapeDtypeStruct((1, LANES), F32)],
        scratch_shapes=[pltpu.VMEM((1, LANES), F32)],
        compiler_params=_cparams(("arbitrary",)),
        name="moe_router",
    )(x2, o2, wo, g.reshape(1, D), rt)


def _dispatch_kernel(dest_ref, h_ref, buf_in, buf_hbm, sem, *, tm):
    del buf_in

    def copy(n, slot):
        return pltpu.make_async_copy(h_ref.at[pl.ds(pl.multiple_of(n * ROW_TILE, ROW_TILE), ROW_TILE)],
                                     buf_hbm.at[pl.ds(pl.multiple_of(slot * ROW_TILE, ROW_TILE), ROW_TILE)], sem)

    def issue(n, carry):
        copy(n, dest_ref[2 * n]).start(priority=0)
        copy(n, dest_ref[2 * n + 1]).start(priority=1)
        return carry

    lax.fori_loop(0, tm, issue, 0)

    def drain(n, carry):
        copy(0, 0).wait()
        copy(0, 0).wait()
        return carry

    lax.fori_loop(0, tm, drain, 0)


def _dispatch(h2, dest_flat, n_rows, tm=512):
    T = h2.shape[0] // ROW_TILE
    buf0 = jnp.zeros((n_rows * ROW_TILE, LANES), F32)
    return pl.pallas_call(
        functools.partial(_dispatch_kernel, tm=tm),
        grid=(T // tm,),
        in_specs=[pl.BlockSpec((2 * tm,), lambda i: (i,), memory_space=pltpu.SMEM),
                  pl.BlockSpec((tm * ROW_TILE, LANES), lambda i: (i, 0)),
                  pl.BlockSpec(memory_space=pl.ANY)],
        out_specs=pl.BlockSpec(memory_space=pl.ANY),
        out_shape=jax.ShapeDtypeStruct((n_rows * ROW_TILE, LANES), F32),
        scratch_shapes=[pltpu.SemaphoreType.DMA(())],
        input_output_aliases={2: 0},
        compiler_params=_cparams(("arbitrary",)),
        name="moe_dispatch",
    )(dest_flat, h2, buf0)


def _moe_ffn_kernel(be_ref, nu_ref, x_ref, w1_ref, w3_ref, w2_ref, y_ref, acc_scr):
    b = pl.program_id(0)
    f = pl.program_id(1)

    @pl.when(b < nu_ref[0])
    def _():
        @pl.when(f == 0)
        def _():
            acc_scr[...] = jnp.zeros(acc_scr.shape, F32)

        h = _tiles_to_rows(x_ref, MOE_BLOCK).astype(BF16)
        u = jnp.dot(h, w1_ref[...], preferred_element_type=F32)
        v = jnp.dot(h, w3_ref[...], preferred_element_type=F32)
        a = (u * jax.nn.sigmoid(u) * v).astype(BF16)
        acc_scr[...] += jnp.dot(a, w2_ref[...], preferred_element_type=F32)

        @pl.when(f == pl.num_programs(1) - 1)
        def _():
            _rows_to_tiles(y_ref, acc_scr[...])

    @pl.when(jnp.logical_and(b >= nu_ref[0], f == 0))
    def _():
        y_ref[...] = jnp.zeros(y_ref.shape, F32)


def _moe_ffn(buf, block_expert, n_used, w1, w3, w2, tf=1792):
    R = buf.shape[0] // ROW_TILE
    E, D, F = w1.shape
    nb = R // MOE_BLOCK

    def last_live(b, nu):
        return jnp.maximum(jnp.minimum(b, nu[0] - 1), 0)

    def row_map(b, f, be, nu):
        return (last_live(b, nu), 0)

    def w_in_map(b, f, be, nu):
        live = b < nu[0]
        return (be[last_live(b, nu)], 0, jnp.where(live, f, F // tf - 1))

    def w_out_map(b, f, be, nu):
        live = b < nu[0]
        return (be[last_live(b, nu)], jnp.where(live, f, F // tf - 1), 0)

    return pl.pallas_call(
        _moe_ffn_kernel,
        grid_spec=pltpu.PrefetchScalarGridSpec(
            num_scalar_prefetch=2,
            grid=(nb, F // tf),
            in_specs=[pl.BlockSpec((MOE_BLOCK * ROW_TILE, LANES), row_map),
                      pl.BlockSpec((None, D, tf), w_in_map),
                      pl.BlockSpec((None, D, tf), w_in_map),
                      pl.BlockSpec((None, tf, D), w_out_map)],
            out_specs=pl.BlockSpec((MOE_BLOCK * ROW_TILE, LANES), lambda b, f, be, nu: (b, 0)),
            scratch_shapes=[pltpu.VMEM((MOE_BLOCK, D), F32)],
        ),
        out_shape=jax.ShapeDtypeStruct((R * ROW_TILE, LANES), F32),
        compiler_params=_cparams(("arbitrary", "arbitrary")),
        name="moe_ffn",
    )(block_expert, n_used, buf, w1, w3, w2)


def _combine_kernel(dest_ref, x_ref, mf_ref, g_ref, ybuf_hbm, y_ref, rows0_scr, rows1_scr, sem, *, tm, final_norm):
    def copy(slot, k, n):
        dst = rows0_scr if k == 0 else rows1_scr
        return pltpu.make_async_copy(ybuf_hbm.at[pl.ds(pl.multiple_of(slot * ROW_TILE, ROW_TILE), ROW_TILE)],
                                     dst.at[pl.ds(pl.multiple_of(n * ROW_TILE, ROW_TILE), ROW_TILE)], sem)

    def issue(n, carry):
        copy(dest_ref[2 * n], 0, n).start(priority=0)
        copy(dest_ref[2 * n + 1], 1, n).start(priority=1)
        return carry

    lax.fori_loop(0, tm, issue, 0)

    def drain(n, carry):
        copy(0, 0, 0).wait()
        copy(0, 1, 0).wait()
        return carry

    lax.fori_loop(0, tm, drain, 0)

    mf = mf_ref[...]
    y = x_ref[...] + mf[:, 0:1] * _tiles_to_rows(rows0_scr, tm) + mf[:, 1:2] * _tiles_to_rows(rows1_scr, tm)
    if final_norm:
        ms = jnp.mean(y * y, axis=-1, keepdims=True)
        y = y * lax.rsqrt(ms + NORM_EPS) * g_ref[...]
    y_ref[...] = y


def _combine(x2, mf, dest_flat, ybuf, g_final, final_norm, tm=512):
    T, D = x2.shape
    return pl.pallas_call(
        functools.partial(_combine_kernel, tm=tm, final_norm=final_norm),
        grid=(T // tm,),
        in_specs=[pl.BlockSpec((2 * tm,), lambda i: (i,), memory_space=pltpu.SMEM),
                  pl.BlockSpec((tm, D), lambda i: (i, 0)),
                  pl.BlockSpec((tm, LANES), lambda i: (i, 0)),
                  pl.BlockSpec((1, D), lambda i: (0, 0)),
                  pl.BlockSpec(memory_space=pl.ANY)],
        out_specs=pl.BlockSpec((tm, D), lambda i: (i, 0)),
        out_shape=jax.ShapeDtypeStruct((T, D), F32),
        scratch_shapes=[pltpu.VMEM((tm * ROW_TILE, LANES), F32), pltpu.VMEM((tm * ROW_TILE, LANES), F32),
                        pltpu.SemaphoreType.DMA(())],
        compiler_params=_cparams(("arbitrary",)),
        name="moe_combine",
    )(dest_flat, x2, mf, g_final.reshape(1, D), ybuf)


def _moe_layer(x2, o2, wo, g, router, w1, w3, w2, g_final, final_norm):
    T, D = x2.shape
    x2, h2, mi, mf, cnt = _router(x2, o2, wo, g, router)
    counts = cnt[0, :N_EXPERTS].astype(jnp.int32)
    padded = ((counts + MOE_BLOCK - 1) // MOE_BLOCK) * MOE_BLOCK
    pend = jnp.cumsum(padded)
    pstart = pend - padded
    dest = pstart[mi[:, 0:2]] + mi[:, 2:4]
    dest_flat = dest.reshape(-1).astype(jnp.int32)
    nb = -(-(2 * T + N_EXPERTS * (MOE_BLOCK - 1)) // MOE_BLOCK)
    block_start = jnp.arange(nb, dtype=jnp.int32) * MOE_BLOCK
    block_expert = jnp.minimum(jnp.sum(pend[None, :] <= block_start[:, None], axis=1),
                               N_EXPERTS - 1).astype(jnp.int32)
    n_used = (pend[-1:] // MOE_BLOCK).astype(jnp.int32)
    buf = _dispatch(h2, dest_flat, nb * MOE_BLOCK)
    ybuf = _moe_ffn(buf, block_expert, n_used, w1, w3, w2)
    return _combine(x2, mf, dest_flat, ybuf, g_final, final_norm)


def _mixer_a(x, g, w_in, w_out, tabs):
    B, S, D = x.shape
    w = w_in.astype(BF16)
    qkvs = []
    for gi, dil in enumerate(DILATIONS):
        cos, sin = tabs[dil]
        qkvs.append(_norm_proj(x, g, w, cos, sin, dil=dil, col_block=gi, ncols=3 * D, tc=D,
                               rope_chunks=_QKV_ROPE_CHUNKS))
    o = _attn_a(qkvs, B, S)
    return o.reshape(B * S, D), w_out.astype(BF16)


def _mixer_b(x, g, w_in, lq1, lk1, lq2, lk2, subln, w_out, lambda_init, tabs, side_casts=()):
    B, S, D = x.shape
    cos, sin = tabs[1]
    qkv = _norm_proj(x, g, w_in.astype(BF16), cos, sin, dil=1, col_block=0, ncols=3 * D, tc=D,
                     rope_chunks=_QKV_ROPE_CHUNKS).reshape(B, S, 3 * D)
    lam_params = jnp.stack([lq1, lk1, lq2, lk2]).astype(F32)
    o, casted = _attn_b(qkv, lam_params, subln.astype(F32), lambda_init, B, S, side_casts)
    return o.reshape(B * S, D), w_out.astype(BF16), casted


def _mixer_c(x, g, w_in, sink, w_out, tabs):
    B, S, D = x.shape
    cos, sin = tabs[1]
    head_order = jnp.arange(16).reshape(2, 8).T.reshape(-1)
    col_order = (head_order[:, None] * HEAD_DIM + jnp.arange(HEAD_DIM)[None, :]).reshape(-1)
    w_q = w_in[:, :D][:, col_order]
    w = jnp.concatenate([w_q, w_in[:, D:]], axis=1).astype(BF16)
    ncols = w.shape[1]
    chunks = tuple(c < (ncols // LANES - 1) for c in range(ncols // LANES))
    qkv = _norm_proj(x, g, w, cos, sin, dil=1, col_block=0, ncols=ncols, tc=ncols,
                     rope_chunks=chunks).reshape(B, S, ncols)
    sk = sink.astype(F32)
    sink_tab = jnp.concatenate([jnp.broadcast_to(sk[:8, None], (8, LANES)),
                                jnp.broadcast_to(sk[8:, None], (8, LANES))], axis=1)
    o = _attn_c(qkv, sink_tab, B, S)
    w_o = w_out[col_order, :].astype(BF16)
    return o.reshape(B * S, D), w_o


def kernel(x, positions, l0_norm_mix, l0_a_w_in, l0_a_w_out, l0_norm_ffn, l0_ffn_w1, l0_ffn_w3, l0_ffn_w2, l1_norm_mix, l1_b_w_in, l1_b_lambda_q1, l1_b_lambda_k1, l1_b_lambda_q2, l1_b_lambda_k2, l1_b_subln, l1_b_w_out, l1_norm_ffn, l1_moe_router, l1_moe_w1, l1_moe_w3, l1_moe_w2, l2_norm_mix, l2_c_w_in, l2_c_sink, l2_c_w_out, l2_norm_ffn, l2_ffn_w1, l2_ffn_w3, l2_ffn_w2, l3_norm_mix, l3_a_w_in, l3_a_w_out, l3_norm_ffn, l3_moe_router, l3_moe_w1, l3_moe_w3, l3_moe_w2, final_norm):
    B, S, D = x.shape
    T = B * S
    cos, sin = lax.optimization_barrier(_rope_tables(positions))
    tabs = {d: (_to_strided(cos, d), _to_strided(sin, d)) for d in DILATIONS}

    o, wo = _mixer_a(x, l0_norm_mix, l0_a_w_in, l0_a_w_out, tabs)
    x = _ffn_dense(x.reshape(T, D), o, wo, l0_norm_ffn, l0_ffn_w1.astype(BF16), l0_ffn_w3.astype(BF16),
                   l0_ffn_w2.astype(BF16)).reshape(B, S, D)
    lambda_init = 0.8 - 0.6 * math.exp(-0.3 * 1)
    moe_w = (l1_moe_w1, l1_moe_w3, l1_moe_w2, l3_moe_w1, l3_moe_w3, l3_moe_w2)
    o, wo, moe_bf = _mixer_b(x, l1_norm_mix, l1_b_w_in, l1_b_lambda_q1, l1_b_lambda_k1, l1_b_lambda_q2,
                             l1_b_lambda_k2, l1_b_subln, l1_b_w_out, lambda_init, tabs,
                             side_casts=tuple(w.reshape(-1, w.shape[-1]) for w in moe_w))
    moe_bf = [wb.reshape(w.shape) for wb, w in zip(moe_bf, moe_w)]
    x = _moe_layer(x.reshape(T, D), o, wo, l1_norm_ffn, l1_moe_router, moe_bf[0], moe_bf[1], moe_bf[2],
                   final_norm, False).reshape(B, S, D)
    o, wo = _mixer_c(x, l2_norm_mix, l2_c_w_in, l2_c_sink, l2_c_w_out, tabs)
    x = _ffn_dense(x.reshape(T, D), o, wo, l2_norm_ffn, l2_ffn_w1.astype(BF16), l2_ffn_w3.astype(BF16),
                   l2_ffn_w2.astype(BF16)).reshape(B, S, D)
    o, wo = _mixer_a(x, l3_norm_mix, l3_a_w_in, l3_a_w_out, tabs)
    x = _moe_layer(x.reshape(T, D), o, wo, l3_norm_ffn, l3_moe_router, moe_bf[3], moe_bf[4], moe_bf[5],
                   final_norm, True).reshape(B, S, D)
    return x
```

```python
import functools
import math

import jax
import jax.numpy as jnp
from jax import lax
from jax.experimental import pallas as pl
from jax.experimental.pallas import tpu as pltpu

F32 = jnp.float32
BF16 = jnp.bfloat16

D_MODEL = 1024
HEAD_DIM = 64
ROT_DIM = HEAD_DIM // 4
ROPE_THETA = 500000.0
NORM_EPS = 1e-6
SUBLN_EPS = 1e-5
DILATIONS = (1, 4, 16)
A_HALF = 64
C_HALF = 128
N_EXPERTS = 8
LANES = 128
VMEM_LIMIT = 56 * 1024 * 1024

MOE_BLOCK = 512
_QKV_ROPE_CHUNKS = (True,) * 16 + (False,) * 8
CK = 512
A_GROUP = 3
C_AHEAD = 4


def _cparams(sem):
    return pltpu.CompilerParams(dimension_semantics=sem, vmem_limit_bytes=VMEM_LIMIT)


def _rope_tables(positions):
    lane = jnp.arange(LANES) % HEAD_DIM
    half = ROT_DIM // 2
    inv_freq = ROPE_THETA ** (-((2 * (lane % half)).astype(F32) / ROT_DIM))
    freq = jnp.where(lane < ROT_DIM, inv_freq, 0.0)
    sign = jnp.where(lane < half, -1.0, jnp.where(lane < ROT_DIM, 1.0, 0.0))
    ang = positions.astype(F32)[..., None] * freq
    return jnp.cos(ang), jnp.sin(ang) * sign


def _to_strided(t, d):
    B, S, C = t.shape
    return jnp.swapaxes(t.reshape(B, S // d, d, C), 1, 2)


def _rope_chunk(x, cos, sin, first_half):
    partner = jnp.where(first_half, pltpu.roll(x, LANES - ROT_DIM // 2, 1),
                        pltpu.roll(x, ROT_DIM // 2, 1))
    return x * cos + partner * sin


def _proj_kernel(*refs, dil, rope_chunks, tc):
    if dil > 1:
        x_ref, g_ref, w_ref, cos_ref, sin_ref, perm_ref, o_ref = refs
    else:
        x_ref, g_ref, w_ref, cos_ref, sin_ref, o_ref = refs
    tm = x_ref.shape[0]
    ncols = w_ref.shape[1]
    n = tm // dil
    x = x_ref[...]
    ms = jnp.mean(x * x, axis=-1, keepdims=True)
    h = (x * lax.rsqrt(ms + NORM_EPS) * g_ref[...]).astype(BF16)
    if dil > 1:
        h = jnp.dot(perm_ref[...], h, preferred_element_type=F32).astype(BF16)
    lane = lax.broadcasted_iota(jnp.int32, (1, LANES), 1)
    first_half = (lane % HEAD_DIM) < (ROT_DIM // 2)
    cos = cos_ref[...].reshape(tm, LANES)
    sin = sin_ref[...].reshape(tm, LANES)
    for c0 in range(0, ncols, tc):
        res = jnp.dot(h, w_ref[:, c0:c0 + tc], preferred_element_type=F32)
        parts = []
        for cc in range(tc // LANES):
            xc = res[:, cc * LANES:(cc + 1) * LANES]
            if rope_chunks[c0 // LANES + cc]:
                xc = _rope_chunk(xc, cos, sin, first_half)
            parts.append(xc.astype(o_ref.dtype))
        val = jnp.concatenate(parts, axis=1)
        for r in range(dil):
            o_ref[r, :, c0:c0 + tc] = val[r * n:(r + 1) * n]


def _norm_proj(x, g, w, cos, sin, *, dil, col_block, ncols, tc, rope_chunks, tm=512):
    B, S, D = x.shape
    n = tm // dil
    L = S // dil
    in_specs = [
        pl.BlockSpec((None, tm, D), lambda b, i: (b, i, 0)),
        pl.BlockSpec((1, D), lambda b, i: (0, 0)),
        pl.BlockSpec((D, ncols), lambda b, i: (0, col_block)),
        pl.BlockSpec((None, dil, n, LANES), lambda b, i: (b, 0, i, 0)),
        pl.BlockSpec((None, dil, n, LANES), lambda b, i: (b, 0, i, 0)),
    ]
    args = [x, g.reshape(1, D), w, cos, sin]
    if dil > 1:
        p = jnp.arange(tm)
        src = (p % n) * dil + p // n
        perm = (src[:, None] == jnp.arange(tm)[None, :]).astype(BF16)
        in_specs.append(pl.BlockSpec((tm, tm), lambda b, i: (0, 0)))
        args.append(perm)
    return pl.pallas_call(
        functools.partial(_proj_kernel, dil=dil, rope_chunks=rope_chunks, tc=tc),
        grid=(B, S // tm),
        in_specs=in_specs,
        out_specs=pl.BlockSpec((None, dil, n, ncols), lambda b, i: (b, 0, i, 0)),
        out_shape=jax.ShapeDtypeStruct((B, dil, L, ncols), BF16),
        compiler_params=_cparams(("parallel", "parallel")),
        name=f"norm_proj_d{dil}",
    )(*args)


def _fill_band_bias(bias_scr, tq, W, half):
    kk = lax.broadcasted_iota(jnp.int32, (W, 2 * tq), 0)
    qq = lax.broadcasted_iota(jnp.int32, (W, 2 * tq), 1) % tq
    for i in range(3):
        ok = jnp.abs(qq + i * half - kk) <= half
        bias_scr[i] = jnp.where(ok, 0.0, -jnp.inf).astype(F32)


def _band_scores(q2, kw, bias):
    lane_lo = lax.broadcasted_iota(jnp.int32, (1, LANES), 1) < HEAD_DIM
    zero = jnp.zeros_like(q2)
    qq = jnp.concatenate([jnp.where(lane_lo, q2, zero), jnp.where(lane_lo, zero, q2)], axis=0)
    qq = qq * jnp.asarray(HEAD_DIM ** -0.5, q2.dtype)
    return lax.dot_general(kw, qq, (((1,), (1,)), ((), ())), preferred_element_type=F32) + bias


def _band_finish(st, vw, want_lse, sink_row=None):
    tq = st.shape[1] // 2
    row_lo = lax.broadcasted_iota(jnp.int32, (LANES, 1), 0) < HEAD_DIM
    m = jnp.max(st, axis=0, keepdims=True)
    if sink_row is not None:
        m = jnp.maximum(m, sink_row)
    p = jnp.exp(st - m)
    l = jnp.sum(p, axis=0, keepdims=True)
    if sink_row is not None:
        l = l + jnp.exp(sink_row - m)
    ot = lax.dot_general(vw, p.astype(BF16), (((0,), (0,)), ((), ())), preferred_element_type=F32) / l
    o = jnp.where(row_lo, ot[:, :tq], ot[:, tq:]).T
    if not want_lse:
        return o, None
    lse_row = m + jnp.log(l)
    lse = jnp.where(row_lo, jnp.broadcast_to(lse_row[:, :tq], (LANES, tq)),
                    jnp.broadcast_to(lse_row[:, tq:], (LANES, tq))).T
    return o, lse


def _attn_a_kernel(*refs, TB, S):
    qkv = refs[:9]
    o_ref = refs[9]
    scr = refs[10:16]
    bias_scr = refs[16]
    t = pl.program_id(2)
    W = 4 * A_HALF

    @pl.when(t == 0)
    def _():
        _fill_band_bias(bias_scr, 128, W, A_HALF)

    subs = []
    for g, dil in enumerate(DILATIONS):
        nq = TB // dil
        tq = min(nq, 128)
        for r in range(dil):
            for jj in range(nq // tq):
                subs.append((g, dil, nq, tq, r, jj))

    def scores(sub):
        g, dil, nq, tq, r, jj = sub
        q_ref, k_ref = qkv[3 * g], qkv[3 * g + 1]
        L = S // dil
        qs = t * nq + jj * tq
        ws = pl.multiple_of(jnp.clip(qs - A_HALF, 0, L - W), A_HALF)
        q2 = q_ref[r, jj * tq:(jj + 1) * tq, :]
        return _band_scores(q2, k_ref[r, pl.ds(ws, W), :], bias_scr[(qs - ws) // A_HALF]), ws

    def finish(sub, st, ws):
        g, dil, nq, tq, r, jj = sub
        o_scr, l_scr = scr[2 * g], scr[2 * g + 1]
        o, lse = _band_finish(st, qkv[3 * g + 2][r, pl.ds(ws, W), :], True)
        row0 = jj * (tq * dil) + r
        if dil == 1:
            o_scr[row0:row0 + tq, :] = o
            l_scr[row0:row0 + tq, :] = lse
        else:
            o_scr[pl.ds(row0, tq, stride=dil), :] = o
            l_scr[pl.ds(row0, tq, stride=dil), :] = lse

    pend = [scores(sub) for sub in subs[:A_GROUP]]
    for n, sub in enumerate(subs):
        if n + A_GROUP < len(subs):
            pend.append(scores(subs[n + A_GROUP]))
        st, ws = pend[n]
        finish(sub, st, ws)

    l0, l1, l2 = scr[1][...], scr[3][...], scr[5][...]
    mx = jnp.maximum(jnp.maximum(l0, l1), l2)
    e0, e1, e2 = jnp.exp(l0 - mx), jnp.exp(l1 - mx), jnp.exp(l2 - mx)
    den = e0 + e1 + e2
    o = (e0 * scr[0][...] + e1 * scr[2][...] + e2 * scr[4][...]) / den
    o_ref[...] = o.astype(o_ref.dtype)


def _attn_a(qkvs, B, S, TB=2048):
    n_pairs = (16 * HEAD_DIM) // LANES
    in_specs, args = [], []
    for g, dil in enumerate(DILATIONS):
        L = S // dil
        in_specs += [
            pl.BlockSpec((None, dil, TB // dil, LANES), lambda b, hp, t: (b, 0, t, hp)),
            pl.BlockSpec((None, dil, L, LANES), lambda b, hp, t: (b, 0, 0, n_pairs + hp)),
            pl.BlockSpec((None, dil, L, LANES), lambda b, hp, t: (b, 0, 0, 2 * n_pairs + hp)),
        ]
        args += [qkvs[g]] * 3
    return pl.pallas_call(
        functools.partial(_attn_a_kernel, TB=TB, S=S),
        grid=(B, n_pairs, S // TB),
        in_specs=in_specs,
        out_specs=pl.BlockSpec((None, TB, LANES), lambda b, hp, t: (b, t, hp)),
        out_shape=jax.ShapeDtypeStruct((B, S, n_pairs * LANES), BF16),
        scratch_shapes=[pltpu.VMEM((TB, LANES), F32)] * 6 + [pltpu.VMEM((3, 4 * A_HALF, 256), F32)],
        compiler_params=_cparams(("parallel", "parallel", "arbitrary")),
        name="attn_dilated",
    )(*args)


def _attn_c_kernel(q_ref, k_ref, v_ref, sink_ref, o_ref, bias_scr, *, TB, S):
    t = pl.program_id(1)
    tq = 128
    W = tq + 2 * C_HALF
    n_pairs = q_ref.shape[1] // LANES

    @pl.when(t == 0)
    def _():
        _fill_band_bias(bias_scr, tq, W, C_HALF)

    units = [(jj, p) for jj in range(TB // tq) for p in range(n_pairs)]

    def window(jj):
        qs = t * TB + jj * tq
        ws = pl.multiple_of(jnp.clip(qs - C_HALF, 0, S - W), C_HALF)
        return qs, ws

    def scores(unit):
        jj, p = unit
        qs, ws = window(jj)
        q2 = q_ref[jj * tq:(jj + 1) * tq, p * LANES:(p + 1) * LANES]
        return _band_scores(q2, k_ref[pl.ds(ws, W), :], bias_scr[(qs - ws) // C_HALF])

    pend = [scores(u) for u in units[:C_AHEAD]]
    for n, (jj, p) in enumerate(units):
        if n + C_AHEAD < len(units):
            pend.append(scores(units[n + C_AHEAD]))
        _, ws = window(jj)
        sink_row = sink_ref[p:p + 1, :]
        o, _ = _band_finish(pend[n], v_ref[pl.ds(ws, W), :], False, sink_row=sink_row)
        pend[n] = None
        o_ref[jj * tq:(jj + 1) * tq, p * LANES:(p + 1) * LANES] = o.astype(o_ref.dtype)


def _attn_c(qkv, sink_tab, B, S, TB=1024):
    nq = 16 * HEAD_DIM
    return pl.pallas_call(
        functools.partial(_attn_c_kernel, TB=TB, S=S),
        grid=(B, S // TB),
        in_specs=[
            pl.BlockSpec((None, TB, nq), lambda b, t: (b, t, 0)),
            pl.BlockSpec((None, S, LANES), lambda b, t: (b, 0, nq // LANES)),
            pl.BlockSpec((None, S, LANES), lambda b, t: (b, 0, nq // LANES + 1)),
            pl.BlockSpec((8, 2 * LANES), lambda b, t: (0, 0)),
        ],
        out_specs=pl.BlockSpec((None, TB, nq), lambda b, t: (b, t, 0)),
        out_shape=jax.ShapeDtypeStruct((B, S, nq), BF16),
        scratch_shapes=[pltpu.VMEM((3, LANES + 2 * C_HALF, 2 * LANES), F32)],
        compiler_params=_cparams(("parallel", "arbitrary")),
        name="attn_swa_sink",
    )(qkv, qkv, qkv, sink_tab)


def _attn_b_kernel(*refs, lambda_init, n_cast):
    lam_ref, q_ref, qn_ref, k_ref, kn_ref, v_ref, subln_ref = refs[:7]
    cast_in = refs[7:7 + n_cast]
    o_ref = refs[7 + n_cast]
    cast_out = refs[8 + n_cast:8 + 2 * n_cast]
    s0_scr, s1_scr, m_scr, l_scr, acc_scr = refs[8 + 2 * n_cast:]
    _attn_b_body(lam_ref, q_ref, qn_ref, k_ref, kn_ref, v_ref, subln_ref, o_ref,
                 s0_scr, s1_scr, m_scr, l_scr, acc_scr, lambda_init=lambda_init)
    for src_ref, dst_ref in zip(cast_in, cast_out):
        dst_ref[...] = src_ref[...].astype(dst_ref.dtype)


def _attn_b_body(lam_ref, q_ref, qn_ref, k_ref, kn_ref, v_ref, subln_ref, o_ref,
                 s0_scr, s1_scr, m_scr, l_scr, acc_scr, *, lambda_init):
    i = pl.program_id(2)
    kv = pl.program_id(3)
    nkv = pl.num_programs(3)
    tk = k_ref.shape[0]
    lane = lax.broadcasted_iota(jnp.int32, (1, LANES), 1)
    lane_lo = lane < HEAD_DIM

    def scores(kref, qref, c, j):
        k = kref[j * CK:(j + 1) * CK, :] * jnp.asarray(HEAD_DIM ** -0.5, kref.dtype)
        sel = lane_lo if c == 0 else jnp.logical_not(lane_lo)
        kc = jnp.where(sel, k, jnp.zeros_like(k))
        return lax.dot_general(kc, qref[...], (((1,), (1,)), ((), ())), preferred_element_type=F32)

    @pl.when(kv == 0)
    def _():
        m_scr[...] = jnp.full(m_scr.shape, -jnp.inf, F32)
        l_scr[...] = jnp.zeros(l_scr.shape, F32)
        acc_scr[...] = jnp.zeros(acc_scr.shape, F32)

    @pl.when(jnp.logical_and(i == 0, kv == 0))
    def _():
        for c in range(2):
            for j in range(tk // CK):
                s0_scr[c * tk + j * CK:c * tk + (j + 1) * CK, :] = scores(k_ref, q_ref, c, j)

    def step(cur_scr, nxt_scr):
        vt = jnp.concatenate([v_ref[...].T, jnp.ones((16, tk), BF16)], axis=0)
        for c in range(2):
            m, l, acc = m_scr[c], l_scr[c], acc_scr[c]
            for j in range(tk // CK):
                rows = slice(c * tk + j * CK, c * tk + (j + 1) * CK)
                nxt_scr[rows, :] = scores(kn_ref, qn_ref, c, j)
                s = cur_scr[rows, :]
                m_new = jnp.maximum(m, jnp.max(s, axis=0, keepdims=True))
                alpha = jnp.exp(m - m_new)
                p = jnp.exp(s - m_new).astype(BF16)
                pv = jnp.dot(vt[:, j * CK:(j + 1) * CK], p, preferred_element_type=F32)
                l = alpha * l + pv[LANES:LANES + 1]
                acc = alpha * acc + pv[:LANES]
                m = m_new
            m_scr[c], l_scr[c], acc_scr[c] = m, l, acc

    @pl.when(kv % 2 == 0)
    def _():
        step(s0_scr, s1_scr)

    @pl.when(kv % 2 == 1)
    def _():
        step(s1_scr, s0_scr)

    @pl.when(kv == nkv - 1)
    def _():
        lp = lam_ref[...]
        lam = (jnp.exp(jnp.sum(lp[0:1] * lp[1:2], axis=-1, keepdims=True))
               - jnp.exp(jnp.sum(lp[2:3] * lp[3:4], axis=-1, keepdims=True)) + lambda_init)
        o = acc_scr[0] / l_scr[0] - lam * (acc_scr[1] / l_scr[1])
        ms = jnp.mean(o * o, axis=0, keepdims=True)
        o = o * lax.rsqrt(ms + SUBLN_EPS) * subln_ref[...] * (1.0 - lambda_init)
        o_ref[...] = o.T.astype(o_ref.dtype)


def _attn_b(qkv, lam_params, subln, lambda_init, B, S, side_casts=(), tq=1024, tk=2048):
    H = 8
    nq, nkv = S // tq, S // tk
    assert nkv % 2 == 0
    nsteps = B * H * nq * nkv
    cast_specs, cast_shapes = [], []
    for w in side_casts:
        rows, cols = w.shape
        nblk = nsteps
        while rows % nblk or (rows // nblk) % 16:
            nblk //= 2
        per = nsteps // nblk

        def w_map(b, h, i, kv, per=per):
            return ((((b * H + h) * nq + i) * nkv + kv) // per, 0)

        cast_specs.append(pl.BlockSpec((rows // nblk, cols), w_map))
        cast_shapes.append(jax.ShapeDtypeStruct((rows, cols), BF16))

    def q_next(b, h, i, kv):
        return (b, jnp.minimum(i + (kv + 1) // nkv, nq - 1), h)

    def k_next(b, h, i, kv):
        return (b, (kv + 1) % nkv, H + h)

    outs = pl.pallas_call(
        functools.partial(_attn_b_kernel, lambda_init=lambda_init, n_cast=len(side_casts)),
        grid=(B, H, nq, nkv),
        in_specs=[
            pl.BlockSpec((4, HEAD_DIM), lambda b, h, i, kv: (0, 0)),
            pl.BlockSpec((None, tq, LANES), lambda b, h, i, kv: (b, i, h)),
            pl.BlockSpec((None, tq, LANES), q_next),
            pl.BlockSpec((None, tk, LANES), lambda b, h, i, kv: (b, kv, H + h)),
            pl.BlockSpec((None, tk, LANES), k_next),
            pl.BlockSpec((None, tk, LANES), lambda b, h, i, kv: (b, kv, 2 * H + h)),
            pl.BlockSpec((LANES, 1), lambda b, h, i, kv: (0, 0)),
        ] + cast_specs,
        out_specs=[pl.BlockSpec((None, tq, LANES), lambda b, h, i, kv: (b, i, h))] + cast_specs,
        out_shape=[jax.ShapeDtypeStruct((B, S, H * LANES), BF16)] + cast_shapes,
        scratch_shapes=[pltpu.VMEM((2 * tk, tq), F32), pltpu.VMEM((2 * tk, tq), F32),
                        pltpu.VMEM((2, 1, tq), F32), pltpu.VMEM((2, 1, tq), F32),
                        pltpu.VMEM((2, LANES, tq), F32)],
        compiler_params=_cparams(("parallel", "parallel", "arbitrary", "arbitrary")),
        name="attn_diff",
    )(lam_params, qkv, qkv, qkv, qkv, qkv, subln.reshape(LANES, 1), *side_casts)
    return outs[0], outs[1:]


def _ffn_kernel(x_ref, o_ref, wo_ref, g_ref, w1_ref, w3_ref, w2_ref, y_ref):
    x = x_ref[...] + jnp.dot(o_ref[...], wo_ref[...], preferred_element_type=F32)
    ms = jnp.mean(x * x, axis=-1, keepdims=True)
    h = (x * lax.rsqrt(ms + NORM_EPS) * g_ref[...]).astype(BF16)
    u = jnp.dot(h, w1_ref[...], preferred_element_type=F32)
    v = jnp.dot(h, w3_ref[...], preferred_element_type=F32)
    a = (u * jax.nn.sigmoid(u) * v).astype(BF16)
    y_ref[...] = x + jnp.dot(a, w2_ref[...], preferred_element_type=F32)


def _ffn_dense(x2, o2, wo, g, w1, w3, w2, tm=512):
    T, D = x2.shape
    K = o2.shape[1]
    F = w1.shape[1]
    return pl.pallas_call(
        _ffn_kernel,
        grid=(T // tm,),
        in_specs=[pl.BlockSpec((tm, D), lambda i: (i, 0)),
                  pl.BlockSpec((tm, K), lambda i: (i, 0)),
                  pl.BlockSpec((K, D), lambda i: (0, 0)),
                  pl.BlockSpec((1, D), lambda i: (0, 0)),
                  pl.BlockSpec((D, F), lambda i: (0, 0)),
                  pl.BlockSpec((D, F), lambda i: (0, 0)),
                  pl.BlockSpec((F, D), lambda i: (0, 0))],
        out_specs=pl.BlockSpec((tm, D), lambda i: (i, 0)),
        out_shape=jax.ShapeDtypeStruct((T, D), F32),
        compiler_params=_cparams(("parallel",)),
        name="ffn_dense",
    )(x2, o2, wo, g.reshape(1, D), w1, w3, w2)


ROW_TILE = 8


def _rows_to_tiles(ref, val):
    n = val.shape[0]
    for a in range(ROW_TILE):
        ref[pl.ds(a, n, stride=ROW_TILE), :] = val[:, a * LANES:(a + 1) * LANES]


def _tiles_to_rows(ref, n):
    return jnp.concatenate([ref[pl.ds(a, n, stride=ROW_TILE), :] for a in range(ROW_TILE)], axis=1)


def _router_kernel(x_ref, o_ref, wo_ref, g_ref, r_ref, x1_ref, h_ref, mi_ref, mf_ref, cnt_ref, carry_scr):
    i = pl.program_id(0)
    tm = x_ref.shape[0]

    @pl.when(i == 0)
    def _():
        carry_scr[...] = jnp.zeros(carry_scr.shape, F32)

    x = x_ref[...] + jnp.dot(o_ref[...], wo_ref[...], preferred_element_type=F32)
    x1_ref[...] = x
    ms = jnp.mean(x * x, axis=-1, keepdims=True)
    h = x * lax.rsqrt(ms + NORM_EPS) * g_ref[...]
    _rows_to_tiles(h_ref, h)
    lane = lax.broadcasted_iota(jnp.int32, (tm, LANES), 1)
    logits = jnp.full((tm, LANES), -jnp.inf, F32)
    for e in range(N_EXPERTS):
        col = jnp.sum(h * r_ref[e:e + 1, :], axis=-1, keepdims=True)
        logits = jnp.where(lane == e, col, logits)
    v0 = jnp.max(logits, axis=-1, keepdims=True)
    i0 = jnp.min(jnp.where(logits == v0, lane, LANES), axis=-1, keepdims=True)
    rest = jnp.where(lane == i0, -jnp.inf, logits)
    v1 = jnp.max(rest, axis=-1, keepdims=True)
    i1 = jnp.min(jnp.where(rest == v1, lane, LANES), axis=-1, keepdims=True)
    tt = jnp.exp(v1 - v0)
    g0 = 1.0 / (1.0 + tt)
    g1 = tt / (1.0 + tt)
    sel0 = lane == i0
    sel1 = lane == i1
    onehot = jnp.where(jnp.logical_or(sel0, sel1), 1.0, 0.0)
    row = lax.broadcasted_iota(jnp.int32, (tm, tm), 0)
    col = lax.broadcasted_iota(jnp.int32, (tm, tm), 1)
    tri = jnp.where(row > col, 1.0, 0.0).astype(BF16)
    before = carry_scr[...] + jnp.dot(tri, onehot.astype(BF16), preferred_element_type=F32)
    rank0 = jnp.sum(jnp.where(sel0, before, 0.0), axis=-1, keepdims=True)
    rank1 = jnp.sum(jnp.where(sel1, before, 0.0), axis=-1, keepdims=True)
    carry_scr[...] = carry_scr[...] + jnp.sum(onehot, axis=0, keepdims=True)
    mi = jnp.where(lane == 0, i0, jnp.where(lane == 1, i1, 0))
    mi = jnp.where(lane == 2, rank0.astype(jnp.int32), jnp.where(lane == 3, rank1.astype(jnp.int32), mi))
    mi_ref[...] = mi
    mf_ref[...] = jnp.where(lane == 0, g0, jnp.where(lane == 1, g1, 0.0))
    cnt_ref[...] = carry_scr[...]


def _router(x2, o2, wo, g, router, tm=512):
    T, D = x2.shape
    K = o2.shape[1]
    rt = router.astype(F32).T
    return pl.pallas_call(
        _router_kernel,
        grid=(T // tm,),
        in_specs=[pl.BlockSpec((tm, D), lambda i: (i, 0)),
                  pl.BlockSpec((tm, K), lambda i: (i, 0)),
                  pl.BlockSpec((K, D), lambda i: (0, 0)),
                  pl.BlockSpec((1, D), lambda i: (0, 0)),
                  pl.BlockSpec((N_EXPERTS, D), lambda i: (0, 0))],
        out_specs=[pl.BlockSpec((tm, D), lambda i: (i, 0)),
                   pl.BlockSpec((tm * ROW_TILE, LANES), lambda i: (i, 0)),
                   pl.BlockSpec((tm, LANES), lambda i: (i, 0)),
                   pl.BlockSpec((tm, LANES), lambda i: (i, 0)),
                   pl.BlockSpec((1, LANES), lambda i: (0, 0))],
        out_shape=[jax.ShapeDtypeStruct((T, D), F32),
                   jax.ShapeDtypeStruct((T * ROW_TILE, LANES), F32),
                   jax.ShapeDtypeStruct((T, LANES), jnp.int32),
                   jax.ShapeDtypeStruct((T, LANES), F32),
                   jax.ShapeDtypeStruct((1, LANES), F32)],
        scratch_shapes=[pltpu.VMEM((1, LANES), F32)],
        compiler_params=_cparams(("arbitrary",)),
        name="moe_router",
    )(x2, o2, wo, g.reshape(1, D), rt)


def _dispatch_kernel(dest_ref, h_ref, buf_in, buf_hbm, sem, *, tm):
    del buf_in

    def copy(n, slot):
        return pltpu.make_async_copy(h_ref.at[pl.ds(pl.multiple_of(n * ROW_TILE, ROW_TILE), ROW_TILE)],
                                     buf_hbm.at[pl.ds(pl.multiple_of(slot * ROW_TILE, ROW_TILE), ROW_TILE)], sem)

    def issue(n, carry):
        copy(n, dest_ref[2 * n]).start(priority=0)
        copy(n, dest_ref[2 * n + 1]).start(priority=1)
        return carry

    lax.fori_loop(0, tm, issue, 0)

    def drain(n, carry):
        copy(0, 0).wait()
        copy(0, 0).wait()
        return carry

    lax.fori_loop(0, tm, drain, 0)


def _dispatch(h2, dest_flat, n_rows, buf0, tm=512):
    T = h2.shape[0] // ROW_TILE
    return pl.pallas_call(
        functools.partial(_dispatch_kernel, tm=tm),
        grid=(T // tm,),
        in_specs=[pl.BlockSpec((2 * tm,), lambda i: (i,), memory_space=pltpu.SMEM),
                  pl.BlockSpec((tm * ROW_TILE, LANES), lambda i: (i, 0)),
                  pl.BlockSpec(memory_space=pl.ANY)],
        out_specs=pl.BlockSpec(memory_space=pl.ANY),
        out_shape=jax.ShapeDtypeStruct((n_rows * ROW_TILE, LANES), F32),
        scratch_shapes=[pltpu.SemaphoreType.DMA(())],
        input_output_aliases={2: 0},
        compiler_params=_cparams(("arbitrary",)),
        name="moe_dispatch",
    )(dest_flat, h2, buf0)


def _moe_ffn_kernel(be_ref, nu_ref, x_ref, w1_ref, w3_ref, w2_ref, y_ref, acc_scr):
    b = pl.program_id(0)
    f = pl.program_id(1)

    @pl.when(b < nu_ref[0])
    def _():
        @pl.when(f == 0)
        def _():
            acc_scr[...] = jnp.zeros(acc_scr.shape, F32)

        h = _tiles_to_rows(x_ref, MOE_BLOCK).astype(BF16)
        u = jnp.dot(h, w1_ref[...], preferred_element_type=F32)
        v = jnp.dot(h, w3_ref[...], preferred_element_type=F32)
        a = (u * jax.nn.sigmoid(u) * v).astype(BF16)
        acc_scr[...] += jnp.dot(a, w2_ref[...], preferred_element_type=F32)

        @pl.when(f == pl.num_programs(1) - 1)
        def _():
            _rows_to_tiles(y_ref, acc_scr[...])

    @pl.when(jnp.logical_and(b >= nu_ref[0], f == 0))
    def _():
        y_ref[...] = jnp.zeros(y_ref.shape, F32)


def _moe_ffn(buf, block_expert, n_used, w1, w3, w2, tf=1792):
    R = buf.shape[0] // ROW_TILE
    E, D, F = w1.shape
    nb = R // MOE_BLOCK

    def last_live(b, nu):
        return jnp.maximum(jnp.minimum(b, nu[0] - 1), 0)

    def row_map(b, f, be, nu):
        return (last_live(b, nu), 0)

    def w_in_map(b, f, be, nu):
        live = b < nu[0]
        return (be[last_live(b, nu)], 0, jnp.where(live, f, F // tf - 1))

    def w_out_map(b, f, be, nu):
        live = b < nu[0]
        return (be[last_live(b, nu)], jnp.where(live, f, F // tf - 1), 0)

    return pl.pallas_call(
        _moe_ffn_kernel,
        grid_spec=pltpu.PrefetchScalarGridSpec(
            num_scalar_prefetch=2,
            grid=(nb, F // tf),
            in_specs=[pl.BlockSpec((MOE_BLOCK * ROW_TILE, LANES), row_map),
                      pl.BlockSpec((None, D, tf), w_in_map),
                      pl.BlockSpec((None, D, tf), w_in_map),
                      pl.BlockSpec((None, tf, D), w_out_map)],
            out_specs=pl.BlockSpec((MOE_BLOCK * ROW_TILE, LANES), lambda b, f, be, nu: (b, 0)),
            scratch_shapes=[pltpu.VMEM((MOE_BLOCK, D), F32)],
        ),
        out_shape=jax.ShapeDtypeStruct((R * ROW_TILE, LANES), F32),
        compiler_params=_cparams(("arbitrary", "arbitrary")),
        name="moe_ffn",
    )(block_expert, n_used, buf, w1, w3, w2)


def _combine_kernel(dest_ref, x_ref, mf_ref, g_ref, ybuf_hbm, y_ref, rows0_scr, rows1_scr, sem, *, tm, final_norm):
    def copy(slot, k, n):
        dst = rows0_scr if k == 0 else rows1_scr
        return pltpu.make_async_copy(ybuf_hbm.at[pl.ds(pl.multiple_of(slot * ROW_TILE, ROW_TILE), ROW_TILE)],
                                     dst.at[pl.ds(pl.multiple_of(n * ROW_TILE, ROW_TILE), ROW_TILE)], sem)

    def issue(n, carry):
        copy(dest_ref[2 * n], 0, n).start(priority=0)
        copy(dest_ref[2 * n + 1], 1, n).start(priority=1)
        return carry

    lax.fori_loop(0, tm, issue, 0)

    def drain(n, carry):
        copy(0, 0, 0).wait()
        copy(0, 1, 0).wait()
        return carry

    lax.fori_loop(0, tm, drain, 0)

    mf = mf_ref[...]
    y = x_ref[...] + mf[:, 0:1] * _tiles_to_rows(rows0_scr, tm) + mf[:, 1:2] * _tiles_to_rows(rows1_scr, tm)
    if final_norm:
        ms = jnp.mean(y * y, axis=-1, keepdims=True)
        y = y * lax.rsqrt(ms + NORM_EPS) * g_ref[...]
    y_ref[...] = y


def _combine(x2, mf, dest_flat, ybuf, g_final, final_norm, tm=512):
    T, D = x2.shape
    return pl.pallas_call(
        functools.partial(_combine_kernel, tm=tm, final_norm=final_norm),
        grid=(T // tm,),
        in_specs=[pl.BlockSpec((2 * tm,), lambda i: (i,), memory_space=pltpu.SMEM),
                  pl.BlockSpec((tm, D), lambda i: (i, 0)),
                  pl.BlockSpec((tm, LANES), lambda i: (i, 0)),
                  pl.BlockSpec((1, D), lambda i: (0, 0)),
                  pl.BlockSpec(memory_space=pl.ANY)],
        out_specs=pl.BlockSpec((tm, D), lambda i: (i, 0)),
        out_shape=jax.ShapeDtypeStruct((T, D), F32),
        scratch_shapes=[pltpu.VMEM((tm * ROW_TILE, LANES), F32), pltpu.VMEM((tm * ROW_TILE, LANES), F32),
                        pltpu.SemaphoreType.DMA(())],
        compiler_params=_cparams(("arbitrary",)),
        name="moe_combine",
    )(dest_flat, x2, mf, g_final.reshape(1, D), ybuf)


def _moe_layer(x2, o2, wo, g, router, w1, w3, w2, g_final, final_norm, buf0=None):
    T, D = x2.shape
    x2, h2, mi, mf, cnt = _router(x2, o2, wo, g, router)
    counts = cnt[0, :N_EXPERTS].astype(jnp.int32)
    padded = ((counts + MOE_BLOCK - 1) // MOE_BLOCK) * MOE_BLOCK
    pend = jnp.cumsum(padded)
    pstart = pend - padded
    dest = pstart[mi[:, 0:2]] + mi[:, 2:4]
    dest_flat = dest.reshape(-1).astype(jnp.int32)
    nb = -(-(2 * T + N_EXPERTS * (MOE_BLOCK - 1)) // MOE_BLOCK)
    block_start = jnp.arange(nb, dtype=jnp.int32) * MOE_BLOCK
    block_expert = jnp.minimum(jnp.sum(pend[None, :] <= block_start[:, None], axis=1),
                               N_EXPERTS - 1).astype(jnp.int32)
    n_used = (pend[-1:] // MOE_BLOCK).astype(jnp.int32)
    if buf0 is None:
        buf0 = jnp.zeros((nb * MOE_BLOCK * ROW_TILE, LANES), F32)
    buf = _dispatch(h2, dest_flat, nb * MOE_BLOCK, buf0)
    ybuf = _moe_ffn(buf, block_expert, n_used, w1, w3, w2)
    return _combine(x2, mf, dest_flat, ybuf, g_final, final_norm), ybuf


def _mixer_a(x, g, w_in, w_out, tabs):
    B, S, D = x.shape
    w = w_in.astype(BF16)
    qkvs = []
    for gi, dil in enumerate(DILATIONS):
        cos, sin = tabs[dil]
        qkvs.append(_norm_proj(x, g, w, cos, sin, dil=dil, col_block=gi, ncols=3 * D, tc=D,
                               rope_chunks=_QKV_ROPE_CHUNKS))
    o = _attn_a(qkvs, B, S)
    return o.reshape(B * S, D), w_out.astype(BF16)


def _mixer_b(x, g, w_in, lq1, lk1, lq2, lk2, subln, w_out, lambda_init, tabs, side_casts=()):
    B, S, D = x.shape
    cos, sin = tabs[1]
    qkv = _norm_proj(x, g, w_in.astype(BF16), cos, sin, dil=1, col_block=0, ncols=3 * D, tc=D,
                     rope_chunks=_QKV_ROPE_CHUNKS).reshape(B, S, 3 * D)
    lam_params = jnp.stack([lq1, lk1, lq2, lk2]).astype(F32)
    o, casted = _attn_b(qkv, lam_params, subln.astype(F32), lambda_init, B, S, side_casts)
    return o.reshape(B * S, D), w_out.astype(BF16), casted


def _mixer_c(x, g, w_in, sink, w_out, tabs):
    B, S, D = x.shape
    cos, sin = tabs[1]
    head_order = jnp.arange(16).reshape(2, 8).T.reshape(-1)
    col_order = (head_order[:, None] * HEAD_DIM + jnp.arange(HEAD_DIM)[None, :]).reshape(-1)
    w_q = w_in[:, :D][:, col_order]
    w = jnp.concatenate([w_q, w_in[:, D:]], axis=1).astype(BF16)
    ncols = w.shape[1]
    chunks = tuple(c < (ncols // LANES - 1) for c in range(ncols // LANES))
    qkv = _norm_proj(x, g, w, cos, sin, dil=1, col_block=0, ncols=ncols, tc=ncols,
                     rope_chunks=chunks).reshape(B, S, ncols)
    sk = sink.astype(F32)
    sink_tab = jnp.concatenate([jnp.broadcast_to(sk[:8, None], (8, LANES)),
                                jnp.broadcast_to(sk[8:, None], (8, LANES))], axis=1)
    o = _attn_c(qkv, sink_tab, B, S)
    w_o = w_out[col_order, :].astype(BF16)
    return o.reshape(B * S, D), w_o


def kernel(x, positions, l0_norm_mix, l0_a_w_in, l0_a_w_out, l0_norm_ffn, l0_ffn_w1, l0_ffn_w3, l0_ffn_w2, l1_norm_mix, l1_b_w_in, l1_b_lambda_q1, l1_b_lambda_k1, l1_b_lambda_q2, l1_b_lambda_k2, l1_b_subln, l1_b_w_out, l1_norm_ffn, l1_moe_router, l1_moe_w1, l1_moe_w3, l1_moe_w2, l2_norm_mix, l2_c_w_in, l2_c_sink, l2_c_w_out, l2_norm_ffn, l2_ffn_w1, l2_ffn_w3, l2_ffn_w2, l3_norm_mix, l3_a_w_in, l3_a_w_out, l3_norm_ffn, l3_moe_router, l3_moe_w1, l3_moe_w3, l3_moe_w2, final_norm):
    B, S, D = x.shape
    T = B * S
    cos, sin = lax.optimization_barrier(_rope_tables(positions))
    tabs = {d: (_to_strided(cos, d), _to_strided(sin, d)) for d in DILATIONS}

    o, wo = _mixer_a(x, l0_norm_mix, l0_a_w_in, l0_a_w_out, tabs)
    x = _ffn_dense(x.reshape(T, D), o, wo, l0_norm_ffn, l0_ffn_w1.astype(BF16), l0_ffn_w3.astype(BF16),
                   l0_ffn_w2.astype(BF16)).reshape(B, S, D)
    lambda_init = 0.8 - 0.6 * math.exp(-0.3 * 1)
    moe_w = (l1_moe_w1, l1_moe_w3, l1_moe_w2, l3_moe_w1, l3_moe_w3, l3_moe_w2)
    o, wo, moe_bf = _mixer_b(x, l1_norm_mix, l1_b_w_in, l1_b_lambda_q1, l1_b_lambda_k1, l1_b_lambda_q2,
                             l1_b_lambda_k2, l1_b_subln, l1_b_w_out, lambda_init, tabs,
                             side_casts=tuple(w.reshape(-1, w.shape[-1]) for w in moe_w))
    moe_bf = [wb.reshape(w.shape) for wb, w in zip(moe_bf, moe_w)]
    x, spare = _moe_layer(x.reshape(T, D), o, wo, l1_norm_ffn, l1_moe_router, moe_bf[0], moe_bf[1], moe_bf[2],
                          final_norm, False)
    x = x.reshape(B, S, D)
    o, wo = _mixer_c(x, l2_norm_mix, l2_c_w_in, l2_c_sink, l2_c_w_out, tabs)
    x = _ffn_dense(x.reshape(T, D), o, wo, l2_norm_ffn, l2_ffn_w1.astype(BF16), l2_ffn_w3.astype(BF16),
                   l2_ffn_w2.astype(BF16)).reshape(B, S, D)
    o, wo = _mixer_a(x, l3_norm_mix, l3_a_w_in, l3_a_w_out, tabs)
    x, _ = _moe_layer(x.reshape(T, D), o, wo, l3_norm_ffn, l3_moe_router, moe_bf[3], moe_bf[4], moe_bf[5],
                      final_norm, True, buf0=spare)
    return x.reshape(B, S, D)
```

```python
import functools
import math

import jax
import jax.numpy as jnp
from jax import lax
from jax.experimental import pallas as pl
from jax.experimental.pallas import tpu as pltpu

F32 = jnp.float32
BF16 = jnp.bfloat16

D_MODEL = 1024
HEAD_DIM = 64
ROT_DIM = HEAD_DIM // 4
ROPE_THETA = 500000.0
NORM_EPS = 1e-6
SUBLN_EPS = 1e-5
DILATIONS = (1, 4, 16)
A_HALF = 64
C_HALF = 128
N_EXPERTS = 8
_QKV_ROPE_CHUNKS = (True,) * 16 + (False,) * 8

LANES = 128
F32_SUBLANES = 8
BF16_SUBLANES = 16
VMEM_LIMIT = 56 * 1024 * 1024

PROJ_TM = 512
Q_SUB = LANES
ATTN_A_TB = 2048
ATTN_C_TB = 1024
ATTN_B_TQ = 1024
ATTN_B_TK = 2048
CK = 512
A_GROUP = 3
C_AHEAD = 4
FFN_TM = 512
MOE_TM = 512
MOE_BLOCK = 512
MOE_TF = 1792


def _cparams(sem):
    return pltpu.CompilerParams(dimension_semantics=sem, vmem_limit_bytes=VMEM_LIMIT)


def _rope_tables(positions):
    lane = jnp.arange(LANES) % HEAD_DIM
    half = ROT_DIM // 2
    inv_freq = ROPE_THETA ** (-((2 * (lane % half)).astype(F32) / ROT_DIM))
    freq = jnp.where(lane < ROT_DIM, inv_freq, 0.0)
    sign = jnp.where(lane < half, -1.0, jnp.where(lane < ROT_DIM, 1.0, 0.0))
    ang = positions.astype(F32)[..., None] * freq
    return jnp.cos(ang), jnp.sin(ang) * sign


def _to_strided(t, d):
    B, S, C = t.shape
    return jnp.swapaxes(t.reshape(B, S // d, d, C), 1, 2)


def _rope_chunk(x, cos, sin, first_half):
    partner = jnp.where(first_half, pltpu.roll(x, LANES - ROT_DIM // 2, 1),
                        pltpu.roll(x, ROT_DIM // 2, 1))
    return x * cos + partner * sin


def _proj_kernel(*refs, dil, rope_chunks, tc):
    if dil > 1:
        x_ref, g_ref, w_ref, cos_ref, sin_ref, perm_ref, o_ref = refs
    else:
        x_ref, g_ref, w_ref, cos_ref, sin_ref, o_ref = refs
    tm = x_ref.shape[0]
    ncols = w_ref.shape[1]
    n = tm // dil
    x = x_ref[...]
    ms = jnp.mean(x * x, axis=-1, keepdims=True)
    h = (x * lax.rsqrt(ms + NORM_EPS) * g_ref[...]).astype(BF16)
    if dil > 1:
        h = jnp.dot(perm_ref[...], h, preferred_element_type=F32).astype(BF16)
    lane = lax.broadcasted_iota(jnp.int32, (1, LANES), 1)
    first_half = (lane % HEAD_DIM) < (ROT_DIM // 2)
    cos = cos_ref[...].reshape(tm, LANES)
    sin = sin_ref[...].reshape(tm, LANES)
    for c0 in range(0, ncols, tc):
        res = jnp.dot(h, w_ref[:, c0:c0 + tc], preferred_element_type=F32)
        parts = []
        for cc in range(tc // LANES):
            xc = res[:, cc * LANES:(cc + 1) * LANES]
            if rope_chunks[c0 // LANES + cc]:
                xc = _rope_chunk(xc, cos, sin, first_half)
            parts.append(xc.astype(o_ref.dtype))
        val = jnp.concatenate(parts, axis=1)
        for r in range(dil):
            o_ref[r, :, c0:c0 + tc] = val[r * n:(r + 1) * n]


def _norm_proj(x, g, w, cos, sin, *, dil, col_block, ncols, tc, rope_chunks, tm=PROJ_TM):
    B, S, D = x.shape
    n = tm // dil
    L = S // dil
    in_specs = [
        pl.BlockSpec((None, tm, D), lambda b, i: (b, i, 0)),
        pl.BlockSpec((1, D), lambda b, i: (0, 0)),
        pl.BlockSpec((D, ncols), lambda b, i: (0, col_block)),
        pl.BlockSpec((None, dil, n, LANES), lambda b, i: (b, 0, i, 0)),
        pl.BlockSpec((None, dil, n, LANES), lambda b, i: (b, 0, i, 0)),
    ]
    args = [x, g.reshape(1, D), w, cos, sin]
    if dil > 1:
        p = jnp.arange(tm)
        src = (p % n) * dil + p // n
        perm = (src[:, None] == jnp.arange(tm)[None, :]).astype(BF16)
        in_specs.append(pl.BlockSpec((tm, tm), lambda b, i: (0, 0)))
        args.append(perm)
    return pl.pallas_call(
        functools.partial(_proj_kernel, dil=dil, rope_chunks=rope_chunks, tc=tc),
        grid=(B, S // tm),
        in_specs=in_specs,
        out_specs=pl.BlockSpec((None, dil, n, ncols), lambda b, i: (b, 0, i, 0)),
        out_shape=jax.ShapeDtypeStruct((B, dil, L, ncols), BF16),
        compiler_params=_cparams(("parallel", "parallel")),
        name=f"norm_proj_d{dil}",
    )(*args)


def _fill_band_bias(bias_scr, tq, W, half):
    kk = lax.broadcasted_iota(jnp.int32, (W, 2 * tq), 0)
    qq = lax.broadcasted_iota(jnp.int32, (W, 2 * tq), 1) % tq
    for i in range(3):
        ok = jnp.abs(qq + i * half - kk) <= half
        bias_scr[i] = jnp.where(ok, 0.0, -jnp.inf).astype(F32)


def _band_scores(q2, kw, bias):
    lane_lo = lax.broadcasted_iota(jnp.int32, (1, LANES), 1) < HEAD_DIM
    zero = jnp.zeros_like(q2)
    qq = jnp.concatenate([jnp.where(lane_lo, q2, zero), jnp.where(lane_lo, zero, q2)], axis=0)
    qq = qq * jnp.asarray(HEAD_DIM ** -0.5, q2.dtype)
    return lax.dot_general(kw, qq, (((1,), (1,)), ((), ())), preferred_element_type=F32) + bias


def _band_finish(st, vw, want_lse, sink_row=None):
    tq = st.shape[1] // 2
    row_lo = lax.broadcasted_iota(jnp.int32, (LANES, 1), 0) < HEAD_DIM
    m = jnp.max(st, axis=0, keepdims=True)
    if sink_row is not None:
        m = jnp.maximum(m, sink_row)
    p = jnp.exp(st - m)
    l = jnp.sum(p, axis=0, keepdims=True)
    if sink_row is not None:
        l = l + jnp.exp(sink_row - m)
    ot = lax.dot_general(vw, p.astype(BF16), (((0,), (0,)), ((), ())), preferred_element_type=F32) / l
    o = jnp.where(row_lo, ot[:, :tq], ot[:, tq:]).T
    if not want_lse:
        return o, None
    lse_row = m + jnp.log(l)
    lse = jnp.where(row_lo, jnp.broadcast_to(lse_row[:, :tq], (LANES, tq)),
                    jnp.broadcast_to(lse_row[:, tq:], (LANES, tq))).T
    return o, lse


def _attn_a_kernel(*refs, TB, S):
    qkv = refs[:9]
    o_ref = refs[9]
    scr = refs[10:16]
    bias_scr = refs[16]
    t = pl.program_id(2)
    W = 4 * A_HALF

    @pl.when(t == 0)
    def _():
        _fill_band_bias(bias_scr, Q_SUB, W, A_HALF)

    subs = []
    for g, dil in enumerate(DILATIONS):
        nq = TB // dil
        tq = min(nq, Q_SUB)
        for r in range(dil):
            for jj in range(nq // tq):
                subs.append((g, dil, nq, tq, r, jj))

    def scores(sub):
        g, dil, nq, tq, r, jj = sub
        q_ref, k_ref = qkv[3 * g], qkv[3 * g + 1]
        L = S // dil
        qs = t * nq + jj * tq
        ws = pl.multiple_of(jnp.clip(qs - A_HALF, 0, L - W), A_HALF)
        q2 = q_ref[r, jj * tq:(jj + 1) * tq, :]
        return _band_scores(q2, k_ref[r, pl.ds(ws, W), :], bias_scr[(qs - ws) // A_HALF]), ws

    def finish(sub, st, ws):
        g, dil, nq, tq, r, jj = sub
        o_scr, l_scr = scr[2 * g], scr[2 * g + 1]
        o, lse = _band_finish(st, qkv[3 * g + 2][r, pl.ds(ws, W), :], True)
        row0 = jj * (tq * dil) + r
        if dil == 1:
            o_scr[row0:row0 + tq, :] = o
            l_scr[row0:row0 + tq, :] = lse
        else:
            o_scr[pl.ds(row0, tq, stride=dil), :] = o
            l_scr[pl.ds(row0, tq, stride=dil), :] = lse

    pend = [scores(sub) for sub in subs[:A_GROUP]]
    for n, sub in enumerate(subs):
        if n + A_GROUP < len(subs):
            pend.append(scores(subs[n + A_GROUP]))
        st, ws = pend[n]
        finish(sub, st, ws)

    l0, l1, l2 = scr[1][...], scr[3][...], scr[5][...]
    mx = jnp.maximum(jnp.maximum(l0, l1), l2)
    e0, e1, e2 = jnp.exp(l0 - mx), jnp.exp(l1 - mx), jnp.exp(l2 - mx)
    den = e0 + e1 + e2
    o = (e0 * scr[0][...] + e1 * scr[2][...] + e2 * scr[4][...]) / den
    o_ref[...] = o.astype(o_ref.dtype)


def _attn_a(qkvs, B, S, TB=ATTN_A_TB):
    n_pairs = (16 * HEAD_DIM) // LANES
    in_specs, args = [], []
    for g, dil in enumerate(DILATIONS):
        L = S // dil
        in_specs += [
            pl.BlockSpec((None, dil, TB // dil, LANES), lambda b, hp, t: (b, 0, t, hp)),
            pl.BlockSpec((None, dil, L, LANES), lambda b, hp, t: (b, 0, 0, n_pairs + hp)),
            pl.BlockSpec((None, dil, L, LANES), lambda b, hp, t: (b, 0, 0, 2 * n_pairs + hp)),
        ]
        args += [qkvs[g]] * 3
    return pl.pallas_call(
        functools.partial(_attn_a_kernel, TB=TB, S=S),
        grid=(B, n_pairs, S // TB),
        in_specs=in_specs,
        out_specs=pl.BlockSpec((None, TB, LANES), lambda b, hp, t: (b, t, hp)),
        out_shape=jax.ShapeDtypeStruct((B, S, n_pairs * LANES), BF16),
        scratch_shapes=[pltpu.VMEM((TB, LANES), F32)] * 6
        + [pltpu.VMEM((3, Q_SUB + 2 * A_HALF, 2 * Q_SUB), F32)],
        compiler_params=_cparams(("parallel", "parallel", "arbitrary")),
        name="attn_dilated",
    )(*args)


def _attn_c_kernel(q_ref, k_ref, v_ref, sink_ref, o_ref, bias_scr, *, TB, S):
    t = pl.program_id(1)
    tq = Q_SUB
    W = tq + 2 * C_HALF
    n_pairs = q_ref.shape[1] // LANES

    @pl.when(t == 0)
    def _():
        _fill_band_bias(bias_scr, tq, W, C_HALF)

    units = [(jj, p) for jj in range(TB // tq) for p in range(n_pairs)]

    def window(jj):
        qs = t * TB + jj * tq
        ws = pl.multiple_of(jnp.clip(qs - C_HALF, 0, S - W), C_HALF)
        return qs, ws

    def scores(unit):
        jj, p = unit
        qs, ws = window(jj)
        q2 = q_ref[jj * tq:(jj + 1) * tq, p * LANES:(p + 1) * LANES]
        return _band_scores(q2, k_ref[pl.ds(ws, W), :], bias_scr[(qs - ws) // C_HALF])

    pend = [scores(u) for u in units[:C_AHEAD]]
    for n, (jj, p) in enumerate(units):
        if n + C_AHEAD < len(units):
            pend.append(scores(units[n + C_AHEAD]))
        _, ws = window(jj)
        sink_row = sink_ref[p:p + 1, :]
        o, _ = _band_finish(pend[n], v_ref[pl.ds(ws, W), :], False, sink_row=sink_row)
        pend[n] = None
        o_ref[jj * tq:(jj + 1) * tq, p * LANES:(p + 1) * LANES] = o.astype(o_ref.dtype)


def _attn_c(qkv, sink_tab, B, S, TB=ATTN_C_TB):
    nq = 16 * HEAD_DIM
    return pl.pallas_call(
        functools.partial(_attn_c_kernel, TB=TB, S=S),
        grid=(B, S // TB),
        in_specs=[
            pl.BlockSpec((None, TB, nq), lambda b, t: (b, t, 0)),
            pl.BlockSpec((None, S, LANES), lambda b, t: (b, 0, nq // LANES)),
            pl.BlockSpec((None, S, LANES), lambda b, t: (b, 0, nq // LANES + 1)),
            pl.BlockSpec((8, 2 * LANES), lambda b, t: (0, 0)),
        ],
        out_specs=pl.BlockSpec((None, TB, nq), lambda b, t: (b, t, 0)),
        out_shape=jax.ShapeDtypeStruct((B, S, nq), BF16),
        scratch_shapes=[pltpu.VMEM((3, Q_SUB + 2 * C_HALF, 2 * Q_SUB), F32)],
        compiler_params=_cparams(("parallel", "arbitrary")),
        name="attn_swa_sink",
    )(qkv, qkv, qkv, sink_tab)


def _attn_b_kernel(*refs, lambda_init, n_cast):
    lam_ref, q_ref, qn_ref, k_ref, kn_ref, v_ref, subln_ref = refs[:7]
    cast_in = refs[7:7 + n_cast]
    o_ref = refs[7 + n_cast]
    cast_out = refs[8 + n_cast:8 + 2 * n_cast]
    s0_scr, s1_scr, m_scr, l_scr, acc_scr = refs[8 + 2 * n_cast:]
    _attn_b_body(lam_ref, q_ref, qn_ref, k_ref, kn_ref, v_ref, subln_ref, o_ref,
                 s0_scr, s1_scr, m_scr, l_scr, acc_scr, lambda_init=lambda_init)
    for src_ref, dst_ref in zip(cast_in, cast_out):
        dst_ref[...] = src_ref[...].astype(dst_ref.dtype)


def _attn_b_body(lam_ref, q_ref, qn_ref, k_ref, kn_ref, v_ref, subln_ref, o_ref,
                 s0_scr, s1_scr, m_scr, l_scr, acc_scr, *, lambda_init):
    i = pl.program_id(2)
    kv = pl.program_id(3)
    nkv = pl.num_programs(3)
    tk = k_ref.shape[0]
    lane = lax.broadcasted_iota(jnp.int32, (1, LANES), 1)
    lane_lo = lane < HEAD_DIM

    def scores(kref, qref, c, j):
        k = kref[j * CK:(j + 1) * CK, :] * jnp.asarray(HEAD_DIM ** -0.5, kref.dtype)
        sel = lane_lo if c == 0 else jnp.logical_not(lane_lo)
        kc = jnp.where(sel, k, jnp.zeros_like(k))
        return lax.dot_general(kc, qref[...], (((1,), (1,)), ((), ())), preferred_element_type=F32)

    @pl.when(kv == 0)
    def _():
        m_scr[...] = jnp.full(m_scr.shape, -jnp.inf, F32)
        l_scr[...] = jnp.zeros(l_scr.shape, F32)
        acc_scr[...] = jnp.zeros(acc_scr.shape, F32)

    @pl.when(jnp.logical_and(i == 0, kv == 0))
    def _():
        for c in range(2):
            for j in range(tk // CK):
                s0_scr[c * tk + j * CK:c * tk + (j + 1) * CK, :] = scores(k_ref, q_ref, c, j)

    def step(cur_scr, nxt_scr):
        vt = jnp.concatenate([v_ref[...].T, jnp.ones((BF16_SUBLANES, tk), BF16)], axis=0)
        for c in range(2):
            m, l, acc = m_scr[c], l_scr[c], acc_scr[c]
            for j in range(tk // CK):
                rows = slice(c * tk + j * CK, c * tk + (j + 1) * CK)
                nxt_scr[rows, :] = scores(kn_ref, qn_ref, c, j)
                s = cur_scr[rows, :]
                m_new = jnp.maximum(m, jnp.max(s, axis=0, keepdims=True))
                alpha = jnp.exp(m - m_new)
                p = jnp.exp(s - m_new).astype(BF16)
                pv = jnp.dot(vt[:, j * CK:(j + 1) * CK], p, preferred_element_type=F32)
                l = alpha * l + pv[LANES:LANES + 1]
                acc = alpha * acc + pv[:LANES]
                m = m_new
            m_scr[c], l_scr[c], acc_scr[c] = m, l, acc

    @pl.when(kv % 2 == 0)
    def _():
        step(s0_scr, s1_scr)

    @pl.when(kv % 2 == 1)
    def _():
        step(s1_scr, s0_scr)

    @pl.when(kv == nkv - 1)
    def _():
        lp = lam_ref[...]
        lam = (jnp.exp(jnp.sum(lp[0:1] * lp[1:2], axis=-1, keepdims=True))
               - jnp.exp(jnp.sum(lp[2:3] * lp[3:4], axis=-1, keepdims=True)) + lambda_init)
        o = acc_scr[0] / l_scr[0] - lam * (acc_scr[1] / l_scr[1])
        ms = jnp.mean(o * o, axis=0, keepdims=True)
        o = o * lax.rsqrt(ms + SUBLN_EPS) * subln_ref[...] * (1.0 - lambda_init)
        o_ref[...] = o.T.astype(o_ref.dtype)


def _attn_b(qkv, lam_params, subln, lambda_init, B, S, side_casts=(), tq=ATTN_B_TQ, tk=ATTN_B_TK):
    H = 8
    nq, nkv = S // tq, S // tk
    assert nkv % 2 == 0
    nsteps = B * H * nq * nkv
    cast_specs, cast_shapes = [], []
    for w in side_casts:
        rows, cols = w.shape
        nblk = nsteps
        while rows % nblk or (rows // nblk) % BF16_SUBLANES:
            nblk //= 2
        per = nsteps // nblk

        def w_map(b, h, i, kv, per=per):
            return ((((b * H + h) * nq + i) * nkv + kv) // per, 0)

        cast_specs.append(pl.BlockSpec((rows // nblk, cols), w_map))
        cast_shapes.append(jax.ShapeDtypeStruct((rows, cols), BF16))

    def q_next(b, h, i, kv):
        return (b, jnp.minimum(i + (kv + 1) // nkv, nq - 1), h)

    def k_next(b, h, i, kv):
        return (b, (kv + 1) % nkv, H + h)

    outs = pl.pallas_call(
        functools.partial(_attn_b_kernel, lambda_init=lambda_init, n_cast=len(side_casts)),
        grid=(B, H, nq, nkv),
        in_specs=[
            pl.BlockSpec((4, HEAD_DIM), lambda b, h, i, kv: (0, 0)),
            pl.BlockSpec((None, tq, LANES), lambda b, h, i, kv: (b, i, h)),
            pl.BlockSpec((None, tq, LANES), q_next),
            pl.BlockSpec((None, tk, LANES), lambda b, h, i, kv: (b, kv, H + h)),
            pl.BlockSpec((None, tk, LANES), k_next),
            pl.BlockSpec((None, tk, LANES), lambda b, h, i, kv: (b, kv, 2 * H + h)),
            pl.BlockSpec((LANES, 1), lambda b, h, i, kv: (0, 0)),
        ] + cast_specs,
        out_specs=[pl.BlockSpec((None, tq, LANES), lambda b, h, i, kv: (b, i, h))] + cast_specs,
        out_shape=[jax.ShapeDtypeStruct((B, S, H * LANES), BF16)] + cast_shapes,
        scratch_shapes=[pltpu.VMEM((2 * tk, tq), F32), pltpu.VMEM((2 * tk, tq), F32),
                        pltpu.VMEM((2, 1, tq), F32), pltpu.VMEM((2, 1, tq), F32),
                        pltpu.VMEM((2, LANES, tq), F32)],
        compiler_params=_cparams(("parallel", "parallel", "arbitrary", "arbitrary")),
        name="attn_diff",
    )(lam_params, qkv, qkv, qkv, qkv, qkv, subln.reshape(LANES, 1), *side_casts)
    return outs[0], outs[1:]


def _ffn_kernel(x_ref, o_ref, wo_ref, g_ref, w1_ref, w3_ref, w2_ref, y_ref):
    x = x_ref[...] + jnp.dot(o_ref[...], wo_ref[...], preferred_element_type=F32)
    ms = jnp.mean(x * x, axis=-1, keepdims=True)
    h = (x * lax.rsqrt(ms + NORM_EPS) * g_ref[...]).astype(BF16)
    u = jnp.dot(h, w1_ref[...], preferred_element_type=F32)
    v = jnp.dot(h, w3_ref[...], preferred_element_type=F32)
    a = (u * jax.nn.sigmoid(u) * v).astype(BF16)
    y_ref[...] = x + jnp.dot(a, w2_ref[...], preferred_element_type=F32)


def _ffn_dense(x2, o2, wo, g, w1, w3, w2, tm=FFN_TM):
    T, D = x2.shape
    K = o2.shape[1]
    F = w1.shape[1]
    return pl.pallas_call(
        _ffn_kernel,
        grid=(T // tm,),
        in_specs=[pl.BlockSpec((tm, D), lambda i: (i, 0)),
                  pl.BlockSpec((tm, K), lambda i: (i, 0)),
                  pl.BlockSpec((K, D), lambda i: (0, 0)),
                  pl.BlockSpec((1, D), lambda i: (0, 0)),
                  pl.BlockSpec((D, F), lambda i: (0, 0)),
                  pl.BlockSpec((D, F), lambda i: (0, 0)),
                  pl.BlockSpec((F, D), lambda i: (0, 0))],
        out_specs=pl.BlockSpec((tm, D), lambda i: (i, 0)),
        out_shape=jax.ShapeDtypeStruct((T, D), F32),
        compiler_params=_cparams(("parallel",)),
        name="ffn_dense",
    )(x2, o2, wo, g.reshape(1, D), w1, w3, w2)


ROW_TILE = F32_SUBLANES


def _rows_to_tiles(ref, val):
    n = val.shape[0]
    for a in range(ROW_TILE):
        ref[pl.ds(a, n, stride=ROW_TILE), :] = val[:, a * LANES:(a + 1) * LANES]


def _tiles_to_rows(ref, n):
    return jnp.concatenate([ref[pl.ds(a, n, stride=ROW_TILE), :] for a in range(ROW_TILE)], axis=1)


def _router_kernel(x_ref, o_ref, wo_ref, g_ref, r_ref, x1_ref, h_ref, mi_ref, mf_ref, cnt_ref, carry_scr):
    i = pl.program_id(0)
    tm = x_ref.shape[0]

    @pl.when(i == 0)
    def _():
        carry_scr[...] = jnp.zeros(carry_scr.shape, F32)

    x = x_ref[...] + jnp.dot(o_ref[...], wo_ref[...], preferred_element_type=F32)
    x1_ref[...] = x
    ms = jnp.mean(x * x, axis=-1, keepdims=True)
    h = x * lax.rsqrt(ms + NORM_EPS) * g_ref[...]
    _rows_to_tiles(h_ref, h)
    lane = lax.broadcasted_iota(jnp.int32, (tm, LANES), 1)
    logits = jnp.full((tm, LANES), -jnp.inf, F32)
    for e in range(N_EXPERTS):
        col = jnp.sum(h * r_ref[e:e + 1, :], axis=-1, keepdims=True)
        logits = jnp.where(lane == e, col, logits)
    v0 = jnp.max(logits, axis=-1, keepdims=True)
    i0 = jnp.min(jnp.where(logits == v0, lane, LANES), axis=-1, keepdims=True)
    rest = jnp.where(lane == i0, -jnp.inf, logits)
    v1 = jnp.max(rest, axis=-1, keepdims=True)
    i1 = jnp.min(jnp.where(rest == v1, lane, LANES), axis=-1, keepdims=True)
    tt = jnp.exp(v1 - v0)
    g0 = 1.0 / (1.0 + tt)
    g1 = tt / (1.0 + tt)
    sel0 = lane == i0
    sel1 = lane == i1
    onehot = jnp.where(jnp.logical_or(sel0, sel1), 1.0, 0.0)
    row = lax.broadcasted_iota(jnp.int32, (tm, tm), 0)
    col = lax.broadcasted_iota(jnp.int32, (tm, tm), 1)
    tri = jnp.where(row > col, 1.0, 0.0).astype(BF16)
    before = carry_scr[...] + jnp.dot(tri, onehot.astype(BF16), preferred_element_type=F32)
    rank0 = jnp.sum(jnp.where(sel0, before, 0.0), axis=-1, keepdims=True)
    rank1 = jnp.sum(jnp.where(sel1, before, 0.0), axis=-1, keepdims=True)
    carry_scr[...] = carry_scr[...] + jnp.sum(onehot, axis=0, keepdims=True)
    mi = jnp.where(lane == 0, i0, jnp.where(lane == 1, i1, 0))
    mi = jnp.where(lane == 2, rank0.astype(jnp.int32), jnp.where(lane == 3, rank1.astype(jnp.int32), mi))
    mi_ref[...] = mi
    mf_ref[...] = jnp.where(lane == 0, g0, jnp.where(lane == 1, g1, 0.0))
    cnt_ref[...] = carry_scr[...]


def _router(x2, o2, wo, g, router, tm=MOE_TM):
    T, D = x2.shape
    K = o2.shape[1]
    rt = router.astype(F32).T
    return pl.pallas_call(
        _router_kernel,
        grid=(T // tm,),
        in_specs=[pl.BlockSpec((tm, D), lambda i: (i, 0)),
                  pl.BlockSpec((tm, K), lambda i: (i, 0)),
                  pl.BlockSpec((K, D), lambda i: (0, 0)),
                  pl.BlockSpec((1, D), lambda i: (0, 0)),
                  pl.BlockSpec((N_EXPERTS, D), lambda i: (0, 0))],
        out_specs=[pl.BlockSpec((tm, D), lambda i: (i, 0)),
                   pl.BlockSpec((tm * ROW_TILE, LANES), lambda i: (i, 0)),
                   pl.BlockSpec((tm, LANES), lambda i: (i, 0)),
                   pl.BlockSpec((tm, LANES), lambda i: (i, 0)),
                   pl.BlockSpec((1, LANES), lambda i: (0, 0))],
        out_shape=[jax.ShapeDtypeStruct((T, D), F32),
                   jax.ShapeDtypeStruct((T * ROW_TILE, LANES), F32),
                   jax.ShapeDtypeStruct((T, LANES), jnp.int32),
                   jax.ShapeDtypeStruct((T, LANES), F32),
                   jax.ShapeDtypeStruct((1, LANES), F32)],
        scratch_shapes=[pltpu.VMEM((1, LANES), F32)],
        compiler_params=_cparams(("arbitrary",)),
        name="moe_router",
    )(x2, o2, wo, g.reshape(1, D), rt)


def _dispatch_kernel(dest_ref, h_ref, buf_in, buf_hbm, sem, *, tm):
    del buf_in

    def copy(n, slot):
        return pltpu.make_async_copy(h_ref.at[pl.ds(pl.multiple_of(n * ROW_TILE, ROW_TILE), ROW_TILE)],
                                     buf_hbm.at[pl.ds(pl.multiple_of(slot * ROW_TILE, ROW_TILE), ROW_TILE)], sem)

    def issue(n, carry):
        copy(n, dest_ref[2 * n]).start(priority=0)
        copy(n, dest_ref[2 * n + 1]).start(priority=1)
        return carry

    lax.fori_loop(0, tm, issue, 0)

    def drain(n, carry):
        copy(0, 0).wait()
        copy(0, 0).wait()
        return carry

    lax.fori_loop(0, tm, drain, 0)


def _dispatch(h2, dest_flat, n_rows, buf0, tm=MOE_TM):
    T = h2.shape[0] // ROW_TILE
    return pl.pallas_call(
        functools.partial(_dispatch_kernel, tm=tm),
        grid=(T // tm,),
        in_specs=[pl.BlockSpec((2 * tm,), lambda i: (i,), memory_space=pltpu.SMEM),
                  pl.BlockSpec((tm * ROW_TILE, LANES), lambda i: (i, 0)),
                  pl.BlockSpec(memory_space=pl.ANY)],
        out_specs=pl.BlockSpec(memory_space=pl.ANY),
        out_shape=jax.ShapeDtypeStruct((n_rows * ROW_TILE, LANES), F32),
        scratch_shapes=[pltpu.SemaphoreType.DMA(())],
        input_output_aliases={2: 0},
        compiler_params=_cparams(("arbitrary",)),
        name="moe_dispatch",
    )(dest_flat, h2, buf0)


def _moe_ffn_kernel(be_ref, nu_ref, x_ref, w1_ref, w3_ref, w2_ref, y_ref, acc_scr):
    b = pl.program_id(0)
    f = pl.program_id(1)

    @pl.when(b < nu_ref[0])
    def _():
        @pl.when(f == 0)
        def _():
            acc_scr[...] = jnp.zeros(acc_scr.shape, F32)

        h = _tiles_to_rows(x_ref, MOE_BLOCK).astype(BF16)
        u = jnp.dot(h, w1_ref[...], preferred_element_type=F32)
        v = jnp.dot(h, w3_ref[...], preferred_element_type=F32)
        a = (u * jax.nn.sigmoid(u) * v).astype(BF16)
        acc_scr[...] += jnp.dot(a, w2_ref[...], preferred_element_type=F32)

        @pl.when(f == pl.num_programs(1) - 1)
        def _():
            _rows_to_tiles(y_ref, acc_scr[...])

    @pl.when(jnp.logical_and(b >= nu_ref[0], f == 0))
    def _():
        y_ref[...] = jnp.zeros(y_ref.shape, F32)


def _moe_ffn(buf, block_expert, n_used, w1, w3, w2, tf=MOE_TF):
    R = buf.shape[0] // ROW_TILE
    E, D, F = w1.shape
    nb = R // MOE_BLOCK

    def last_live(b, nu):
        return jnp.maximum(jnp.minimum(b, nu[0] - 1), 0)

    def row_map(b, f, be, nu):
        return (last_live(b, nu), 0)

    def w_in_map(b, f, be, nu):
        live = b < nu[0]
        return (be[last_live(b, nu)], 0, jnp.where(live, f, F // tf - 1))

    def w_out_map(b, f, be, nu):
        live = b < nu[0]
        return (be[last_live(b, nu)], jnp.where(live, f, F // tf - 1), 0)

    return pl.pallas_call(
        _moe_ffn_kernel,
        grid_spec=pltpu.PrefetchScalarGridSpec(
            num_scalar_prefetch=2,
            grid=(nb, F // tf),
            in_specs=[pl.BlockSpec((MOE_BLOCK * ROW_TILE, LANES), row_map),
                      pl.BlockSpec((None, D, tf), w_in_map),
                      pl.BlockSpec((None, D, tf), w_in_map),
                      pl.BlockSpec((None, tf, D), w_out_map)],
            out_specs=pl.BlockSpec((MOE_BLOCK * ROW_TILE, LANES), lambda b, f, be, nu: (b, 0)),
            scratch_shapes=[pltpu.VMEM((MOE_BLOCK, D), F32)],
        ),
        out_shape=jax.ShapeDtypeStruct((R * ROW_TILE, LANES), F32),
        compiler_params=_cparams(("arbitrary", "arbitrary")),
        name="moe_ffn",
    )(block_expert, n_used, buf, w1, w3, w2)


def _combine_kernel(dest_ref, dnext_ref, x_ref, mf_ref, g_ref, ybuf_hbm, y_ref, rows_scr, sems, *, tm, final_norm):
    i = pl.program_id(0)
    last = pl.num_programs(0) - 1

    def copy(slot, par, k, n):
        return pltpu.make_async_copy(ybuf_hbm.at[pl.ds(pl.multiple_of(slot * ROW_TILE, ROW_TILE), ROW_TILE)],
                                     rows_scr.at[par, k, pl.ds(pl.multiple_of(n * ROW_TILE, ROW_TILE), ROW_TILE)],
                                     sems.at[par])

    def issue_tile(idx_ref, par):
        def issue(n, carry):
            copy(idx_ref[2 * n], par, 0, n).start(priority=0)
            copy(idx_ref[2 * n + 1], par, 1, n).start(priority=1)
            return carry

        lax.fori_loop(0, tm, issue, 0)

    def wait_tile(par):
        def drain(n, carry):
            copy(0, par, 0, 0).wait()
            copy(0, par, 1, 0).wait()
            return carry

        lax.fori_loop(0, tm, drain, 0)

    def finish(par):
        wait_tile(par)
        mf = mf_ref[...]
        y = (x_ref[...] + mf[:, 0:1] * _tiles_to_rows(rows_scr.at[par, 0], tm)
             + mf[:, 1:2] * _tiles_to_rows(rows_scr.at[par, 1], tm))
        if final_norm:
            ms = jnp.mean(y * y, axis=-1, keepdims=True)
            y = y * lax.rsqrt(ms + NORM_EPS) * g_ref[...]
        y_ref[...] = y

    @pl.when(i == 0)
    def _():
        issue_tile(dest_ref, 0)

    for par in range(2):
        @pl.when(jnp.logical_and(i % 2 == par, i < last))
        def _(par=par):
            issue_tile(dnext_ref, 1 - par)

        @pl.when(i % 2 == par)
        def _(par=par):
            finish(par)


def _combine(x2, mf, dest_flat, ybuf, g_final, final_norm, tm=MOE_TM):
    T, D = x2.shape
    nt = T // tm
    return pl.pallas_call(
        functools.partial(_combine_kernel, tm=tm, final_norm=final_norm),
        grid=(nt,),
        in_specs=[pl.BlockSpec((2 * tm,), lambda i: (i,), memory_space=pltpu.SMEM),
                  pl.BlockSpec((2 * tm,), lambda i: (jnp.minimum(i + 1, nt - 1),), memory_space=pltpu.SMEM),
                  pl.BlockSpec((tm, D), lambda i: (i, 0)),
                  pl.BlockSpec((tm, LANES), lambda i: (i, 0)),
                  pl.BlockSpec((1, D), lambda i: (0, 0)),
                  pl.BlockSpec(memory_space=pl.ANY)],
        out_specs=pl.BlockSpec((tm, D), lambda i: (i, 0)),
        out_shape=jax.ShapeDtypeStruct((T, D), F32),
        scratch_shapes=[pltpu.VMEM((2, 2, tm * ROW_TILE, LANES), F32), pltpu.SemaphoreType.DMA((2,))],
        compiler_params=_cparams(("arbitrary",)),
        name="moe_combine",
    )(dest_flat, dest_flat, x2, mf, g_final.reshape(1, D), ybuf)


def _moe_layer(x2, o2, wo, g, router, w1, w3, w2, g_final, final_norm, buf0=None):
    T, D = x2.shape
    x2, h2, mi, mf, cnt = _router(x2, o2, wo, g, router)
    counts = cnt[0, :N_EXPERTS].astype(jnp.int32)
    padded = ((counts + MOE_BLOCK - 1) // MOE_BLOCK) * MOE_BLOCK
    pend = jnp.cumsum(padded)
    pstart = pend - padded
    dest = pstart[mi[:, 0:2]] + mi[:, 2:4]
    dest_flat = dest.reshape(-1).astype(jnp.int32)
    nb = -(-(2 * T + N_EXPERTS * (MOE_BLOCK - 1)) // MOE_BLOCK)
    block_start = jnp.arange(nb, dtype=jnp.int32) * MOE_BLOCK
    block_expert = jnp.minimum(jnp.sum(pend[None, :] <= block_start[:, None], axis=1),
                               N_EXPERTS - 1).astype(jnp.int32)
    n_used = (pend[-1:] // MOE_BLOCK).astype(jnp.int32)
    if buf0 is None:
        buf0 = jnp.zeros((nb * MOE_BLOCK * ROW_TILE, LANES), F32)
    buf = _dispatch(h2, dest_flat, nb * MOE_BLOCK, buf0)
    ybuf = _moe_ffn(buf, block_expert, n_used, w1, w3, w2)
    return _combine(x2, mf, dest_flat, ybuf, g_final, final_norm), ybuf


def _mixer_a(x, g, w_in, w_out, tabs):
    B, S, D = x.shape
    w = w_in.astype(BF16)
    qkvs = []
    for gi, dil in enumerate(DILATIONS):
        cos, sin = tabs[dil]
        qkvs.append(_norm_proj(x, g, w, cos, sin, dil=dil, col_block=gi, ncols=3 * D, tc=D,
                               rope_chunks=_QKV_ROPE_CHUNKS))
    o = _attn_a(qkvs, B, S)
    return o.reshape(B * S, D), w_out.astype(BF16)


def _mixer_b(x, g, w_in, lq1, lk1, lq2, lk2, subln, w_out, lambda_init, tabs, side_casts=()):
    B, S, D = x.shape
    cos, sin = tabs[1]
    qkv = _norm_proj(x, g, w_in.astype(BF16), cos, sin, dil=1, col_block=0, ncols=3 * D, tc=D,
                     rope_chunks=_QKV_ROPE_CHUNKS).reshape(B, S, 3 * D)
    lam_params = jnp.stack([lq1, lk1, lq2, lk2]).astype(F32)
    o, casted = _attn_b(qkv, lam_params, subln.astype(F32), lambda_init, B, S, side_casts)
    return o.reshape(B * S, D), w_out.astype(BF16), casted


def _mixer_c(x, g, w_in, sink, w_out, tabs):
    B, S, D = x.shape
    cos, sin = tabs[1]
    head_order = jnp.arange(16).reshape(2, 8).T.reshape(-1)
    col_order = (head_order[:, None] * HEAD_DIM + jnp.arange(HEAD_DIM)[None, :]).reshape(-1)
    w_q = w_in[:, :D][:, col_order]
    w = jnp.concatenate([w_q, w_in[:, D:]], axis=1).astype(BF16)
    ncols = w.shape[1]
    chunks = tuple(c < (ncols // LANES - 1) for c in range(ncols // LANES))
    qkv = _norm_proj(x, g, w, cos, sin, dil=1, col_block=0, ncols=ncols, tc=ncols,
                     rope_chunks=chunks).reshape(B, S, ncols)
    sk = sink.astype(F32)
    sink_tab = jnp.concatenate([jnp.broadcast_to(sk[:8, None], (8, LANES)),
                                jnp.broadcast_to(sk[8:, None], (8, LANES))], axis=1)
    o = _attn_c(qkv, sink_tab, B, S)
    w_o = w_out[col_order, :].astype(BF16)
    return o.reshape(B * S, D), w_o


def kernel(x, positions, l0_norm_mix, l0_a_w_in, l0_a_w_out, l0_norm_ffn, l0_ffn_w1, l0_ffn_w3, l0_ffn_w2, l1_norm_mix, l1_b_w_in, l1_b_lambda_q1, l1_b_lambda_k1, l1_b_lambda_q2, l1_b_lambda_k2, l1_b_subln, l1_b_w_out, l1_norm_ffn, l1_moe_router, l1_moe_w1, l1_moe_w3, l1_moe_w2, l2_norm_mix, l2_c_w_in, l2_c_sink, l2_c_w_out, l2_norm_ffn, l2_ffn_w1, l2_ffn_w3, l2_ffn_w2, l3_norm_mix, l3_a_w_in, l3_a_w_out, l3_norm_ffn, l3_moe_router, l3_moe_w1, l3_moe_w3, l3_moe_w2, final_norm):
    B, S, D = x.shape
    T = B * S
    cos, sin = lax.optimization_barrier(_rope_tables(positions))
    tabs = {d: (_to_strided(cos, d), _to_strided(sin, d)) for d in DILATIONS}

    o, wo = _mixer_a(x, l0_norm_mix, l0_a_w_in, l0_a_w_out, tabs)
    x = _ffn_dense(x.reshape(T, D), o, wo, l0_norm_ffn, l0_ffn_w1.astype(BF16), l0_ffn_w3.astype(BF16),
                   l0_ffn_w2.astype(BF16)).reshape(B, S, D)
    lambda_init = 0.8 - 0.6 * math.exp(-0.3 * 1)
    moe_w = (l1_moe_w1, l1_moe_w3, l1_moe_w2, l3_moe_w1, l3_moe_w3, l3_moe_w2)
    o, wo, moe_bf = _mixer_b(x, l1_norm_mix, l1_b_w_in, l1_b_lambda_q1, l1_b_lambda_k1, l1_b_lambda_q2,
                             l1_b_lambda_k2, l1_b_subln, l1_b_w_out, lambda_init, tabs,
                             side_casts=tuple(w.reshape(-1, w.shape[-1]) for w in moe_w))
    moe_bf = [wb.reshape(w.shape) for wb, w in zip(moe_bf, moe_w)]
    x, spare = _moe_layer(x.reshape(T, D), o, wo, l1_norm_ffn, l1_moe_router, moe_bf[0], moe_bf[1], moe_bf[2],
                          final_norm, False)
    x = x.reshape(B, S, D)
    o, wo = _mixer_c(x, l2_norm_mix, l2_c_w_in, l2_c_sink, l2_c_w_out, tabs)
    x = _ffn_dense(x.reshape(T, D), o, wo, l2_norm_ffn, l2_ffn_w1.astype(BF16), l2_ffn_w3.astype(BF16),
                   l2_ffn_w2.astype(BF16)).reshape(B, S, D)
    o, wo = _mixer_a(x, l3_norm_mix, l3_a_w_in, l3_a_w_out, tabs)
    x, _ = _moe_layer(x.reshape(T, D), o, wo, l3_norm_ffn, l3_moe_router, moe_bf[3], moe_bf[4], moe_bf[5],
                      final_norm, True, buf0=spare)
    return x.reshape(B, S, D)
```

```python
import functools
import math

import jax
import jax.numpy as jnp
from jax import lax
from jax.experimental import pallas as pl
from jax.experimental.pallas import tpu as pltpu

F32 = jnp.float32
BF16 = jnp.bfloat16

D_MODEL = 1024
HEAD_DIM = 64
ROT_DIM = HEAD_DIM // 4
ROPE_THETA = 500000.0
NORM_EPS = 1e-6
SUBLN_EPS = 1e-5
DILATIONS = (1, 4, 16)
A_HALF = 64
C_HALF = 128
N_EXPERTS = 8
_QKV_ROPE_CHUNKS = (True,) * 16 + (False,) * 8

LANES = 128
F32_SUBLANES = 8
BF16_SUBLANES = 16
VMEM_LIMIT = 56 * 1024 * 1024

PROJ_TM = 512
Q_SUB = LANES
ATTN_A_TB = 2048
ATTN_C_TB = 1024
ATTN_B_TQ = 1024
ATTN_B_TK = 2048
CK = 512
A_GROUP = 3
C_AHEAD = 4
FFN_TM = 512
MOE_TM = 512
MOE_BLOCK = 512
MOE_TF = 1792


def _cparams(sem):
    return pltpu.CompilerParams(dimension_semantics=sem, vmem_limit_bytes=VMEM_LIMIT)


def _rope_tables(positions):
    lane = jnp.arange(LANES) % HEAD_DIM
    half = ROT_DIM // 2
    inv_freq = ROPE_THETA ** (-((2 * (lane % half)).astype(F32) / ROT_DIM))
    freq = jnp.where(lane < ROT_DIM, inv_freq, 0.0)
    sign = jnp.where(lane < half, -1.0, jnp.where(lane < ROT_DIM, 1.0, 0.0))
    ang = positions.astype(F32)[..., None] * freq
    return jnp.cos(ang), jnp.sin(ang) * sign


def _to_strided(t, d):
    B, S, C = t.shape
    return jnp.swapaxes(t.reshape(B, S // d, d, C), 1, 2)


def _rope_chunk(x, cos, sin, first_half):
    partner = jnp.where(first_half, pltpu.roll(x, LANES - ROT_DIM // 2, 1),
                        pltpu.roll(x, ROT_DIM // 2, 1))
    return x * cos + partner * sin


def _proj_kernel(*refs, dil, rope_chunks, tc):
    if dil > 1:
        x_ref, g_ref, w_ref, cos_ref, sin_ref, perm_ref, o_ref = refs
    else:
        x_ref, g_ref, w_ref, cos_ref, sin_ref, o_ref = refs
    tm = x_ref.shape[0]
    ncols = w_ref.shape[1]
    n = tm // dil
    x = x_ref[...]
    ms = jnp.mean(x * x, axis=-1, keepdims=True)
    h = (x * lax.rsqrt(ms + NORM_EPS) * g_ref[...]).astype(BF16)
    if dil > 1:
        h = jnp.dot(perm_ref[...], h, preferred_element_type=F32).astype(BF16)
    lane = lax.broadcasted_iota(jnp.int32, (1, LANES), 1)
    first_half = (lane % HEAD_DIM) < (ROT_DIM // 2)
    cos = cos_ref[...].reshape(tm, LANES)
    sin = sin_ref[...].reshape(tm, LANES)
    for c0 in range(0, ncols, tc):
        res = jnp.dot(h, w_ref[:, c0:c0 + tc], preferred_element_type=F32)
        parts = []
        for cc in range(tc // LANES):
            xc = res[:, cc * LANES:(cc + 1) * LANES]
            if rope_chunks[c0 // LANES + cc]:
                xc = _rope_chunk(xc, cos, sin, first_half)
            parts.append(xc.astype(o_ref.dtype))
        val = jnp.concatenate(parts, axis=1)
        for r in range(dil):
            o_ref[r, :, c0:c0 + tc] = val[r * n:(r + 1) * n]


def _norm_proj(x, g, w, cos, sin, *, dil, col_block, ncols, tc, rope_chunks, tm=PROJ_TM):
    B, S, D = x.shape
    n = tm // dil
    L = S // dil
    in_specs = [
        pl.BlockSpec((None, tm, D), lambda b, i: (b, i, 0)),
        pl.BlockSpec((1, D), lambda b, i: (0, 0)),
        pl.BlockSpec((D, ncols), lambda b, i: (0, col_block)),
        pl.BlockSpec((None, dil, n, LANES), lambda b, i: (b, 0, i, 0)),
        pl.BlockSpec((None, dil, n, LANES), lambda b, i: (b, 0, i, 0)),
    ]
    args = [x, g.reshape(1, D), w, cos, sin]
    if dil > 1:
        p = jnp.arange(tm)
        src = (p % n) * dil + p // n
        perm = (src[:, None] == jnp.arange(tm)[None, :]).astype(BF16)
        in_specs.append(pl.BlockSpec((tm, tm), lambda b, i: (0, 0)))
        args.append(perm)
    return pl.pallas_call(
        functools.partial(_proj_kernel, dil=dil, rope_chunks=rope_chunks, tc=tc),
        grid=(B, S // tm),
        in_specs=in_specs,
        out_specs=pl.BlockSpec((None, dil, n, ncols), lambda b, i: (b, 0, i, 0)),
        out_shape=jax.ShapeDtypeStruct((B, dil, L, ncols), BF16),
        compiler_params=_cparams(("parallel", "parallel")),
        name=f"norm_proj_d{dil}",
    )(*args)


def _fill_band_bias(bias_scr, tq, W, half):
    kk = lax.broadcasted_iota(jnp.int32, (W, 2 * tq), 0)
    qq = lax.broadcasted_iota(jnp.int32, (W, 2 * tq), 1) % tq
    for i in range(3):
        ok = jnp.abs(qq + i * half - kk) <= half
        bias_scr[i] = jnp.where(ok, 0.0, -jnp.inf).astype(F32)


def _band_scores(q2, kw, bias):
    lane_lo = lax.broadcasted_iota(jnp.int32, (1, LANES), 1) < HEAD_DIM
    zero = jnp.zeros_like(q2)
    qq = jnp.concatenate([jnp.where(lane_lo, q2, zero), jnp.where(lane_lo, zero, q2)], axis=0)
    qq = qq * jnp.asarray(HEAD_DIM ** -0.5, q2.dtype)
    return lax.dot_general(kw, qq, (((1,), (1,)), ((), ())), preferred_element_type=F32) + bias


def _band_finish(st, vw, want_lse, sink_row=None):
    tq = st.shape[1] // 2
    row_lo = lax.broadcasted_iota(jnp.int32, (LANES, 1), 0) < HEAD_DIM
    m = jnp.max(st, axis=0, keepdims=True)
    if sink_row is not None:
        m = jnp.maximum(m, sink_row)
    p = jnp.exp(st - m)
    l = jnp.sum(p, axis=0, keepdims=True)
    if sink_row is not None:
        l = l + jnp.exp(sink_row - m)
    ot = lax.dot_general(vw, p.astype(BF16), (((0,), (0,)), ((), ())), preferred_element_type=F32) / l
    o = jnp.where(row_lo, ot[:, :tq], ot[:, tq:]).T
    if not want_lse:
        return o, None
    lse_row = m + jnp.log(l)
    lse = jnp.where(row_lo, jnp.broadcast_to(lse_row[:, :tq], (LANES, tq)),
                    jnp.broadcast_to(lse_row[:, tq:], (LANES, tq))).T
    return o, lse


def _attn_a_kernel(*refs, TB, S):
    qkv = refs[:9]
    o_ref = refs[9]
    scr = refs[10:16]
    bias_scr = refs[16]
    t = pl.program_id(2)
    W = 4 * A_HALF

    @pl.when(t == 0)
    def _():
        _fill_band_bias(bias_scr, Q_SUB, W, A_HALF)

    subs = []
    for g, dil in enumerate(DILATIONS):
        nq = TB // dil
        tq = min(nq, Q_SUB)
        for r in range(dil):
            for jj in range(nq // tq):
                subs.append((g, dil, nq, tq, r, jj))

    def scores(sub):
        g, dil, nq, tq, r, jj = sub
        q_ref, k_ref = qkv[3 * g], qkv[3 * g + 1]
        L = S // dil
        qs = t * nq + jj * tq
        ws = pl.multiple_of(jnp.clip(qs - A_HALF, 0, L - W), A_HALF)
        q2 = q_ref[r, jj * tq:(jj + 1) * tq, :]
        return _band_scores(q2, k_ref[r, pl.ds(ws, W), :], bias_scr[(qs - ws) // A_HALF]), ws

    def finish(sub, st, ws):
        g, dil, nq, tq, r, jj = sub
        o_scr, l_scr = scr[2 * g], scr[2 * g + 1]
        o, lse = _band_finish(st, qkv[3 * g + 2][r, pl.ds(ws, W), :], True)
        row0 = jj * (tq * dil) + r
        if dil == 1:
            o_scr[row0:row0 + tq, :] = o
            l_scr[row0:row0 + tq, :] = lse
        else:
            o_scr[pl.ds(row0, tq, stride=dil), :] = o
            l_scr[pl.ds(row0, tq, stride=dil), :] = lse

    pend = [scores(sub) for sub in subs[:A_GROUP]]
    for n, sub in enumerate(subs):
        if n + A_GROUP < len(subs):
            pend.append(scores(subs[n + A_GROUP]))
        st, ws = pend[n]
        finish(sub, st, ws)

    l0, l1, l2 = scr[1][...], scr[3][...], scr[5][...]
    mx = jnp.maximum(jnp.maximum(l0, l1), l2)
    e0, e1, e2 = jnp.exp(l0 - mx), jnp.exp(l1 - mx), jnp.exp(l2 - mx)
    den = e0 + e1 + e2
    o = (e0 * scr[0][...] + e1 * scr[2][...] + e2 * scr[4][...]) / den
    o_ref[...] = o.astype(o_ref.dtype)


def _attn_a(qkvs, B, S, TB=ATTN_A_TB):
    n_pairs = (16 * HEAD_DIM) // LANES
    in_specs, args = [], []
    for g, dil in enumerate(DILATIONS):
        L = S // dil
        in_specs += [
            pl.BlockSpec((None, dil, TB // dil, LANES), lambda b, hp, t: (b, 0, t, hp)),
            pl.BlockSpec((None, dil, L, LANES), lambda b, hp, t: (b, 0, 0, n_pairs + hp)),
            pl.BlockSpec((None, dil, L, LANES), lambda b, hp, t: (b, 0, 0, 2 * n_pairs + hp)),
        ]
        args += [qkvs[g]] * 3
    return pl.pallas_call(
        functools.partial(_attn_a_kernel, TB=TB, S=S),
        grid=(B, n_pairs, S // TB),
        in_specs=in_specs,
        out_specs=pl.BlockSpec((None, TB, LANES), lambda b, hp, t: (b, t, hp)),
        out_shape=jax.ShapeDtypeStruct((B, S, n_pairs * LANES), BF16),
        scratch_shapes=[pltpu.VMEM((TB, LANES), F32)] * 6
        + [pltpu.VMEM((3, Q_SUB + 2 * A_HALF, 2 * Q_SUB), F32)],
        compiler_params=_cparams(("parallel", "parallel", "arbitrary")),
        name="attn_dilated",
    )(*args)


def _attn_c_kernel(q_ref, k_ref, v_ref, sink_ref, o_ref, bias_scr, *, TB, S):
    t = pl.program_id(1)
    tq = Q_SUB
    W = tq + 2 * C_HALF
    n_pairs = q_ref.shape[1] // LANES

    @pl.when(t == 0)
    def _():
        _fill_band_bias(bias_scr, tq, W, C_HALF)

    units = [(jj, p) for jj in range(TB // tq) for p in range(n_pairs)]

    def window(jj):
        qs = t * TB + jj * tq
        ws = pl.multiple_of(jnp.clip(qs - C_HALF, 0, S - W), C_HALF)
        return qs, ws

    def scores(unit):
        jj, p = unit
        qs, ws = window(jj)
        q2 = q_ref[jj * tq:(jj + 1) * tq, p * LANES:(p + 1) * LANES]
        return _band_scores(q2, k_ref[pl.ds(ws, W), :], bias_scr[(qs - ws) // C_HALF])

    pend = [scores(u) for u in units[:C_AHEAD]]
    for n, (jj, p) in enumerate(units):
        if n + C_AHEAD < len(units):
            pend.append(scores(units[n + C_AHEAD]))
        _, ws = window(jj)
        sink_row = sink_ref[p:p + 1, :]
        o, _ = _band_finish(pend[n], v_ref[pl.ds(ws, W), :], False, sink_row=sink_row)
        pend[n] = None
        o_ref[jj * tq:(jj + 1) * tq, p * LANES:(p + 1) * LANES] = o.astype(o_ref.dtype)


def _attn_c(qkv, sink_tab, B, S, TB=ATTN_C_TB):
    nq = 16 * HEAD_DIM
    return pl.pallas_call(
        functools.partial(_attn_c_kernel, TB=TB, S=S),
        grid=(B, S // TB),
        in_specs=[
            pl.BlockSpec((None, TB, nq), lambda b, t: (b, t, 0)),
            pl.BlockSpec((None, S, LANES), lambda b, t: (b, 0, nq // LANES)),
            pl.BlockSpec((None, S, LANES), lambda b, t: (b, 0, nq // LANES + 1)),
            pl.BlockSpec((8, 2 * LANES), lambda b, t: (0, 0)),
        ],
        out_specs=pl.BlockSpec((None, TB, nq), lambda b, t: (b, t, 0)),
        out_shape=jax.ShapeDtypeStruct((B, S, nq), BF16),
        scratch_shapes=[pltpu.VMEM((3, Q_SUB + 2 * C_HALF, 2 * Q_SUB), F32)],
        compiler_params=_cparams(("parallel", "arbitrary")),
        name="attn_swa_sink",
    )(qkv, qkv, qkv, sink_tab)


def _attn_b_kernel(*refs, lambda_init, n_cast):
    lam_ref, q_ref, qn_ref, k_ref, kn_ref, v_ref, subln_ref = refs[:7]
    cast_in = refs[7:7 + n_cast]
    o_ref = refs[7 + n_cast]
    cast_out = refs[8 + n_cast:8 + 2 * n_cast]
    s0_scr, s1_scr, m_scr, l_scr, acc_scr = refs[8 + 2 * n_cast:]
    _attn_b_body(lam_ref, q_ref, qn_ref, k_ref, kn_ref, v_ref, subln_ref, o_ref,
                 s0_scr, s1_scr, m_scr, l_scr, acc_scr, lambda_init=lambda_init)
    for src_ref, dst_ref in zip(cast_in, cast_out):
        dst_ref[...] = src_ref[...].astype(dst_ref.dtype)


def _attn_b_body(lam_ref, q_ref, qn_ref, k_ref, kn_ref, v_ref, subln_ref, o_ref,
                 s0_scr, s1_scr, m_scr, l_scr, acc_scr, *, lambda_init):
    i = pl.program_id(2)
    kv = pl.program_id(3)
    nkv = pl.num_programs(3)
    tk = k_ref.shape[0]
    lane = lax.broadcasted_iota(jnp.int32, (1, LANES), 1)
    lane_lo = lane < HEAD_DIM

    def scores(kref, qref, c, j):
        k = kref[j * CK:(j + 1) * CK, :] * jnp.asarray(HEAD_DIM ** -0.5, kref.dtype)
        sel = lane_lo if c == 0 else jnp.logical_not(lane_lo)
        kc = jnp.where(sel, k, jnp.zeros_like(k))
        return lax.dot_general(kc, qref[...], (((1,), (1,)), ((), ())), preferred_element_type=F32)

    @pl.when(kv == 0)
    def _():
        m_scr[...] = jnp.full(m_scr.shape, -jnp.inf, F32)
        l_scr[...] = jnp.zeros(l_scr.shape, F32)
        acc_scr[...] = jnp.zeros(acc_scr.shape, F32)

    @pl.when(jnp.logical_and(i == 0, kv == 0))
    def _():
        for c in range(2):
            for j in range(tk // CK):
                s0_scr[c * tk + j * CK:c * tk + (j + 1) * CK, :] = scores(k_ref, q_ref, c, j)

    def step(cur_scr, nxt_scr):
        vt = jnp.concatenate([v_ref[...].T, jnp.ones((BF16_SUBLANES, tk), BF16)], axis=0)
        for c in range(2):
            m, l, acc = m_scr[c], l_scr[c], acc_scr[c]
            for j in range(tk // CK):
                rows = slice(c * tk + j * CK, c * tk + (j + 1) * CK)
                nxt_scr[rows, :] = scores(kn_ref, qn_ref, c, j)
                s = cur_scr[rows, :]
                m_new = jnp.maximum(m, jnp.max(s, axis=0, keepdims=True))
                alpha = jnp.exp(m - m_new)
                p = jnp.exp(s - m_new).astype(BF16)
                pv = jnp.dot(vt[:, j * CK:(j + 1) * CK], p, preferred_element_type=F32)
                l = alpha * l + pv[LANES:LANES + 1]
                acc = alpha * acc + pv[:LANES]
                m = m_new
            m_scr[c], l_scr[c], acc_scr[c] = m, l, acc

    @pl.when(kv % 2 == 0)
    def _():
        step(s0_scr, s1_scr)

    @pl.when(kv % 2 == 1)
    def _():
        step(s1_scr, s0_scr)

    @pl.when(kv == nkv - 1)
    def _():
        lp = lam_ref[...]
        lam = (jnp.exp(jnp.sum(lp[0:1] * lp[1:2], axis=-1, keepdims=True))
               - jnp.exp(jnp.sum(lp[2:3] * lp[3:4], axis=-1, keepdims=True)) + lambda_init)
        o = acc_scr[0] / l_scr[0] - lam * (acc_scr[1] / l_scr[1])
        ms = jnp.mean(o * o, axis=0, keepdims=True)
        o = o * lax.rsqrt(ms + SUBLN_EPS) * subln_ref[...] * (1.0 - lambda_init)
        o_ref[...] = o.T.astype(o_ref.dtype)


def _attn_b(qkv, lam_params, subln, lambda_init, B, S, side_casts=(), tq=ATTN_B_TQ, tk=ATTN_B_TK):
    H = 8
    nq, nkv = S // tq, S // tk
    assert nkv % 2 == 0
    nsteps = B * H * nq * nkv
    cast_specs, cast_shapes = [], []
    for w in side_casts:
        rows, cols = w.shape
        nblk = nsteps
        while rows % nblk or (rows // nblk) % BF16_SUBLANES:
            nblk //= 2
        per = nsteps // nblk

        def w_map(b, h, i, kv, per=per):
            return ((((b * H + h) * nq + i) * nkv + kv) // per, 0)

        cast_specs.append(pl.BlockSpec((rows // nblk, cols), w_map))
        cast_shapes.append(jax.ShapeDtypeStruct((rows, cols), BF16))

    def q_next(b, h, i, kv):
        return (b, jnp.minimum(i + (kv + 1) // nkv, nq - 1), h)

    def k_next(b, h, i, kv):
        return (b, (kv + 1) % nkv, H + h)

    outs = pl.pallas_call(
        functools.partial(_attn_b_kernel, lambda_init=lambda_init, n_cast=len(side_casts)),
        grid=(B, H, nq, nkv),
        in_specs=[
            pl.BlockSpec((4, HEAD_DIM), lambda b, h, i, kv: (0, 0)),
            pl.BlockSpec((None, tq, LANES), lambda b, h, i, kv: (b, i, h)),
            pl.BlockSpec((None, tq, LANES), q_next),
            pl.BlockSpec((None, tk, LANES), lambda b, h, i, kv: (b, kv, H + h)),
            pl.BlockSpec((None, tk, LANES), k_next),
            pl.BlockSpec((None, tk, LANES), lambda b, h, i, kv: (b, kv, 2 * H + h)),
            pl.BlockSpec((LANES, 1), lambda b, h, i, kv: (0, 0)),
        ] + cast_specs,
        out_specs=[pl.BlockSpec((None, tq, LANES), lambda b, h, i, kv: (b, i, h))] + cast_specs,
        out_shape=[jax.ShapeDtypeStruct((B, S, H * LANES), BF16)] + cast_shapes,
        scratch_shapes=[pltpu.VMEM((2 * tk, tq), F32), pltpu.VMEM((2 * tk, tq), F32),
                        pltpu.VMEM((2, 1, tq), F32), pltpu.VMEM((2, 1, tq), F32),
                        pltpu.VMEM((2, LANES, tq), F32)],
        compiler_params=_cparams(("parallel", "parallel", "arbitrary", "arbitrary")),
        name="attn_diff",
    )(lam_params, qkv, qkv, qkv, qkv, qkv, subln.reshape(LANES, 1), *side_casts)
    return outs[0], outs[1:]


def _ffn_kernel(x_ref, o_ref, wo_ref, g_ref, w1_ref, w3_ref, w2_ref, y_ref):
    x = x_ref[...] + jnp.dot(o_ref[...], wo_ref[...], preferred_element_type=F32)
    ms = jnp.mean(x * x, axis=-1, keepdims=True)
    h = (x * lax.rsqrt(ms + NORM_EPS) * g_ref[...]).astype(BF16)
    u = jnp.dot(h, w1_ref[...], preferred_element_type=F32)
    v = jnp.dot(h, w3_ref[...], preferred_element_type=F32)
    a = (u * jax.nn.sigmoid(u) * v).astype(BF16)
    y_ref[...] = x + jnp.dot(a, w2_ref[...], preferred_element_type=F32)


def _ffn_dense(x2, o2, wo, g, w1, w3, w2, tm=FFN_TM):
    T, D = x2.shape
    K = o2.shape[1]
    F = w1.shape[1]
    return pl.pallas_call(
        _ffn_kernel,
        grid=(T // tm,),
        in_specs=[pl.BlockSpec((tm, D), lambda i: (i, 0)),
                  pl.BlockSpec((tm, K), lambda i: (i, 0)),
                  pl.BlockSpec((K, D), lambda i: (0, 0)),
                  pl.BlockSpec((1, D), lambda i: (0, 0)),
                  pl.BlockSpec((D, F), lambda i: (0, 0)),
                  pl.BlockSpec((D, F), lambda i: (0, 0)),
                  pl.BlockSpec((F, D), lambda i: (0, 0))],
        out_specs=pl.BlockSpec((tm, D), lambda i: (i, 0)),
        out_shape=jax.ShapeDtypeStruct((T, D), F32),
        compiler_params=_cparams(("parallel",)),
        name="ffn_dense",
    )(x2, o2, wo, g.reshape(1, D), w1, w3, w2)


ROW_TILE = F32_SUBLANES


def _rows_to_tiles(ref, val):
    n = val.shape[0]
    for a in range(ROW_TILE):
        ref[pl.ds(a, n, stride=ROW_TILE), :] = val[:, a * LANES:(a + 1) * LANES]


def _tiles_to_rows(ref, n):
    return jnp.concatenate([ref[pl.ds(a, n, stride=ROW_TILE), :] for a in range(ROW_TILE)], axis=1)


def _router_kernel(x_ref, o_ref, wo_ref, g_ref, r_ref, x1_ref, h_ref, mi_ref, mf_ref, cnt_ref, carry_scr):
    i = pl.program_id(0)
    tm = x_ref.shape[0]

    @pl.when(i == 0)
    def _():
        carry_scr[...] = jnp.zeros(carry_scr.shape, F32)

    x = x_ref[...] + jnp.dot(o_ref[...], wo_ref[...], preferred_element_type=F32)
    x1_ref[...] = x
    ms = jnp.mean(x * x, axis=-1, keepdims=True)
    h = x * lax.rsqrt(ms + NORM_EPS) * g_ref[...]
    _rows_to_tiles(h_ref, h)
    lane = lax.broadcasted_iota(jnp.int32, (tm, LANES), 1)
    logits = jnp.full((tm, LANES), -jnp.inf, F32)
    for e in range(N_EXPERTS):
        col = jnp.sum(h * r_ref[e:e + 1, :], axis=-1, keepdims=True)
        logits = jnp.where(lane == e, col, logits)
    v0 = jnp.max(logits, axis=-1, keepdims=True)
    i0 = jnp.min(jnp.where(logits == v0, lane, LANES), axis=-1, keepdims=True)
    rest = jnp.where(lane == i0, -jnp.inf, logits)
    v1 = jnp.max(rest, axis=-1, keepdims=True)
    i1 = jnp.min(jnp.where(rest == v1, lane, LANES), axis=-1, keepdims=True)
    tt = jnp.exp(v1 - v0)
    g0 = 1.0 / (1.0 + tt)
    g1 = tt / (1.0 + tt)
    sel0 = lane == i0
    sel1 = lane == i1
    onehot = jnp.where(jnp.logical_or(sel0, sel1), 1.0, 0.0)
    row = lax.broadcasted_iota(jnp.int32, (tm, tm), 0)
    col = lax.broadcasted_iota(jnp.int32, (tm, tm), 1)
    tri = jnp.where(row > col, 1.0, 0.0).astype(BF16)
    before = carry_scr[...] + jnp.dot(tri, onehot.astype(BF16), preferred_element_type=F32)
    rank0 = jnp.sum(jnp.where(sel0, before, 0.0), axis=-1, keepdims=True)
    rank1 = jnp.sum(jnp.where(sel1, before, 0.0), axis=-1, keepdims=True)
    carry_scr[...] = carry_scr[...] + jnp.sum(onehot, axis=0, keepdims=True)
    mi = jnp.where(lane == 0, i0, jnp.where(lane == 1, i1, 0))
    mi = jnp.where(lane == 2, rank0.astype(jnp.int32), jnp.where(lane == 3, rank1.astype(jnp.int32), mi))
    mi_ref[...] = mi
    mf_ref[...] = jnp.where(lane == 0, g0, jnp.where(lane == 1, g1, 0.0))
    cnt_ref[...] = carry_scr[...]


def _router(x2, o2, wo, g, router, tm=MOE_TM):
    T, D = x2.shape
    K = o2.shape[1]
    rt = router.astype(F32).T
    return pl.pallas_call(
        _router_kernel,
        grid=(T // tm,),
        in_specs=[pl.BlockSpec((tm, D), lambda i: (i, 0)),
                  pl.BlockSpec((tm, K), lambda i: (i, 0)),
                  pl.BlockSpec((K, D), lambda i: (0, 0)),
                  pl.BlockSpec((1, D), lambda i: (0, 0)),
                  pl.BlockSpec((N_EXPERTS, D), lambda i: (0, 0))],
        out_specs=[pl.BlockSpec((tm, D), lambda i: (i, 0)),
                   pl.BlockSpec((tm * ROW_TILE, LANES), lambda i: (i, 0)),
                   pl.BlockSpec((tm, LANES), lambda i: (i, 0)),
                   pl.BlockSpec((tm, LANES), lambda i: (i, 0)),
                   pl.BlockSpec((1, LANES), lambda i: (0, 0))],
        out_shape=[jax.ShapeDtypeStruct((T, D), F32),
                   jax.ShapeDtypeStruct((T * ROW_TILE, LANES), F32),
                   jax.ShapeDtypeStruct((T, LANES), jnp.int32),
                   jax.ShapeDtypeStruct((T, LANES), F32),
                   jax.ShapeDtypeStruct((1, LANES), F32)],
        scratch_shapes=[pltpu.VMEM((1, LANES), F32)],
        compiler_params=_cparams(("arbitrary",)),
        name="moe_router",
    )(x2, o2, wo, g.reshape(1, D), rt)


def _moe_ffn_kernel(be_ref, nu_ref, nv_ref, idx_ref, idxn_ref, h_hbm, w1_ref, w3_ref, w2_ref, out_hbm,
                    x_scr, y_scr, acc_scr, gsem, ssem):
    b = pl.program_id(0)
    f = pl.program_id(1)
    nf = pl.num_programs(1)
    nu = nu_ref[0]

    def rows(n):
        return pl.ds(pl.multiple_of(n * ROW_TILE, ROW_TILE), ROW_TILE)

    def gather(iref, count, par, wait):
        def body(n, carry):
            cp = pltpu.make_async_copy(h_hbm.at[rows(iref[n])], x_scr.at[par, rows(n)], gsem.at[par])
            cp.wait() if wait else cp.start()
            return carry

        lax.fori_loop(0, count, body, 0)

    def scatter(iref, count, par, wait):
        def body(n, carry):
            cp = pltpu.make_async_copy(y_scr.at[par, rows(n)], out_hbm.at[rows(iref[MOE_BLOCK + n])], ssem.at[par])
            cp.wait() if wait else cp.start()
            return carry

        lax.fori_loop(0, count, body, 0)

    def live_block(par):
        @pl.when(f == 0)
        def _():
            @pl.when(b == 0)
            def _():
                x_scr[...] = jnp.zeros(x_scr.shape, F32)
                gather(idx_ref, nv_ref[0], 0, False)

            gather(idx_ref, nv_ref[b], par, True)

            @pl.when(b + 1 < nu)
            def _():
                gather(idxn_ref, nv_ref[b + 1], 1 - par, False)

            acc_scr[...] = jnp.zeros(acc_scr.shape, F32)

        h = _tiles_to_rows(x_scr.at[par], MOE_BLOCK).astype(BF16)
        u = jnp.dot(h, w1_ref[...], preferred_element_type=F32)
        v = jnp.dot(h, w3_ref[...], preferred_element_type=F32)
        a = (u * jax.nn.sigmoid(u) * v).astype(BF16)
        acc_scr[...] += jnp.dot(a, w2_ref[...], preferred_element_type=F32)

        @pl.when(f == nf - 1)
        def _():
            @pl.when(b >= 2)
            def _():
                scatter(idx_ref, nv_ref[b - 2], par, True)

            _rows_to_tiles(y_scr.at[par], acc_scr[...])
            scatter(idx_ref, nv_ref[b], par, False)

            @pl.when(b == nu - 1)
            def _():
                scatter(idx_ref, nv_ref[b], par, True)

                @pl.when(b >= 1)
                def _():
                    scatter(idx_ref, nv_ref[b - 1], 1 - par, True)

    for par in range(2):
        @pl.when(jnp.logical_and(b < nu, b % 2 == par))
        def _(par=par):
            live_block(par)


def _moe_ffn(h2, idx, block_expert, n_used, n_valid, n_out_rows, w1, w3, w2, tf=MOE_TF):
    E, D, F = w1.shape
    nb = block_expert.shape[0]

    def last_live(b, nu):
        return jnp.maximum(jnp.minimum(b, nu[0] - 1), 0)

    def w_in_map(b, f, be, nu, nv):
        live = b < nu[0]
        return (be[last_live(b, nu)], 0, jnp.where(live, f, F // tf - 1))

    def w_out_map(b, f, be, nu, nv):
        live = b < nu[0]
        return (be[last_live(b, nu)], jnp.where(live, f, F // tf - 1), 0)

    return pl.pallas_call(
        _moe_ffn_kernel,
        grid_spec=pltpu.PrefetchScalarGridSpec(
            num_scalar_prefetch=3,
            grid=(nb, F // tf),
            in_specs=[pl.BlockSpec((2 * MOE_BLOCK,), lambda b, f, be, nu, nv: (last_live(b, nu),),
                                   memory_space=pltpu.SMEM),
                      pl.BlockSpec((2 * MOE_BLOCK,), lambda b, f, be, nu, nv: (last_live(b + 1, nu),),
                                   memory_space=pltpu.SMEM),
                      pl.BlockSpec(memory_space=pl.ANY),
                      pl.BlockSpec((None, D, tf), w_in_map),
                      pl.BlockSpec((None, D, tf), w_in_map),
                      pl.BlockSpec((None, tf, D), w_out_map)],
            out_specs=pl.BlockSpec(memory_space=pl.ANY),
            scratch_shapes=[pltpu.VMEM((2, MOE_BLOCK * ROW_TILE, LANES), F32),
                            pltpu.VMEM((2, MOE_BLOCK * ROW_TILE, LANES), F32),
                            pltpu.VMEM((MOE_BLOCK, D), F32),
                            pltpu.SemaphoreType.DMA((2,)), pltpu.SemaphoreType.DMA((2,))],
        ),
        out_shape=jax.ShapeDtypeStruct((n_out_rows * ROW_TILE, LANES), F32),
        compiler_params=_cparams(("arbitrary", "arbitrary")),
        name="moe_ffn",
    )(block_expert, n_used, n_valid, idx, idx, h2, w1, w3, w2)


def _combine_kernel(x_ref, mf_ref, g_ref, y0_ref, y1_ref, y_ref, *, tm, final_norm):
    mf = mf_ref[...]
    y = x_ref[...] + mf[:, 0:1] * _tiles_to_rows(y0_ref, tm) + mf[:, 1:2] * _tiles_to_rows(y1_ref, tm)
    if final_norm:
        ms = jnp.mean(y * y, axis=-1, keepdims=True)
        y = y * lax.rsqrt(ms + NORM_EPS) * g_ref[...]
    y_ref[...] = y


def _combine(x2, mf, yrows, g_final, final_norm, tm=MOE_TM):
    T, D = x2.shape
    nt = T // tm
    return pl.pallas_call(
        functools.partial(_combine_kernel, tm=tm, final_norm=final_norm),
        grid=(nt,),
        in_specs=[pl.BlockSpec((tm, D), lambda i: (i, 0)),
                  pl.BlockSpec((tm, LANES), lambda i: (i, 0)),
                  pl.BlockSpec((1, D), lambda i: (0, 0)),
                  pl.BlockSpec((tm * ROW_TILE, LANES), lambda i: (i, 0)),
                  pl.BlockSpec((tm * ROW_TILE, LANES), lambda i: (nt + i, 0))],
        out_specs=pl.BlockSpec((tm, D), lambda i: (i, 0)),
        out_shape=jax.ShapeDtypeStruct((T, D), F32),
        compiler_params=_cparams(("parallel",)),
        name="moe_combine",
    )(x2, mf, g_final.reshape(1, D), yrows, yrows)


def _moe_layer(x2, o2, wo, g, router, w1, w3, w2, g_final, final_norm):
    T, D = x2.shape
    x2, h2, mi, mf, cnt = _router(x2, o2, wo, g, router)
    counts = cnt[0, :N_EXPERTS].astype(jnp.int32)
    nblk = (counts + MOE_BLOCK - 1) // MOE_BLOCK
    bend = jnp.cumsum(nblk)
    bstart = bend - nblk
    slot = bstart[mi[:, 0:2]] * MOE_BLOCK + mi[:, 2:4]
    slot_flat = slot.reshape(-1).astype(jnp.int32)
    nb = -(-(2 * T + N_EXPERTS * (MOE_BLOCK - 1)) // MOE_BLOCK)
    blocks = jnp.arange(nb, dtype=jnp.int32)
    block_expert = jnp.minimum(jnp.sum(bend[None, :] <= blocks[:, None], axis=1),
                               N_EXPERTS - 1).astype(jnp.int32)
    n_used = bend[-1:].astype(jnp.int32)
    n_valid = jnp.clip(counts[block_expert] - (blocks - bstart[block_expert]) * MOE_BLOCK,
                       0, MOE_BLOCK).astype(jnp.int32)
    n_valid = jnp.where(blocks < n_used[0], n_valid, 0)
    tok = jnp.repeat(jnp.arange(T, dtype=jnp.int32), 2)
    dst = jnp.tile(jnp.arange(2, dtype=jnp.int32) * T, T) + tok
    src_by_slot = jnp.zeros((nb * MOE_BLOCK,), jnp.int32).at[slot_flat].set(tok)
    dst_by_slot = jnp.zeros((nb * MOE_BLOCK,), jnp.int32).at[slot_flat].set(dst)
    idx = jnp.concatenate([src_by_slot.reshape(nb, MOE_BLOCK), dst_by_slot.reshape(nb, MOE_BLOCK)],
                          axis=1).reshape(-1)
    yrows = _moe_ffn(h2, idx, block_expert, n_used, n_valid, 2 * T, w1, w3, w2)
    return _combine(x2, mf, yrows, g_final, final_norm)


def _mixer_a(x, g, w_in, w_out, tabs):
    B, S, D = x.shape
    w = w_in.astype(BF16)
    qkvs = []
    for gi, dil in enumerate(DILATIONS):
        cos, sin = tabs[dil]
        qkvs.append(_norm_proj(x, g, w, cos, sin, dil=dil, col_block=gi, ncols=3 * D, tc=D,
                               rope_chunks=_QKV_ROPE_CHUNKS))
    o = _attn_a(qkvs, B, S)
    return o.reshape(B * S, D), w_out.astype(BF16)


def _mixer_b(x, g, w_in, lq1, lk1, lq2, lk2, subln, w_out, lambda_init, tabs, side_casts=()):
    B, S, D = x.shape
    cos, sin = tabs[1]
    qkv = _norm_proj(x, g, w_in.astype(BF16), cos, sin, dil=1, col_block=0, ncols=3 * D, tc=D,
                     rope_chunks=_QKV_ROPE_CHUNKS).reshape(B, S, 3 * D)
    lam_params = jnp.stack([lq1, lk1, lq2, lk2]).astype(F32)
    o, casted = _attn_b(qkv, lam_params, subln.astype(F32), lambda_init, B, S, side_casts)
    return o.reshape(B * S, D), w_out.astype(BF16), casted


def _mixer_c(x, g, w_in, sink, w_out, tabs):
    B, S, D = x.shape
    cos, sin = tabs[1]
    head_order = jnp.arange(16).reshape(2, 8).T.reshape(-1)
    col_order = (head_order[:, None] * HEAD_DIM + jnp.arange(HEAD_DIM)[None, :]).reshape(-1)
    w_q = w_in[:, :D][:, col_order]
    w = jnp.concatenate([w_q, w_in[:, D:]], axis=1).astype(BF16)
    ncols = w.shape[1]
    chunks = tuple(c < (ncols // LANES - 1) for c in range(ncols // LANES))
    qkv = _norm_proj(x, g, w, cos, sin, dil=1, col_block=0, ncols=ncols, tc=ncols,
                     rope_chunks=chunks).reshape(B, S, ncols)
    sk = sink.astype(F32)
    sink_tab = jnp.concatenate([jnp.broadcast_to(sk[:8, None], (8, LANES)),
                                jnp.broadcast_to(sk[8:, None], (8, LANES))], axis=1)
    o = _attn_c(qkv, sink_tab, B, S)
    w_o = w_out[col_order, :].astype(BF16)
    return o.reshape(B * S, D), w_o


def kernel(x, positions, l0_norm_mix, l0_a_w_in, l0_a_w_out, l0_norm_ffn, l0_ffn_w1, l0_ffn_w3, l0_ffn_w2, l1_norm_mix, l1_b_w_in, l1_b_lambda_q1, l1_b_lambda_k1, l1_b_lambda_q2, l1_b_lambda_k2, l1_b_subln, l1_b_w_out, l1_norm_ffn, l1_moe_router, l1_moe_w1, l1_moe_w3, l1_moe_w2, l2_norm_mix, l2_c_w_in, l2_c_sink, l2_c_w_out, l2_norm_ffn, l2_ffn_w1, l2_ffn_w3, l2_ffn_w2, l3_norm_mix, l3_a_w_in, l3_a_w_out, l3_norm_ffn, l3_moe_router, l3_moe_w1, l3_moe_w3, l3_moe_w2, final_norm):
    B, S, D = x.shape
    T = B * S
    cos, sin = lax.optimization_barrier(_rope_tables(positions))
    tabs = {d: (_to_strided(cos, d), _to_strided(sin, d)) for d in DILATIONS}

    o, wo = _mixer_a(x, l0_norm_mix, l0_a_w_in, l0_a_w_out, tabs)
    x = _ffn_dense(x.reshape(T, D), o, wo, l0_norm_ffn, l0_ffn_w1.astype(BF16), l0_ffn_w3.astype(BF16),
                   l0_ffn_w2.astype(BF16)).reshape(B, S, D)
    lambda_init = 0.8 - 0.6 * math.exp(-0.3 * 1)
    moe_w = (l1_moe_w1, l1_moe_w3, l1_moe_w2, l3_moe_w1, l3_moe_w3, l3_moe_w2)
    o, wo, moe_bf = _mixer_b(x, l1_norm_mix, l1_b_w_in, l1_b_lambda_q1, l1_b_lambda_k1, l1_b_lambda_q2,
                             l1_b_lambda_k2, l1_b_subln, l1_b_w_out, lambda_init, tabs,
                             side_casts=tuple(w.reshape(-1, w.shape[-1]) for w in moe_w))
    moe_bf = [wb.reshape(w.shape) for wb, w in zip(moe_bf, moe_w)]
    x = _moe_layer(x.reshape(T, D), o, wo, l1_norm_ffn, l1_moe_router, moe_bf[0], moe_bf[1], moe_bf[2],
                   final_norm, False).reshape(B, S, D)
    o, wo = _mixer_c(x, l2_norm_mix, l2_c_w_in, l2_c_sink, l2_c_w_out, tabs)
    x = _ffn_dense(x.reshape(T, D), o, wo, l2_norm_ffn, l2_ffn_w1.astype(BF16), l2_ffn_w3.astype(BF16),
                   l2_ffn_w2.astype(BF16)).reshape(B, S, D)
    o, wo = _mixer_a(x, l3_norm_mix, l3_a_w_in, l3_a_w_out, tabs)
    x = _moe_layer(x.reshape(T, D), o, wo, l3_norm_ffn, l3_moe_router, moe_bf[3], moe_bf[4], moe_bf[5],
                   final_norm, True)
    return x.reshape(B, S, D)
```

```python
import functools
import math

import jax
import jax.numpy as jnp
from jax import lax
from jax.experimental import pallas as pl
from jax.experimental.pallas import tpu as pltpu

F32 = jnp.float32
BF16 = jnp.bfloat16

D_MODEL = 1024
HEAD_DIM = 64
ROT_DIM = HEAD_DIM // 4
ROPE_THETA = 500000.0
NORM_EPS = 1e-6
SUBLN_EPS = 1e-5
DILATIONS = (1, 4, 16)
A_HALF = 64
C_HALF = 128
N_EXPERTS = 8
_QKV_ROPE_CHUNKS = (True,) * 16 + (False,) * 8

LANES = 128
F32_SUBLANES = 8
BF16_SUBLANES = 16
VMEM_LIMIT = 56 * 1024 * 1024

PROJ_TM = 512
Q_SUB = LANES
ATTN_A_TB = 2048
ATTN_C_TB = 1024
ATTN_B_TQ = 1024
ATTN_B_TK = 2048
CK = 512
A_GROUP = 3
C_AHEAD = 4
FFN_TM = 512
MOE_TM = 512
MOE_BLOCK = 512
MOE_TF = 1792


def _cparams(sem):
    return pltpu.CompilerParams(dimension_semantics=sem, vmem_limit_bytes=VMEM_LIMIT)


def _rope_tables(positions):
    lane = jnp.arange(LANES) % HEAD_DIM
    half = ROT_DIM // 2
    inv_freq = ROPE_THETA ** (-((2 * (lane % half)).astype(F32) / ROT_DIM))
    freq = jnp.where(lane < ROT_DIM, inv_freq, 0.0)
    sign = jnp.where(lane < half, -1.0, jnp.where(lane < ROT_DIM, 1.0, 0.0))
    ang = positions.astype(F32)[..., None] * freq
    return jnp.cos(ang), jnp.sin(ang) * sign


def _to_strided(t, d):
    B, S, C = t.shape
    return jnp.swapaxes(t.reshape(B, S // d, d, C), 1, 2)


def _rope_chunk(x, cos, sin, first_half):
    partner = jnp.where(first_half, pltpu.roll(x, LANES - ROT_DIM // 2, 1),
                        pltpu.roll(x, ROT_DIM // 2, 1))
    return x * cos + partner * sin


def _proj_kernel(x_ref, g_ref, w_ref, cos_ref, sin_ref, o_ref, *, rope_chunks, tc):
    ncols = w_ref.shape[1]
    x = x_ref[...]
    ms = jnp.mean(x * x, axis=-1, keepdims=True)
    h = (x * lax.rsqrt(ms + NORM_EPS) * g_ref[...]).astype(BF16)
    lane = lax.broadcasted_iota(jnp.int32, (1, LANES), 1)
    first_half = (lane % HEAD_DIM) < (ROT_DIM // 2)
    cos = cos_ref[...]
    sin = sin_ref[...]
    for c0 in range(0, ncols, tc):
        res = jnp.dot(h, w_ref[:, c0:c0 + tc], preferred_element_type=F32)
        parts = []
        for cc in range(tc // LANES):
            xc = res[:, cc * LANES:(cc + 1) * LANES]
            if rope_chunks[c0 // LANES + cc]:
                xc = _rope_chunk(xc, cos, sin, first_half)
            parts.append(xc.astype(o_ref.dtype))
        o_ref[:, c0:c0 + tc] = jnp.concatenate(parts, axis=1)


def _norm_proj(x, g, w, cos, sin, *, tc, rope_chunks, tm=PROJ_TM):
    B, S, D = x.shape
    ncols = w.shape[1]
    return pl.pallas_call(
        functools.partial(_proj_kernel, rope_chunks=rope_chunks, tc=tc),
        grid=(B, S // tm),
        in_specs=[
            pl.BlockSpec((None, tm, D), lambda b, i: (b, i, 0)),
            pl.BlockSpec((1, D), lambda b, i: (0, 0)),
            pl.BlockSpec((D, ncols), lambda b, i: (0, 0)),
            pl.BlockSpec((None, None, tm, LANES), lambda b, i: (b, 0, i, 0)),
            pl.BlockSpec((None, None, tm, LANES), lambda b, i: (b, 0, i, 0)),
        ],
        out_specs=pl.BlockSpec((None, tm, ncols), lambda b, i: (b, i, 0)),
        out_shape=jax.ShapeDtypeStruct((B, S, ncols), BF16),
        compiler_params=_cparams(("parallel", "parallel")),
        name="norm_proj",
    )(x, g.reshape(1, D), w, cos, sin)


def _proj_a_kernel(x_ref, g_ref, w_ref, c1, s1, c4, s4, c16, s16, p4, p16, o1, o4, o16):
    tm = x_ref.shape[0]
    x = x_ref[...]
    ms = jnp.mean(x * x, axis=-1, keepdims=True)
    h1 = (x * lax.rsqrt(ms + NORM_EPS) * g_ref[...]).astype(BF16)
    lane = lax.broadcasted_iota(jnp.int32, (1, LANES), 1)
    first_half = (lane % HEAD_DIM) < (ROT_DIM // 2)
    ncols = o1.shape[-1]
    for gi, (dil, cos_ref, sin_ref, perm_ref, o_ref) in enumerate(
            ((1, c1, s1, None, o1), (4, c4, s4, p4, o4), (16, c16, s16, p16, o16))):
        n = tm // dil
        h = h1 if perm_ref is None else jnp.dot(perm_ref[...], h1, preferred_element_type=F32).astype(BF16)
        cos = cos_ref[...].reshape(tm, LANES)
        sin = sin_ref[...].reshape(tm, LANES)
        for c0 in range(0, ncols, D_MODEL):
            res = jnp.dot(h, w_ref[:, gi * ncols + c0:gi * ncols + c0 + D_MODEL], preferred_element_type=F32)
            parts = []
            for cc in range(D_MODEL // LANES):
                xc = res[:, cc * LANES:(cc + 1) * LANES]
                if _QKV_ROPE_CHUNKS[c0 // LANES + cc]:
                    xc = _rope_chunk(xc, cos, sin, first_half)
                parts.append(xc.astype(o_ref.dtype))
            val = jnp.concatenate(parts, axis=1)
            for r in range(dil):
                o_ref[r, :, c0:c0 + D_MODEL] = val[r * n:(r + 1) * n]


def _norm_proj_a(x, g, w, tabs, tm=PROJ_TM):
    B, S, D = x.shape
    ncols = 3 * D
    in_specs = [pl.BlockSpec((None, tm, D), lambda b, i: (b, i, 0)),
                pl.BlockSpec((1, D), lambda b, i: (0, 0)),
                pl.BlockSpec((D, 3 * ncols), lambda b, i: (0, 0), pipeline_mode=pl.Buffered(1))]
    args = [x, g.reshape(1, D), w]
    for dil in DILATIONS:
        n = tm // dil
        for tab in tabs[dil]:
            in_specs.append(pl.BlockSpec((None, dil, n, LANES), lambda b, i: (b, 0, i, 0)))
            args.append(tab)
    for dil in DILATIONS[1:]:
        n = tm // dil
        p = jnp.arange(tm)
        src = (p % n) * dil + p // n
        args.append((src[:, None] == jnp.arange(tm)[None, :]).astype(BF16))
        in_specs.append(pl.BlockSpec((tm, tm), lambda b, i: (0, 0)))
    return pl.pallas_call(
        _proj_a_kernel,
        grid=(B, S // tm),
        in_specs=in_specs,
        out_specs=[pl.BlockSpec((None, dil, tm // dil, ncols), lambda b, i: (b, 0, i, 0)) for dil in DILATIONS],
        out_shape=[jax.ShapeDtypeStruct((B, dil, S // dil, ncols), BF16) for dil in DILATIONS],
        compiler_params=_cparams(("parallel", "parallel")),
        name="norm_proj_a",
    )(*args)


def _fill_band_bias(bias_scr, tq, W, half):
    kk = lax.broadcasted_iota(jnp.int32, (W, 2 * tq), 0)
    qq = lax.broadcasted_iota(jnp.int32, (W, 2 * tq), 1) % tq
    for i in range(3):
        ok = jnp.abs(qq + i * half - kk) <= half
        bias_scr[i] = jnp.where(ok, 0.0, -jnp.inf).astype(F32)


def _band_scores(q2, kw, bias):
    lane_lo = lax.broadcasted_iota(jnp.int32, (1, LANES), 1) < HEAD_DIM
    zero = jnp.zeros_like(q2)
    qq = jnp.concatenate([jnp.where(lane_lo, q2, zero), jnp.where(lane_lo, zero, q2)], axis=0)
    qq = qq * jnp.asarray(HEAD_DIM ** -0.5, q2.dtype)
    return lax.dot_general(kw, qq, (((1,), (1,)), ((), ())), preferred_element_type=F32) + bias


def _band_finish(st, vw, want_lse, sink_row=None):
    tq = st.shape[1] // 2
    row_lo = lax.broadcasted_iota(jnp.int32, (LANES, 1), 0) < HEAD_DIM
    m = jnp.max(st, axis=0, keepdims=True)
    if sink_row is not None:
        m = jnp.maximum(m, sink_row)
    p = jnp.exp(st - m)
    l = jnp.sum(p, axis=0, keepdims=True)
    if sink_row is not None:
        l = l + jnp.exp(sink_row - m)
    ot = lax.dot_general(vw, p.astype(BF16), (((0,), (0,)), ((), ())), preferred_element_type=F32) / l
    o = jnp.where(row_lo, ot[:, :tq], ot[:, tq:]).T
    if not want_lse:
        return o, None
    lse_row = m + jnp.log(l)
    lse = jnp.where(row_lo, jnp.broadcast_to(lse_row[:, :tq], (LANES, tq)),
                    jnp.broadcast_to(lse_row[:, tq:], (LANES, tq))).T
    return o, lse


def _attn_a_kernel(*refs, TB, S):
    qkv = refs[:9]
    o_ref = refs[9]
    scr = refs[10:16]
    bias_scr = refs[16]
    t = pl.program_id(2)
    W = 4 * A_HALF

    @pl.when(t == 0)
    def _():
        _fill_band_bias(bias_scr, Q_SUB, W, A_HALF)

    subs = []
    for g, dil in enumerate(DILATIONS):
        nq = TB // dil
        tq = min(nq, Q_SUB)
        for r in range(dil):
            for jj in range(nq // tq):
                subs.append((g, dil, nq, tq, r, jj))

    def scores(sub):
        g, dil, nq, tq, r, jj = sub
        q_ref, k_ref = qkv[3 * g], qkv[3 * g + 1]
        L = S // dil
        qs = t * nq + jj * tq
        ws = pl.multiple_of(jnp.clip(qs - A_HALF, 0, L - W), A_HALF)
        q2 = q_ref[r, jj * tq:(jj + 1) * tq, :]
        return _band_scores(q2, k_ref[r, pl.ds(ws, W), :], bias_scr[(qs - ws) // A_HALF]), ws

    def finish(sub, st, ws):
        g, dil, nq, tq, r, jj = sub
        o_scr, l_scr = scr[2 * g], scr[2 * g + 1]
        o, lse = _band_finish(st, qkv[3 * g + 2][r, pl.ds(ws, W), :], True)
        row0 = jj * (tq * dil) + r
        if dil == 1:
            o_scr[row0:row0 + tq, :] = o
            l_scr[row0:row0 + tq, :] = lse
        else:
            o_scr[pl.ds(row0, tq, stride=dil), :] = o
            l_scr[pl.ds(row0, tq, stride=dil), :] = lse

    pend = [scores(sub) for sub in subs[:A_GROUP]]
    for n, sub in enumerate(subs):
        if n + A_GROUP < len(subs):
            pend.append(scores(subs[n + A_GROUP]))
        st, ws = pend[n]
        finish(sub, st, ws)

    l0, l1, l2 = scr[1][...], scr[3][...], scr[5][...]
    mx = jnp.maximum(jnp.maximum(l0, l1), l2)
    e0, e1, e2 = jnp.exp(l0 - mx), jnp.exp(l1 - mx), jnp.exp(l2 - mx)
    den = e0 + e1 + e2
    o = (e0 * scr[0][...] + e1 * scr[2][...] + e2 * scr[4][...]) / den
    o_ref[...] = o.astype(o_ref.dtype)


def _attn_a(qkvs, B, S, TB=ATTN_A_TB):
    n_pairs = (16 * HEAD_DIM) // LANES
    in_specs, args = [], []
    for g, dil in enumerate(DILATIONS):
        L = S // dil
        in_specs += [
            pl.BlockSpec((None, dil, TB // dil, LANES), lambda b, hp, t: (b, 0, t, hp)),
            pl.BlockSpec((None, dil, L, LANES), lambda b, hp, t: (b, 0, 0, n_pairs + hp)),
            pl.BlockSpec((None, dil, L, LANES), lambda b, hp, t: (b, 0, 0, 2 * n_pairs + hp)),
        ]
        args += [qkvs[g]] * 3
    return pl.pallas_call(
        functools.partial(_attn_a_kernel, TB=TB, S=S),
        grid=(B, n_pairs, S // TB),
        in_specs=in_specs,
        out_specs=pl.BlockSpec((None, TB, LANES), lambda b, hp, t: (b, t, hp)),
        out_shape=jax.ShapeDtypeStruct((B, S, n_pairs * LANES), BF16),
        scratch_shapes=[pltpu.VMEM((TB, LANES), F32)] * 6
        + [pltpu.VMEM((3, Q_SUB + 2 * A_HALF, 2 * Q_SUB), F32)],
        compiler_params=_cparams(("parallel", "parallel", "arbitrary")),
        name="attn_dilated",
    )(*args)


def _attn_c_kernel(q_ref, k_ref, v_ref, sink_ref, o_ref, bias_scr, *, TB, S):
    t = pl.program_id(1)
    tq = Q_SUB
    W = tq + 2 * C_HALF
    n_pairs = q_ref.shape[1] // LANES

    @pl.when(t == 0)
    def _():
        _fill_band_bias(bias_scr, tq, W, C_HALF)

    units = [(jj, p) for jj in range(TB // tq) for p in range(n_pairs)]

    def window(jj):
        qs = t * TB + jj * tq
        ws = pl.multiple_of(jnp.clip(qs - C_HALF, 0, S - W), C_HALF)
        return qs, ws

    def scores(unit):
        jj, p = unit
        qs, ws = window(jj)
        q2 = q_ref[jj * tq:(jj + 1) * tq, p * LANES:(p + 1) * LANES]
        return _band_scores(q2, k_ref[pl.ds(ws, W), :], bias_scr[(qs - ws) // C_HALF])

    pend = [scores(u) for u in units[:C_AHEAD]]
    for n, (jj, p) in enumerate(units):
        if n + C_AHEAD < len(units):
            pend.append(scores(units[n + C_AHEAD]))
        _, ws = window(jj)
        sink_row = sink_ref[p:p + 1, :]
        o, _ = _band_finish(pend[n], v_ref[pl.ds(ws, W), :], False, sink_row=sink_row)
        pend[n] = None
        o_ref[jj * tq:(jj + 1) * tq, p * LANES:(p + 1) * LANES] = o.astype(o_ref.dtype)


def _attn_c(qkv, sink_tab, B, S, TB=ATTN_C_TB):
    nq = 16 * HEAD_DIM
    return pl.pallas_call(
        functools.partial(_attn_c_kernel, TB=TB, S=S),
        grid=(B, S // TB),
        in_specs=[
            pl.BlockSpec((None, TB, nq), lambda b, t: (b, t, 0)),
            pl.BlockSpec((None, S, LANES), lambda b, t: (b, 0, nq // LANES)),
            pl.BlockSpec((None, S, LANES), lambda b, t: (b, 0, nq // LANES + 1)),
            pl.BlockSpec((8, 2 * LANES), lambda b, t: (0, 0)),
        ],
        out_specs=pl.BlockSpec((None, TB, nq), lambda b, t: (b, t, 0)),
        out_shape=jax.ShapeDtypeStruct((B, S, nq), BF16),
        scratch_shapes=[pltpu.VMEM((3, Q_SUB + 2 * C_HALF, 2 * Q_SUB), F32)],
        compiler_params=_cparams(("parallel", "arbitrary")),
        name="attn_swa_sink",
    )(qkv, qkv, qkv, sink_tab)


def _attn_b_kernel(*refs, lambda_init, n_cast):
    lam_ref, q_ref, qn_ref, k_ref, kn_ref, v_ref, subln_ref = refs[:7]
    cast_in = refs[7:7 + n_cast]
    o_ref = refs[7 + n_cast]
    cast_out = refs[8 + n_cast:8 + 2 * n_cast]
    s0_scr, s1_scr, m_scr, l_scr, acc_scr = refs[8 + 2 * n_cast:]
    _attn_b_body(lam_ref, q_ref, qn_ref, k_ref, kn_ref, v_ref, subln_ref, o_ref,
                 s0_scr, s1_scr, m_scr, l_scr, acc_scr, lambda_init=lambda_init)
    for src_ref, dst_ref in zip(cast_in, cast_out):
        dst_ref[...] = src_ref[...].astype(dst_ref.dtype)


def _attn_b_body(lam_ref, q_ref, qn_ref, k_ref, kn_ref, v_ref, subln_ref, o_ref,
                 s0_scr, s1_scr, m_scr, l_scr, acc_scr, *, lambda_init):
    i = pl.program_id(2)
    kv = pl.program_id(3)
    nkv = pl.num_programs(3)
    tk = k_ref.shape[0]
    lane = lax.broadcasted_iota(jnp.int32, (1, LANES), 1)
    lane_lo = lane < HEAD_DIM

    def scores(kref, qref, c, j):
        k = kref[j * CK:(j + 1) * CK, :] * jnp.asarray(HEAD_DIM ** -0.5, kref.dtype)
        sel = lane_lo if c == 0 else jnp.logical_not(lane_lo)
        kc = jnp.where(sel, k, jnp.zeros_like(k))
        return lax.dot_general(kc, qref[...], (((1,), (1,)), ((), ())), preferred_element_type=F32)

    @pl.when(kv == 0)
    def _():
        m_scr[...] = jnp.full(m_scr.shape, -jnp.inf, F32)
        l_scr[...] = jnp.zeros(l_scr.shape, F32)
        acc_scr[...] = jnp.zeros(acc_scr.shape, F32)

    @pl.when(jnp.logical_and(i == 0, kv == 0))
    def _():
        for c in range(2):
            for j in range(tk // CK):
                s0_scr[c * tk + j * CK:c * tk + (j + 1) * CK, :] = scores(k_ref, q_ref, c, j)

    def step(cur_scr, nxt_scr):
        vt = jnp.concatenate([v_ref[...].T, jnp.ones((BF16_SUBLANES, tk), BF16)], axis=0)
        for c in range(2):
            m, l, acc = m_scr[c], l_scr[c], acc_scr[c]
            for j in range(tk // CK):
                rows = slice(c * tk + j * CK, c * tk + (j + 1) * CK)
                nxt_scr[rows, :] = scores(kn_ref, qn_ref, c, j)
                s = cur_scr[rows, :]
                m_new = jnp.maximum(m, jnp.max(s, axis=0, keepdims=True))
                alpha = jnp.exp(m - m_new)
                p = jnp.exp(s - m_new).astype(BF16)
                pv = jnp.dot(vt[:, j * CK:(j + 1) * CK], p, preferred_element_type=F32)
                l = alpha * l + pv[LANES:LANES + 1]
                acc = alpha * acc + pv[:LANES]
                m = m_new
            m_scr[c], l_scr[c], acc_scr[c] = m, l, acc

    @pl.when(kv % 2 == 0)
    def _():
        step(s0_scr, s1_scr)

    @pl.when(kv % 2 == 1)
    def _():
        step(s1_scr, s0_scr)

    @pl.when(kv == nkv - 1)
    def _():
        lp = lam_ref[...]
        lam = (jnp.exp(jnp.sum(lp[0:1] * lp[1:2], axis=-1, keepdims=True))
               - jnp.exp(jnp.sum(lp[2:3] * lp[3:4], axis=-1, keepdims=True)) + lambda_init)
        o = acc_scr[0] / l_scr[0] - lam * (acc_scr[1] / l_scr[1])
        ms = jnp.mean(o * o, axis=0, keepdims=True)
        o = o * lax.rsqrt(ms + SUBLN_EPS) * subln_ref[...] * (1.0 - lambda_init)
        o_ref[...] = o.T.astype(o_ref.dtype)


def _attn_b(qkv, lam_params, subln, lambda_init, B, S, side_casts=(), tq=ATTN_B_TQ, tk=ATTN_B_TK):
    H = 8
    nq, nkv = S // tq, S // tk
    assert nkv % 2 == 0
    nsteps = B * H * nq * nkv
    cast_specs, cast_shapes = [], []
    for w in side_casts:
        rows, cols = w.shape
        nblk = nsteps
        while rows % nblk or (rows // nblk) % BF16_SUBLANES:
            nblk //= 2
        per = nsteps // nblk

        def w_map(b, h, i, kv, per=per):
            return ((((b * H + h) * nq + i) * nkv + kv) // per, 0)

        cast_specs.append(pl.BlockSpec((rows // nblk, cols), w_map))
        cast_shapes.append(jax.ShapeDtypeStruct((rows, cols), BF16))

    def q_next(b, h, i, kv):
        return (b, jnp.minimum(i + (kv + 1) // nkv, nq - 1), h)

    def k_next(b, h, i, kv):
        return (b, (kv + 1) % nkv, H + h)

    outs = pl.pallas_call(
        functools.partial(_attn_b_kernel, lambda_init=lambda_init, n_cast=len(side_casts)),
        grid=(B, H, nq, nkv),
        in_specs=[
            pl.BlockSpec((4, HEAD_DIM), lambda b, h, i, kv: (0, 0)),
            pl.BlockSpec((None, tq, LANES), lambda b, h, i, kv: (b, i, h)),
            pl.BlockSpec((None, tq, LANES), q_next),
            pl.BlockSpec((None, tk, LANES), lambda b, h, i, kv: (b, kv, H + h)),
            pl.BlockSpec((None, tk, LANES), k_next),
            pl.BlockSpec((None, tk, LANES), lambda b, h, i, kv: (b, kv, 2 * H + h)),
            pl.BlockSpec((LANES, 1), lambda b, h, i, kv: (0, 0)),
        ] + cast_specs,
        out_specs=[pl.BlockSpec((None, tq, LANES), lambda b, h, i, kv: (b, i, h))] + cast_specs,
        out_shape=[jax.ShapeDtypeStruct((B, S, H * LANES), BF16)] + cast_shapes,
        scratch_shapes=[pltpu.VMEM((2 * tk, tq), F32), pltpu.VMEM((2 * tk, tq), F32),
                        pltpu.VMEM((2, 1, tq), F32), pltpu.VMEM((2, 1, tq), F32),
                        pltpu.VMEM((2, LANES, tq), F32)],
        compiler_params=_cparams(("parallel", "parallel", "arbitrary", "arbitrary")),
        name="attn_diff",
    )(lam_params, qkv, qkv, qkv, qkv, qkv, subln.reshape(LANES, 1), *side_casts)
    return outs[0], outs[1:]


def _ffn_kernel(x_ref, o_ref, wo_ref, g_ref, w1_ref, w3_ref, w2_ref, y_ref):
    x = x_ref[...] + jnp.dot(o_ref[...], wo_ref[...], preferred_element_type=F32)
    ms = jnp.mean(x * x, axis=-1, keepdims=True)
    h = (x * lax.rsqrt(ms + NORM_EPS) * g_ref[...]).astype(BF16)
    u = jnp.dot(h, w1_ref[...], preferred_element_type=F32)
    v = jnp.dot(h, w3_ref[...], preferred_element_type=F32)
    a = (u * jax.nn.sigmoid(u) * v).astype(BF16)
    y_ref[...] = x + jnp.dot(a, w2_ref[...], preferred_element_type=F32)


def _ffn_dense(x2, o2, wo, g, w1, w3, w2, tm=FFN_TM):
    T, D = x2.shape
    K = o2.shape[1]
    F = w1.shape[1]
    return pl.pallas_call(
        _ffn_kernel,
        grid=(T // tm,),
        in_specs=[pl.BlockSpec((tm, D), lambda i: (i, 0)),
                  pl.BlockSpec((tm, K), lambda i: (i, 0)),
                  pl.BlockSpec((K, D), lambda i: (0, 0)),
                  pl.BlockSpec((1, D), lambda i: (0, 0)),
                  pl.BlockSpec((D, F), lambda i: (0, 0)),
                  pl.BlockSpec((D, F), lambda i: (0, 0)),
                  pl.BlockSpec((F, D), lambda i: (0, 0))],
        out_specs=pl.BlockSpec((tm, D), lambda i: (i, 0)),
        out_shape=jax.ShapeDtypeStruct((T, D), F32),
        compiler_params=_cparams(("parallel",)),
        name="ffn_dense",
    )(x2, o2, wo, g.reshape(1, D), w1, w3, w2)


ROW_TILE = F32_SUBLANES


def _rows_to_tiles(ref, val):
    n = val.shape[0]
    for a in range(ROW_TILE):
        ref[pl.ds(a, n, stride=ROW_TILE), :] = val[:, a * LANES:(a + 1) * LANES]


def _tiles_to_rows(ref, n):
    return jnp.concatenate([ref[pl.ds(a, n, stride=ROW_TILE), :] for a in range(ROW_TILE)], axis=1)


def _router_kernel(x_ref, o_ref, wo_ref, g_ref, r_ref, x1_ref, h_ref, mi_ref, mf_ref, cnt_ref, carry_scr):
    i = pl.program_id(0)
    tm = x_ref.shape[0]

    @pl.when(i == 0)
    def _():
        carry_scr[...] = jnp.zeros(carry_scr.shape, F32)

    x = x_ref[...] + jnp.dot(o_ref[...], wo_ref[...], preferred_element_type=F32)
    x1_ref[...] = x
    ms = jnp.mean(x * x, axis=-1, keepdims=True)
    h = x * lax.rsqrt(ms + NORM_EPS) * g_ref[...]
    _rows_to_tiles(h_ref, h)
    lane = lax.broadcasted_iota(jnp.int32, (tm, LANES), 1)
    logits = jnp.full((tm, LANES), -jnp.inf, F32)
    for e in range(N_EXPERTS):
        col = jnp.sum(h * r_ref[e:e + 1, :], axis=-1, keepdims=True)
        logits = jnp.where(lane == e, col, logits)
    v0 = jnp.max(logits, axis=-1, keepdims=True)
    i0 = jnp.min(jnp.where(logits == v0, lane, LANES), axis=-1, keepdims=True)
    rest = jnp.where(lane == i0, -jnp.inf, logits)
    v1 = jnp.max(rest, axis=-1, keepdims=True)
    i1 = jnp.min(jnp.where(rest == v1, lane, LANES), axis=-1, keepdims=True)
    tt = jnp.exp(v1 - v0)
    g0 = 1.0 / (1.0 + tt)
    g1 = tt / (1.0 + tt)
    sel0 = lane == i0
    sel1 = lane == i1
    onehot = jnp.where(jnp.logical_or(sel0, sel1), 1.0, 0.0)
    row = lax.broadcasted_iota(jnp.int32, (tm, tm), 0)
    col = lax.broadcasted_iota(jnp.int32, (tm, tm), 1)
    tri = jnp.where(row > col, 1.0, 0.0).astype(BF16)
    before = carry_scr[...] + jnp.dot(tri, onehot.astype(BF16), preferred_element_type=F32)
    rank0 = jnp.sum(jnp.where(sel0, before, 0.0), axis=-1, keepdims=True)
    rank1 = jnp.sum(jnp.where(sel1, before, 0.0), axis=-1, keepdims=True)
    carry_scr[...] = carry_scr[...] + jnp.sum(onehot, axis=0, keepdims=True)
    mi = jnp.where(lane == 0, i0, jnp.where(lane == 1, i1, 0))
    mi = jnp.where(lane == 2, rank0.astype(jnp.int32), jnp.where(lane == 3, rank1.astype(jnp.int32), mi))
    mi_ref[...] = mi
    mf_ref[...] = jnp.where(lane == 0, g0, jnp.where(lane == 1, g1, 0.0))
    cnt_ref[...] = carry_scr[...]


def _router(x2, o2, wo, g, router, tm=MOE_TM):
    T, D = x2.shape
    K = o2.shape[1]
    rt = router.astype(F32).T
    return pl.pallas_call(
        _router_kernel,
        grid=(T // tm,),
        in_specs=[pl.BlockSpec((tm, D), lambda i: (i, 0)),
                  pl.BlockSpec((tm, K), lambda i: (i, 0)),
                  pl.BlockSpec((K, D), lambda i: (0, 0)),
                  pl.BlockSpec((1, D), lambda i: (0, 0)),
                  pl.BlockSpec((N_EXPERTS, D), lambda i: (0, 0))],
        out_specs=[pl.BlockSpec((tm, D), lambda i: (i, 0)),
                   pl.BlockSpec((tm * ROW_TILE, LANES), lambda i: (i, 0)),
                   pl.BlockSpec((tm, LANES), lambda i: (i, 0)),
                   pl.BlockSpec((tm, LANES), lambda i: (i, 0)),
                   pl.BlockSpec((1, LANES), lambda i: (0, 0))],
        out_shape=[jax.ShapeDtypeStruct((T, D), F32),
                   jax.ShapeDtypeStruct((T * ROW_TILE, LANES), F32),
                   jax.ShapeDtypeStruct((T, LANES), jnp.int32),
                   jax.ShapeDtypeStruct((T, LANES), F32),
                   jax.ShapeDtypeStruct((1, LANES), F32)],
        scratch_shapes=[pltpu.VMEM((1, LANES), F32)],
        compiler_params=_cparams(("arbitrary",)),
        name="moe_router",
    )(x2, o2, wo, g.reshape(1, D), rt)


def _dispatch_kernel(dest_ref, h_ref, buf_in, buf_hbm, sem, *, tm):
    del buf_in

    def copy(n, slot):
        return pltpu.make_async_copy(h_ref.at[pl.ds(pl.multiple_of(n * ROW_TILE, ROW_TILE), ROW_TILE)],
                                     buf_hbm.at[pl.ds(pl.multiple_of(slot * ROW_TILE, ROW_TILE), ROW_TILE)], sem)

    def issue(n, carry):
        copy(n, dest_ref[2 * n]).start(priority=0)
        copy(n, dest_ref[2 * n + 1]).start(priority=1)
        return carry

    lax.fori_loop(0, tm, issue, 0)

    def drain(n, carry):
        copy(0, 0).wait()
        copy(0, 0).wait()
        return carry

    lax.fori_loop(0, tm, drain, 0)


def _dispatch(h2, dest_flat, n_rows, buf0, tm=MOE_TM):
    T = h2.shape[0] // ROW_TILE
    return pl.pallas_call(
        functools.partial(_dispatch_kernel, tm=tm),
        grid=(T // tm,),
        in_specs=[pl.BlockSpec((2 * tm,), lambda i: (i,), memory_space=pltpu.SMEM),
                  pl.BlockSpec((tm * ROW_TILE, LANES), lambda i: (i, 0)),
                  pl.BlockSpec(memory_space=pl.ANY)],
        out_specs=pl.BlockSpec(memory_space=pl.ANY),
        out_shape=jax.ShapeDtypeStruct((n_rows * ROW_TILE, LANES), F32),
        scratch_shapes=[pltpu.SemaphoreType.DMA(())],
        input_output_aliases={2: 0},
        compiler_params=_cparams(("arbitrary",)),
        name="moe_dispatch",
    )(dest_flat, h2, buf0)


def _moe_ffn_kernel(be_ref, nu_ref, x_ref, w1_ref, w3_ref, w2_ref, y_ref, acc_scr):
    b = pl.program_id(0)
    f = pl.program_id(1)

    @pl.when(b < nu_ref[0])
    def _():
        @pl.when(f == 0)
        def _():
            acc_scr[...] = jnp.zeros(acc_scr.shape, F32)

        h = _tiles_to_rows(x_ref, MOE_BLOCK).astype(BF16)
        u = jnp.dot(h, w1_ref[...], preferred_element_type=F32)
        v = jnp.dot(h, w3_ref[...], preferred_element_type=F32)
        a = (u * jax.nn.sigmoid(u) * v).astype(BF16)
        acc_scr[...] += jnp.dot(a, w2_ref[...], preferred_element_type=F32)

        @pl.when(f == pl.num_programs(1) - 1)
        def _():
            _rows_to_tiles(y_ref, acc_scr[...])

    @pl.when(jnp.logical_and(b >= nu_ref[0], f == 0))
    def _():
        y_ref[...] = jnp.zeros(y_ref.shape, F32)


def _moe_ffn(buf, block_expert, n_used, w1, w3, w2, tf=MOE_TF):
    R = buf.shape[0] // ROW_TILE
    E, D, F = w1.shape
    nb = R // MOE_BLOCK

    def last_live(b, nu):
        return jnp.maximum(jnp.minimum(b, nu[0] - 1), 0)

    def row_map(b, f, be, nu):
        return (last_live(b, nu), 0)

    def w_in_map(b, f, be, nu):
        live = b < nu[0]
        return (be[last_live(b, nu)], 0, jnp.where(live, f, F // tf - 1))

    def w_out_map(b, f, be, nu):
        live = b < nu[0]
        return (be[last_live(b, nu)], jnp.where(live, f, F // tf - 1), 0)

    return pl.pallas_call(
        _moe_ffn_kernel,
        grid_spec=pltpu.PrefetchScalarGridSpec(
            num_scalar_prefetch=2,
            grid=(nb, F // tf),
            in_specs=[pl.BlockSpec((MOE_BLOCK * ROW_TILE, LANES), row_map),
                      pl.BlockSpec((None, D, tf), w_in_map),
                      pl.BlockSpec((None, D, tf), w_in_map),
                      pl.BlockSpec((None, tf, D), w_out_map)],
            out_specs=pl.BlockSpec((MOE_BLOCK * ROW_TILE, LANES), lambda b, f, be, nu: (b, 0)),
            scratch_shapes=[pltpu.VMEM((MOE_BLOCK, D), F32)],
        ),
        out_shape=jax.ShapeDtypeStruct((R * ROW_TILE, LANES), F32),
        compiler_params=_cparams(("arbitrary", "arbitrary")),
        name="moe_ffn",
    )(block_expert, n_used, buf, w1, w3, w2)


def _combine_kernel(dest_ref, dnext_ref, x_ref, mf_ref, g_ref, ybuf_hbm, y_ref, rows_scr, sems, *, tm, final_norm):
    i = pl.program_id(0)
    last = pl.num_programs(0) - 1

    def copy(slot, par, k, n):
        return pltpu.make_async_copy(ybuf_hbm.at[pl.ds(pl.multiple_of(slot * ROW_TILE, ROW_TILE), ROW_TILE)],
                                     rows_scr.at[par, k, pl.ds(pl.multiple_of(n * ROW_TILE, ROW_TILE), ROW_TILE)],
                                     sems.at[par])

    def issue_tile(idx_ref, par):
        def issue(n, carry):
            copy(idx_ref[2 * n], par, 0, n).start(priority=0)
            copy(idx_ref[2 * n + 1], par, 1, n).start(priority=1)
            return carry

        lax.fori_loop(0, tm, issue, 0)

    def wait_tile(par):
        def drain(n, carry):
            copy(0, par, 0, 0).wait()
            copy(0, par, 1, 0).wait()
            return carry

        lax.fori_loop(0, tm, drain, 0)

    def finish(par):
        wait_tile(par)
        mf = mf_ref[...]
        y = (x_ref[...] + mf[:, 0:1] * _tiles_to_rows(rows_scr.at[par, 0], tm)
             + mf[:, 1:2] * _tiles_to_rows(rows_scr.at[par, 1], tm))
        if final_norm:
            ms = jnp.mean(y * y, axis=-1, keepdims=True)
            y = y * lax.rsqrt(ms + NORM_EPS) * g_ref[...]
        y_ref[...] = y

    @pl.when(i == 0)
    def _():
        issue_tile(dest_ref, 0)

    for par in range(2):
        @pl.when(jnp.logical_and(i % 2 == par, i < last))
        def _(par=par):
            issue_tile(dnext_ref, 1 - par)

        @pl.when(i % 2 == par)
        def _(par=par):
            finish(par)


def _combine(x2, mf, dest_flat, ybuf, g_final, final_norm, tm=MOE_TM):
    T, D = x2.shape
    nt = T // tm
    return pl.pallas_call(
        functools.partial(_combine_kernel, tm=tm, final_norm=final_norm),
        grid=(nt,),
        in_specs=[pl.BlockSpec((2 * tm,), lambda i: (i,), memory_space=pltpu.SMEM),
                  pl.BlockSpec((2 * tm,), lambda i: (jnp.minimum(i + 1, nt - 1),), memory_space=pltpu.SMEM),
                  pl.BlockSpec((tm, D), lambda i: (i, 0)),
                  pl.BlockSpec((tm, LANES), lambda i: (i, 0)),
                  pl.BlockSpec((1, D), lambda i: (0, 0)),
                  pl.BlockSpec(memory_space=pl.ANY)],
        out_specs=pl.BlockSpec((tm, D), lambda i: (i, 0)),
        out_shape=jax.ShapeDtypeStruct((T, D), F32),
        scratch_shapes=[pltpu.VMEM((2, 2, tm * ROW_TILE, LANES), F32), pltpu.SemaphoreType.DMA((2,))],
        compiler_params=_cparams(("arbitrary",)),
        name="moe_combine",
    )(dest_flat, dest_flat, x2, mf, g_final.reshape(1, D), ybuf)


def _moe_layer(x2, o2, wo, g, router, w1, w3, w2, g_final, final_norm, buf0=None):
    T, D = x2.shape
    x2, h2, mi, mf, cnt = _router(x2, o2, wo, g, router)
    counts = cnt[0, :N_EXPERTS].astype(jnp.int32)
    padded = ((counts + MOE_BLOCK - 1) // MOE_BLOCK) * MOE_BLOCK
    pend = jnp.cumsum(padded)
    pstart = pend - padded
    dest = pstart[mi[:, 0:2]] + mi[:, 2:4]
    dest_flat = dest.reshape(-1).astype(jnp.int32)
    nb = -(-(2 * T + N_EXPERTS * (MOE_BLOCK - 1)) // MOE_BLOCK)
    block_start = jnp.arange(nb, dtype=jnp.int32) * MOE_BLOCK
    block_expert = jnp.minimum(jnp.sum(pend[None, :] <= block_start[:, None], axis=1),
                               N_EXPERTS - 1).astype(jnp.int32)
    n_used = (pend[-1:] // MOE_BLOCK).astype(jnp.int32)
    if buf0 is None:
        buf0 = jnp.zeros((nb * MOE_BLOCK * ROW_TILE, LANES), F32)
    buf = _dispatch(h2, dest_flat, nb * MOE_BLOCK, buf0)
    ybuf = _moe_ffn(buf, block_expert, n_used, w1, w3, w2)
    return _combine(x2, mf, dest_flat, ybuf, g_final, final_norm), ybuf


def _mixer_a(x, g, w_in, w_out, tabs):
    B, S, D = x.shape
    w = w_in.astype(BF16)
    qkvs = _norm_proj_a(x, g, w, tabs)
    o = _attn_a(qkvs, B, S)
    return o.reshape(B * S, D), w_out.astype(BF16)


def _mixer_b(x, g, w_in, lq1, lk1, lq2, lk2, subln, w_out, lambda_init, tabs, side_casts=()):
    B, S, D = x.shape
    cos, sin = tabs[1]
    qkv = _norm_proj(x, g, w_in.astype(BF16), cos, sin, tc=D, rope_chunks=_QKV_ROPE_CHUNKS)
    lam_params = jnp.stack([lq1, lk1, lq2, lk2]).astype(F32)
    o, casted = _attn_b(qkv, lam_params, subln.astype(F32), lambda_init, B, S, side_casts)
    return o.reshape(B * S, D), w_out.astype(BF16), casted


def _mixer_c(x, g, w_in, sink, w_out, tabs):
    B, S, D = x.shape
    cos, sin = tabs[1]
    head_order = jnp.arange(16).reshape(2, 8).T.reshape(-1)
    col_order = (head_order[:, None] * HEAD_DIM + jnp.arange(HEAD_DIM)[None, :]).reshape(-1)
    w_q = w_in[:, :D][:, col_order]
    w = jnp.concatenate([w_q, w_in[:, D:]], axis=1).astype(BF16)
    ncols = w.shape[1]
    chunks = tuple(c < (ncols // LANES - 1) for c in range(ncols // LANES))
    qkv = _norm_proj(x, g, w, cos, sin, tc=ncols, rope_chunks=chunks)
    sk = sink.astype(F32)
    sink_tab = jnp.concatenate([jnp.broadcast_to(sk[:8, None], (8, LANES)),
                                jnp.broadcast_to(sk[8:, None], (8, LANES))], axis=1)
    o = _attn_c(qkv, sink_tab, B, S)
    w_o = w_out[col_order, :].astype(BF16)
    return o.reshape(B * S, D), w_o


def kernel(x, positions, l0_norm_mix, l0_a_w_in, l0_a_w_out, l0_norm_ffn, l0_ffn_w1, l0_ffn_w3, l0_ffn_w2, l1_norm_mix, l1_b_w_in, l1_b_lambda_q1, l1_b_lambda_k1, l1_b_lambda_q2, l1_b_lambda_k2, l1_b_subln, l1_b_w_out, l1_norm_ffn, l1_moe_router, l1_moe_w1, l1_moe_w3, l1_moe_w2, l2_norm_mix, l2_c_w_in, l2_c_sink, l2_c_w_out, l2_norm_ffn, l2_ffn_w1, l2_ffn_w3, l2_ffn_w2, l3_norm_mix, l3_a_w_in, l3_a_w_out, l3_norm_ffn, l3_moe_router, l3_moe_w1, l3_moe_w3, l3_moe_w2, final_norm):
    B, S, D = x.shape
    T = B * S
    cos, sin = lax.optimization_barrier(_rope_tables(positions))
    tabs = {d: (_to_strided(cos, d), _to_strided(sin, d)) for d in DILATIONS}

    o, wo = _mixer_a(x, l0_norm_mix, l0_a_w_in, l0_a_w_out, tabs)
    x = _ffn_dense(x.reshape(T, D), o, wo, l0_norm_ffn, l0_ffn_w1.astype(BF16), l0_ffn_w3.astype(BF16),
                   l0_ffn_w2.astype(BF16)).reshape(B, S, D)
    lambda_init = 0.8 - 0.6 * math.exp(-0.3 * 1)
    moe_w = (l1_moe_w1, l1_moe_w3, l1_moe_w2, l3_moe_w1, l3_moe_w3, l3_moe_w2)
    o, wo, moe_bf = _mixer_b(x, l1_norm_mix, l1_b_w_in, l1_b_lambda_q1, l1_b_lambda_k1, l1_b_lambda_q2,
                             l1_b_lambda_k2, l1_b_subln, l1_b_w_out, lambda_init, tabs,
                             side_casts=tuple(w.reshape(-1, w.shape[-1]) for w in moe_w))
    moe_bf = [wb.reshape(w.shape) for wb, w in zip(moe_bf, moe_w)]
    x, spare = _moe_layer(x.reshape(T, D), o, wo, l1_norm_ffn, l1_moe_router, moe_bf[0], moe_bf[1], moe_bf[2],
                          final_norm, False)
    x = x.reshape(B, S, D)
    o, wo = _mixer_c(x, l2_norm_mix, l2_c_w_in, l2_c_sink, l2_c_w_out, tabs)
    x = _ffn_dense(x.reshape(T, D), o, wo, l2_norm_ffn, l2_ffn_w1.astype(BF16), l2_ffn_w3.astype(BF16),
                   l2_ffn_w2.astype(BF16)).reshape(B, S, D)
    o, wo = _mixer_a(x, l3_norm_mix, l3_a_w_in, l3_a_w_out, tabs)
    x, _ = _moe_layer(x.reshape(T, D), o, wo, l3_norm_ffn, l3_moe_router, moe_bf[3], moe_bf[4], moe_bf[5],
                      final_norm, True, buf0=spare)
    return x.reshape(B, S, D)
```

```python
import functools
import math

import jax
import jax.numpy as jnp
from jax import lax
from jax.experimental import pallas as pl
from jax.experimental.pallas import tpu as pltpu

F32 = jnp.float32
BF16 = jnp.bfloat16

D_MODEL = 1024
HEAD_DIM = 64
ROT_DIM = HEAD_DIM // 4
ROPE_THETA = 500000.0
NORM_EPS = 1e-6
SUBLN_EPS = 1e-5
DILATIONS = (1, 4, 16)
A_HALF = 64
C_HALF = 128
N_EXPERTS = 8
_QKV_ROPE_CHUNKS = (True,) * 16 + (False,) * 8

LANES = 128
F32_SUBLANES = 8
BF16_SUBLANES = 16
VMEM_LIMIT = 56 * 1024 * 1024

PROJ_TM = 512
Q_SUB = LANES
ATTN_A_TB = 2048
ATTN_C_TB = 1024
ATTN_B_TQ = 1024
ATTN_B_TK = 2048
CK = 512
A_GROUP = 3
C_AHEAD = 4
FFN_TM = 512
MOE_TM = 512
MOE_BLOCK = 512
MOE_TF = 1792


def _cparams(sem):
    return pltpu.CompilerParams(dimension_semantics=sem, vmem_limit_bytes=VMEM_LIMIT)


def _rope_tables(positions):
    lane = jnp.arange(LANES) % HEAD_DIM
    half = ROT_DIM // 2
    inv_freq = ROPE_THETA ** (-((2 * (lane % half)).astype(F32) / ROT_DIM))
    freq = jnp.where(lane < ROT_DIM, inv_freq, 0.0)
    sign = jnp.where(lane < half, -1.0, jnp.where(lane < ROT_DIM, 1.0, 0.0))
    ang = positions.astype(F32)[..., None] * freq
    return jnp.cos(ang), jnp.sin(ang) * sign


def _to_strided(t, d):
    B, S, C = t.shape
    return jnp.swapaxes(t.reshape(B, S // d, d, C), 1, 2)


def _rope_chunk(x, cos, sin, first_half):
    partner = jnp.where(first_half, pltpu.roll(x, LANES - ROT_DIM // 2, 1),
                        pltpu.roll(x, ROT_DIM // 2, 1))
    return x * cos + partner * sin


def _proj_kernel(x_ref, g_ref, w_ref, cos_ref, sin_ref, o_ref, *, rope_chunks, tc):
    ncols = w_ref.shape[1]
    x = x_ref[...]
    ms = jnp.mean(x * x, axis=-1, keepdims=True)
    h = (x * lax.rsqrt(ms + NORM_EPS) * g_ref[...]).astype(BF16)
    lane = lax.broadcasted_iota(jnp.int32, (1, LANES), 1)
    first_half = (lane % HEAD_DIM) < (ROT_DIM // 2)
    cos = cos_ref[...]
    sin = sin_ref[...]
    for c0 in range(0, ncols, tc):
        res = jnp.dot(h, w_ref[:, c0:c0 + tc], preferred_element_type=F32)
        parts = []
        for cc in range(tc // LANES):
            xc = res[:, cc * LANES:(cc + 1) * LANES]
            if rope_chunks[c0 // LANES + cc]:
                xc = _rope_chunk(xc, cos, sin, first_half)
            parts.append(xc.astype(o_ref.dtype))
        o_ref[:, c0:c0 + tc] = jnp.concatenate(parts, axis=1)


def _norm_proj(x, g, w, cos, sin, *, tc, rope_chunks, tm=PROJ_TM):
    B, S, D = x.shape
    ncols = w.shape[1]
    return pl.pallas_call(
        functools.partial(_proj_kernel, rope_chunks=rope_chunks, tc=tc),
        grid=(B, S // tm),
        in_specs=[
            pl.BlockSpec((None, tm, D), lambda b, i: (b, i, 0)),
            pl.BlockSpec((1, D), lambda b, i: (0, 0)),
            pl.BlockSpec((D, ncols), lambda b, i: (0, 0)),
            pl.BlockSpec((None, None, tm, LANES), lambda b, i: (b, 0, i, 0)),
            pl.BlockSpec((None, None, tm, LANES), lambda b, i: (b, 0, i, 0)),
        ],
        out_specs=pl.BlockSpec((None, tm, ncols), lambda b, i: (b, i, 0)),
        out_shape=jax.ShapeDtypeStruct((B, S, ncols), BF16),
        compiler_params=_cparams(("parallel", "parallel")),
        name="norm_proj",
    )(x, g.reshape(1, D), w, cos, sin)


def _proj_a_kernel(x_ref, g_ref, w_ref, c1, s1, c4, s4, c16, s16, p4, p16, o1, o4, o16):
    tm = x_ref.shape[0]
    x = x_ref[...]
    ms = jnp.mean(x * x, axis=-1, keepdims=True)
    h1 = (x * lax.rsqrt(ms + NORM_EPS) * g_ref[...]).astype(BF16)
    lane = lax.broadcasted_iota(jnp.int32, (1, LANES), 1)
    first_half = (lane % HEAD_DIM) < (ROT_DIM // 2)
    ncols = o1.shape[-1]
    for gi, (dil, cos_ref, sin_ref, perm_ref, o_ref) in enumerate(
            ((1, c1, s1, None, o1), (4, c4, s4, p4, o4), (16, c16, s16, p16, o16))):
        n = tm // dil
        h = h1 if perm_ref is None else jnp.dot(perm_ref[...], h1, preferred_element_type=F32).astype(BF16)
        cos = cos_ref[...].reshape(tm, LANES)
        sin = sin_ref[...].reshape(tm, LANES)
        for c0 in range(0, ncols, D_MODEL):
            res = jnp.dot(h, w_ref[:, gi * ncols + c0:gi * ncols + c0 + D_MODEL], preferred_element_type=F32)
            parts = []
            for cc in range(D_MODEL // LANES):
                xc = res[:, cc * LANES:(cc + 1) * LANES]
                if _QKV_ROPE_CHUNKS[c0 // LANES + cc]:
                    xc = _rope_chunk(xc, cos, sin, first_half)
                parts.append(xc.astype(o_ref.dtype))
            val = jnp.concatenate(parts, axis=1)
            for r in range(dil):
                o_ref[r, :, c0:c0 + D_MODEL] = val[r * n:(r + 1) * n]


def _norm_proj_a(x, g, w, tabs, tm=PROJ_TM):
    B, S, D = x.shape
    ncols = 3 * D
    in_specs = [pl.BlockSpec((None, tm, D), lambda b, i: (b, i, 0)),
                pl.BlockSpec((1, D), lambda b, i: (0, 0)),
                pl.BlockSpec((D, 3 * ncols), lambda b, i: (0, 0), pipeline_mode=pl.Buffered(1))]
    args = [x, g.reshape(1, D), w]
    for dil in DILATIONS:
        n = tm // dil
        for tab in tabs[dil]:
            in_specs.append(pl.BlockSpec((None, dil, n, LANES), lambda b, i: (b, 0, i, 0)))
            args.append(tab)
    for dil in DILATIONS[1:]:
        n = tm // dil
        p = jnp.arange(tm)
        src = (p % n) * dil + p // n
        args.append((src[:, None] == jnp.arange(tm)[None, :]).astype(BF16))
        in_specs.append(pl.BlockSpec((tm, tm), lambda b, i: (0, 0)))
    return pl.pallas_call(
        _proj_a_kernel,
        grid=(B, S // tm),
        in_specs=in_specs,
        out_specs=[pl.BlockSpec((None, dil, tm // dil, ncols), lambda b, i: (b, 0, i, 0)) for dil in DILATIONS],
        out_shape=[jax.ShapeDtypeStruct((B, dil, S // dil, ncols), BF16) for dil in DILATIONS],
        compiler_params=_cparams(("parallel", "parallel")),
        name="norm_proj_a",
    )(*args)


def _fill_band_bias(bias_scr, tq, W, half):
    kk = lax.broadcasted_iota(jnp.int32, (W, 2 * tq), 0)
    qq = lax.broadcasted_iota(jnp.int32, (W, 2 * tq), 1) % tq
    for i in range(3):
        ok = jnp.abs(qq + i * half - kk) <= half
        bias_scr[i] = jnp.where(ok, 0.0, -jnp.inf).astype(F32)


def _band_scores(q2, kw, bias):
    lane_lo = lax.broadcasted_iota(jnp.int32, (1, LANES), 1) < HEAD_DIM
    zero = jnp.zeros_like(q2)
    qq = jnp.concatenate([jnp.where(lane_lo, q2, zero), jnp.where(lane_lo, zero, q2)], axis=0)
    qq = qq * jnp.asarray(HEAD_DIM ** -0.5, q2.dtype)
    return lax.dot_general(kw, qq, (((1,), (1,)), ((), ())), preferred_element_type=F32) + bias


def _band_finish(st, vw, want_lse, sink_row=None):
    tq = st.shape[1] // 2
    row_lo = lax.broadcasted_iota(jnp.int32, (LANES, 1), 0) < HEAD_DIM
    m = jnp.max(st, axis=0, keepdims=True)
    if sink_row is not None:
        m = jnp.maximum(m, sink_row)
    p = jnp.exp(st - m)
    l = jnp.sum(p, axis=0, keepdims=True)
    if sink_row is not None:
        l = l + jnp.exp(sink_row - m)
    ot = lax.dot_general(vw, p.astype(BF16), (((0,), (0,)), ((), ())), preferred_element_type=F32) / l
    o = jnp.where(row_lo, ot[:, :tq], ot[:, tq:]).T
    if not want_lse:
        return o, None
    lse_row = m + jnp.log(l)
    lse = jnp.where(row_lo, jnp.broadcast_to(lse_row[:, :tq], (LANES, tq)),
                    jnp.broadcast_to(lse_row[:, tq:], (LANES, tq))).T
    return o, lse


def _attn_a_kernel(*refs, TB, S):
    qkv = refs[:9]
    o_ref = refs[9]
    scr = refs[10:16]
    bias_scr = refs[16]
    t = pl.program_id(2)
    W = 4 * A_HALF

    @pl.when(t == 0)
    def _():
        _fill_band_bias(bias_scr, Q_SUB, W, A_HALF)

    subs = []
    for g, dil in enumerate(DILATIONS):
        nq = TB // dil
        tq = min(nq, Q_SUB)
        for r in range(dil):
            for jj in range(nq // tq):
                subs.append((g, dil, nq, tq, r, jj))

    def scores(sub):
        g, dil, nq, tq, r, jj = sub
        q_ref, k_ref = qkv[3 * g], qkv[3 * g + 1]
        L = S // dil
        qs = t * nq + jj * tq
        ws = pl.multiple_of(jnp.clip(qs - A_HALF, 0, L - W), A_HALF)
        q2 = q_ref[r, jj * tq:(jj + 1) * tq, :]
        return _band_scores(q2, k_ref[r, pl.ds(ws, W), :], bias_scr[(qs - ws) // A_HALF]), ws

    def finish(sub, st, ws):
        g, dil, nq, tq, r, jj = sub
        o_scr, l_scr = scr[2 * g], scr[2 * g + 1]
        o, lse = _band_finish(st, qkv[3 * g + 2][r, pl.ds(ws, W), :], True)
        row0 = jj * (tq * dil) + r
        if dil == 1:
            o_scr[row0:row0 + tq, :] = o
            l_scr[row0:row0 + tq, :] = lse
        else:
            o_scr[pl.ds(row0, tq, stride=dil), :] = o
            l_scr[pl.ds(row0, tq, stride=dil), :] = lse

    pend = [scores(sub) for sub in subs[:A_GROUP]]
    for n, sub in enumerate(subs):
        if n + A_GROUP < len(subs):
            pend.append(scores(subs[n + A_GROUP]))
        st, ws = pend[n]
        finish(sub, st, ws)

    l0, l1, l2 = scr[1][...], scr[3][...], scr[5][...]
    mx = jnp.maximum(jnp.maximum(l0, l1), l2)
    e0, e1, e2 = jnp.exp(l0 - mx), jnp.exp(l1 - mx), jnp.exp(l2 - mx)
    den = e0 + e1 + e2
    o = (e0 * scr[0][...] + e1 * scr[2][...] + e2 * scr[4][...]) / den
    o_ref[...] = o.astype(o_ref.dtype)


def _attn_a(qkvs, B, S, TB=ATTN_A_TB):
    n_pairs = (16 * HEAD_DIM) // LANES
    in_specs, args = [], []
    for g, dil in enumerate(DILATIONS):
        L = S // dil
        in_specs += [
            pl.BlockSpec((None, dil, TB // dil, LANES), lambda b, hp, t: (b, 0, t, hp)),
            pl.BlockSpec((None, dil, L, LANES), lambda b, hp, t: (b, 0, 0, n_pairs + hp)),
            pl.BlockSpec((None, dil, L, LANES), lambda b, hp, t: (b, 0, 0, 2 * n_pairs + hp)),
        ]
        args += [qkvs[g]] * 3
    return pl.pallas_call(
        functools.partial(_attn_a_kernel, TB=TB, S=S),
        grid=(B, n_pairs, S // TB),
        in_specs=in_specs,
        out_specs=pl.BlockSpec((None, TB, LANES), lambda b, hp, t: (b, t, hp)),
        out_shape=jax.ShapeDtypeStruct((B, S, n_pairs * LANES), BF16),
        scratch_shapes=[pltpu.VMEM((TB, LANES), F32)] * 6
        + [pltpu.VMEM((3, Q_SUB + 2 * A_HALF, 2 * Q_SUB), F32)],
        compiler_params=_cparams(("parallel", "parallel", "arbitrary")),
        name="attn_dilated",
    )(*args)


def _attn_c_kernel(q_ref, k_ref, v_ref, sink_ref, o_ref, bias_scr, *, TB, S):
    t = pl.program_id(1)
    tq = Q_SUB
    W = tq + 2 * C_HALF
    n_pairs = q_ref.shape[1] // LANES

    @pl.when(t == 0)
    def _():
        _fill_band_bias(bias_scr, tq, W, C_HALF)

    units = [(jj, p) for jj in range(TB // tq) for p in range(n_pairs)]

    def window(jj):
        qs = t * TB + jj * tq
        ws = pl.multiple_of(jnp.clip(qs - C_HALF, 0, S - W), C_HALF)
        return qs, ws

    def scores(unit):
        jj, p = unit
        qs, ws = window(jj)
        q2 = q_ref[jj * tq:(jj + 1) * tq, p * LANES:(p + 1) * LANES]
        return _band_scores(q2, k_ref[pl.ds(ws, W), :], bias_scr[(qs - ws) // C_HALF])

    pend = [scores(u) for u in units[:C_AHEAD]]
    for n, (jj, p) in enumerate(units):
        if n + C_AHEAD < len(units):
            pend.append(scores(units[n + C_AHEAD]))
        _, ws = window(jj)
        sink_row = sink_ref[p:p + 1, :]
        o, _ = _band_finish(pend[n], v_ref[pl.ds(ws, W), :], False, sink_row=sink_row)
        pend[n] = None
        o_ref[jj * tq:(jj + 1) * tq, p * LANES:(p + 1) * LANES] = o.astype(o_ref.dtype)


def _attn_c(qkv, sink_tab, B, S, TB=ATTN_C_TB):
    nq = 16 * HEAD_DIM
    return pl.pallas_call(
        functools.partial(_attn_c_kernel, TB=TB, S=S),
        grid=(B, S // TB),
        in_specs=[
            pl.BlockSpec((None, TB, nq), lambda b, t: (b, t, 0)),
            pl.BlockSpec((None, S, LANES), lambda b, t: (b, 0, nq // LANES)),
            pl.BlockSpec((None, S, LANES), lambda b, t: (b, 0, nq // LANES + 1)),
            pl.BlockSpec((8, 2 * LANES), lambda b, t: (0, 0)),
        ],
        out_specs=pl.BlockSpec((None, TB, nq), lambda b, t: (b, t, 0)),
        out_shape=jax.ShapeDtypeStruct((B, S, nq), BF16),
        scratch_shapes=[pltpu.VMEM((3, Q_SUB + 2 * C_HALF, 2 * Q_SUB), F32)],
        compiler_params=_cparams(("parallel", "arbitrary")),
        name="attn_swa_sink",
    )(qkv, qkv, qkv, sink_tab)


def _attn_b_kernel(*refs, lambda_init, n_cast):
    lam_ref, q_ref, qn_ref, k_ref, kn_ref, v_ref, subln_ref = refs[:7]
    cast_in = refs[7:7 + n_cast]
    o_ref = refs[7 + n_cast]
    cast_out = refs[8 + n_cast:8 + 2 * n_cast]
    s0_scr, s1_scr, m_scr, l_scr, acc_scr = refs[8 + 2 * n_cast:]
    _attn_b_body(lam_ref, q_ref, qn_ref, k_ref, kn_ref, v_ref, subln_ref, o_ref,
                 s0_scr, s1_scr, m_scr, l_scr, acc_scr, lambda_init=lambda_init)
    for src_ref, dst_ref in zip(cast_in, cast_out):
        dst_ref[...] = src_ref[...].astype(dst_ref.dtype)


def _attn_b_body(lam_ref, q_ref, qn_ref, k_ref, kn_ref, v_ref, subln_ref, o_ref,
                 s0_scr, s1_scr, m_scr, l_scr, acc_scr, *, lambda_init):
    i = pl.program_id(2)
    kv = pl.program_id(3)
    nkv = pl.num_programs(3)
    tk = k_ref.shape[0]
    lane = lax.broadcasted_iota(jnp.int32, (1, LANES), 1)
    lane_lo = lane < HEAD_DIM

    def scores(kref, qref, c, j):
        k = kref[j * CK:(j + 1) * CK, :] * jnp.asarray(HEAD_DIM ** -0.5, kref.dtype)
        sel = lane_lo if c == 0 else jnp.logical_not(lane_lo)
        kc = jnp.where(sel, k, jnp.zeros_like(k))
        return lax.dot_general(kc, qref[...], (((1,), (1,)), ((), ())), preferred_element_type=F32)

    @pl.when(kv == 0)
    def _():
        m_scr[...] = jnp.full(m_scr.shape, -jnp.inf, F32)
        l_scr[...] = jnp.zeros(l_scr.shape, F32)
        acc_scr[...] = jnp.zeros(acc_scr.shape, F32)

    @pl.when(jnp.logical_and(i == 0, kv == 0))
    def _():
        for c in range(2):
            for j in range(tk // CK):
                s0_scr[c * tk + j * CK:c * tk + (j + 1) * CK, :] = scores(k_ref, q_ref, c, j)

    def step(cur_scr, nxt_scr):
        vt = jnp.concatenate([v_ref[...].T, jnp.ones((BF16_SUBLANES, tk), BF16)], axis=0)
        for c in range(2):
            m, l, acc = m_scr[c], l_scr[c], acc_scr[c]
            for j in range(tk // CK):
                rows = slice(c * tk + j * CK, c * tk + (j + 1) * CK)
                nxt_scr[rows, :] = scores(kn_ref, qn_ref, c, j)
                s = cur_scr[rows, :]
                m_new = jnp.maximum(m, jnp.max(s, axis=0, keepdims=True))
                alpha = jnp.exp(m - m_new)
                p = jnp.exp(s - m_new).astype(BF16)
                pv = jnp.dot(vt[:, j * CK:(j + 1) * CK], p, preferred_element_type=F32)
                l = alpha * l + pv[LANES:LANES + 1]
                acc = alpha * acc + pv[:LANES]
                m = m_new
            m_scr[c], l_scr[c], acc_scr[c] = m, l, acc

    @pl.when(kv % 2 == 0)
    def _():
        step(s0_scr, s1_scr)

    @pl.when(kv % 2 == 1)
    def _():
        step(s1_scr, s0_scr)

    @pl.when(kv == nkv - 1)
    def _():
        lp = lam_ref[...]
        lam = (jnp.exp(jnp.sum(lp[0:1] * lp[1:2], axis=-1, keepdims=True))
               - jnp.exp(jnp.sum(lp[2:3] * lp[3:4], axis=-1, keepdims=True)) + lambda_init)
        o = acc_scr[0] / l_scr[0] - lam * (acc_scr[1] / l_scr[1])
        ms = jnp.mean(o * o, axis=0, keepdims=True)
        o = o * lax.rsqrt(ms + SUBLN_EPS) * subln_ref[...] * (1.0 - lambda_init)
        o_ref[...] = o.T.astype(o_ref.dtype)


def _attn_b(qkv, lam_params, subln, lambda_init, B, S, side_casts=(), tq=ATTN_B_TQ, tk=ATTN_B_TK):
    H = 8
    nq, nkv = S // tq, S // tk
    assert nkv % 2 == 0
    nsteps = B * H * nq * nkv
    cast_specs, cast_shapes = [], []
    for w in side_casts:
        rows, cols = w.shape
        nblk = nsteps
        while rows % nblk or (rows // nblk) % BF16_SUBLANES:
            nblk //= 2
        per = nsteps // nblk

        def w_map(b, h, i, kv, per=per):
            return ((((b * H + h) * nq + i) * nkv + kv) // per, 0)

        cast_specs.append(pl.BlockSpec((rows // nblk, cols), w_map))
        cast_shapes.append(jax.ShapeDtypeStruct((rows, cols), BF16))

    def q_next(b, h, i, kv):
        return (b, jnp.minimum(i + (kv + 1) // nkv, nq - 1), h)

    def k_next(b, h, i, kv):
        return (b, (kv + 1) % nkv, H + h)

    outs = pl.pallas_call(
        functools.partial(_attn_b_kernel, lambda_init=lambda_init, n_cast=len(side_casts)),
        grid=(B, H, nq, nkv),
        in_specs=[
            pl.BlockSpec((4, HEAD_DIM), lambda b, h, i, kv: (0, 0)),
            pl.BlockSpec((None, tq, LANES), lambda b, h, i, kv: (b, i, h)),
            pl.BlockSpec((None, tq, LANES), q_next),
            pl.BlockSpec((None, tk, LANES), lambda b, h, i, kv: (b, kv, H + h)),
            pl.BlockSpec((None, tk, LANES), k_next),
            pl.BlockSpec((None, tk, LANES), lambda b, h, i, kv: (b, kv, 2 * H + h)),
            pl.BlockSpec((LANES, 1), lambda b, h, i, kv: (0, 0)),
        ] + cast_specs,
        out_specs=[pl.BlockSpec((None, tq, LANES), lambda b, h, i, kv: (b, i, h))] + cast_specs,
        out_shape=[jax.ShapeDtypeStruct((B, S, H * LANES), BF16)] + cast_shapes,
        scratch_shapes=[pltpu.VMEM((2 * tk, tq), F32), pltpu.VMEM((2 * tk, tq), F32),
                        pltpu.VMEM((2, 1, tq), F32), pltpu.VMEM((2, 1, tq), F32),
                        pltpu.VMEM((2, LANES, tq), F32)],
        compiler_params=_cparams(("parallel", "parallel", "arbitrary", "arbitrary")),
        name="attn_diff",
    )(lam_params, qkv, qkv, qkv, qkv, qkv, subln.reshape(LANES, 1), *side_casts)
    return outs[0], outs[1:]


def _ffn_kernel(x_ref, o_ref, wo_ref, g_ref, w1_ref, w3_ref, w2_ref, y_ref):
    x = x_ref[...] + jnp.dot(o_ref[...], wo_ref[...], preferred_element_type=F32)
    ms = jnp.mean(x * x, axis=-1, keepdims=True)
    h = (x * lax.rsqrt(ms + NORM_EPS) * g_ref[...]).astype(BF16)
    u = jnp.dot(h, w1_ref[...], preferred_element_type=F32)
    v = jnp.dot(h, w3_ref[...], preferred_element_type=F32)
    a = (u * jax.nn.sigmoid(u) * v).astype(BF16)
    y_ref[...] = x + jnp.dot(a, w2_ref[...], preferred_element_type=F32)


def _ffn_dense(x2, o2, wo, g, w1, w3, w2, tm=FFN_TM):
    T, D = x2.shape
    K = o2.shape[1]
    F = w1.shape[1]
    return pl.pallas_call(
        _ffn_kernel,
        grid=(T // tm,),
        in_specs=[pl.BlockSpec((tm, D), lambda i: (i, 0)),
                  pl.BlockSpec((tm, K), lambda i: (i, 0)),
                  pl.BlockSpec((K, D), lambda i: (0, 0)),
                  pl.BlockSpec((1, D), lambda i: (0, 0)),
                  pl.BlockSpec((D, F), lambda i: (0, 0)),
                  pl.BlockSpec((D, F), lambda i: (0, 0)),
                  pl.BlockSpec((F, D), lambda i: (0, 0))],
        out_specs=pl.BlockSpec((tm, D), lambda i: (i, 0)),
        out_shape=jax.ShapeDtypeStruct((T, D), F32),
        compiler_params=_cparams(("parallel",)),
        name="ffn_dense",
    )(x2, o2, wo, g.reshape(1, D), w1, w3, w2)


ROW_TILE = F32_SUBLANES


def _rows_to_tiles(ref, val):
    n = val.shape[0]
    for a in range(ROW_TILE):
        ref[pl.ds(a, n, stride=ROW_TILE), :] = val[:, a * LANES:(a + 1) * LANES]


def _tiles_to_rows(ref, n):
    return jnp.concatenate([ref[pl.ds(a, n, stride=ROW_TILE), :] for a in range(ROW_TILE)], axis=1)


def _router_kernel(x_ref, o_ref, wo_ref, g_ref, r_ref, x1_ref, h_ref, mi_ref, mf_ref, cnt_ref, st_ref, carry_scr):
    i = pl.program_id(0)
    tm = x_ref.shape[0]

    @pl.when(i == 0)
    def _():
        carry_scr[...] = jnp.zeros(carry_scr.shape, F32)

    x = x_ref[...] + jnp.dot(o_ref[...], wo_ref[...], preferred_element_type=F32)
    x1_ref[...] = x
    ms = jnp.mean(x * x, axis=-1, keepdims=True)
    h = x * lax.rsqrt(ms + NORM_EPS) * g_ref[...]
    _rows_to_tiles(h_ref, h)
    lane = lax.broadcasted_iota(jnp.int32, (tm, LANES), 1)
    logits = jnp.full((tm, LANES), -jnp.inf, F32)
    for e in range(N_EXPERTS):
        col = jnp.sum(h * r_ref[e:e + 1, :], axis=-1, keepdims=True)
        logits = jnp.where(lane == e, col, logits)
    v0 = jnp.max(logits, axis=-1, keepdims=True)
    i0 = jnp.min(jnp.where(logits == v0, lane, LANES), axis=-1, keepdims=True)
    rest = jnp.where(lane == i0, -jnp.inf, logits)
    v1 = jnp.max(rest, axis=-1, keepdims=True)
    i1 = jnp.min(jnp.where(rest == v1, lane, LANES), axis=-1, keepdims=True)
    tt = jnp.exp(v1 - v0)
    g0 = 1.0 / (1.0 + tt)
    g1 = tt / (1.0 + tt)
    sel0 = lane == i0
    sel1 = lane == i1
    onehot = jnp.where(jnp.logical_or(sel0, sel1), 1.0, 0.0)
    row = lax.broadcasted_iota(jnp.int32, (tm, tm), 0)
    col = lax.broadcasted_iota(jnp.int32, (tm, tm), 1)
    tri = jnp.where(row > col, 1.0, 0.0).astype(BF16)
    prefix = jnp.dot(tri, onehot.astype(BF16), preferred_element_type=F32)
    before = carry_scr[...] + prefix
    rank0 = jnp.sum(jnp.where(sel0, before, 0.0), axis=-1, keepdims=True)
    rank1 = jnp.sum(jnp.where(sel1, before, 0.0), axis=-1, keepdims=True)
    tile_cnt = jnp.sum(onehot, axis=0, keepdims=True)
    lane1 = lane[0:1]
    cum = jnp.zeros((1, LANES), F32)
    for e in range(N_EXPERTS - 1):
        cum = cum + jnp.where(lane1 > e, tile_cnt[:, e:e + 1], 0.0)
    loc0 = jnp.sum(jnp.where(sel0, cum + prefix, 0.0), axis=-1, keepdims=True)
    loc1 = jnp.sum(jnp.where(sel1, cum + prefix, 0.0), axis=-1, keepdims=True)
    row8 = lax.broadcasted_iota(jnp.int32, (F32_SUBLANES, LANES), 0)
    st_ref[...] = jnp.where(row8 == 0, carry_scr[...], jnp.where(row8 == 1, tile_cnt, jnp.where(row8 == 2, cum, 0.0)))
    carry_scr[...] = carry_scr[...] + tile_cnt
    mi = jnp.where(lane == 0, i0, jnp.where(lane == 1, i1, 0))
    mi = jnp.where(lane == 2, rank0.astype(jnp.int32), jnp.where(lane == 3, rank1.astype(jnp.int32), mi))
    mi = jnp.where(lane == 4, loc0.astype(jnp.int32), jnp.where(lane == 5, loc1.astype(jnp.int32), mi))
    mi_ref[...] = mi
    mf_ref[...] = jnp.where(lane == 0, g0, jnp.where(lane == 1, g1, 0.0))
    cnt_ref[...] = carry_scr[...]


def _router(x2, o2, wo, g, router, tm=MOE_TM):
    T, D = x2.shape
    K = o2.shape[1]
    rt = router.astype(F32).T
    return pl.pallas_call(
        _router_kernel,
        grid=(T // tm,),
        in_specs=[pl.BlockSpec((tm, D), lambda i: (i, 0)),
                  pl.BlockSpec((tm, K), lambda i: (i, 0)),
                  pl.BlockSpec((K, D), lambda i: (0, 0)),
                  pl.BlockSpec((1, D), lambda i: (0, 0)),
                  pl.BlockSpec((N_EXPERTS, D), lambda i: (0, 0))],
        out_specs=[pl.BlockSpec((tm, D), lambda i: (i, 0)),
                   pl.BlockSpec((tm * ROW_TILE, LANES), lambda i: (i, 0)),
                   pl.BlockSpec((tm, LANES), lambda i: (i, 0)),
                   pl.BlockSpec((tm, LANES), lambda i: (i, 0)),
                   pl.BlockSpec((1, LANES), lambda i: (0, 0)),
                   pl.BlockSpec((F32_SUBLANES, LANES), lambda i: (i, 0))],
        out_shape=[jax.ShapeDtypeStruct((T, D), F32),
                   jax.ShapeDtypeStruct((T * ROW_TILE, LANES), F32),
                   jax.ShapeDtypeStruct((T, LANES), jnp.int32),
                   jax.ShapeDtypeStruct((T, LANES), F32),
                   jax.ShapeDtypeStruct((1, LANES), F32),
                   jax.ShapeDtypeStruct((T // tm * F32_SUBLANES, LANES), F32)],
        scratch_shapes=[pltpu.VMEM((1, LANES), F32)],
        compiler_params=_cparams(("arbitrary",)),
        name="moe_router",
    )(x2, o2, wo, g.reshape(1, D), rt)


def _dispatch_kernel(dest_ref, h_ref, buf_in, buf_hbm, sem, *, tm):
    del buf_in

    def copy(n, slot):
        return pltpu.make_async_copy(h_ref.at[pl.ds(pl.multiple_of(n * ROW_TILE, ROW_TILE), ROW_TILE)],
                                     buf_hbm.at[pl.ds(pl.multiple_of(slot * ROW_TILE, ROW_TILE), ROW_TILE)], sem)

    def issue(n, carry):
        copy(n, dest_ref[2 * n]).start(priority=0)
        copy(n, dest_ref[2 * n + 1]).start(priority=1)
        return carry

    lax.fori_loop(0, tm, issue, 0)

    def drain(n, carry):
        copy(0, 0).wait()
        copy(0, 0).wait()
        return carry

    lax.fori_loop(0, tm, drain, 0)


def _dispatch(h2, dest_flat, n_rows, buf0, tm=MOE_TM):
    T = h2.shape[0] // ROW_TILE
    return pl.pallas_call(
        functools.partial(_dispatch_kernel, tm=tm),
        grid=(T // tm,),
        in_specs=[pl.BlockSpec((2 * tm,), lambda i: (i,), memory_space=pltpu.SMEM),
                  pl.BlockSpec((tm * ROW_TILE, LANES), lambda i: (i, 0)),
                  pl.BlockSpec(memory_space=pl.ANY)],
        out_specs=pl.BlockSpec(memory_space=pl.ANY),
        out_shape=jax.ShapeDtypeStruct((n_rows * ROW_TILE, LANES), F32),
        scratch_shapes=[pltpu.SemaphoreType.DMA(())],
        input_output_aliases={2: 0},
        compiler_params=_cparams(("arbitrary",)),
        name="moe_dispatch",
    )(dest_flat, h2, buf0)


def _moe_ffn_kernel(be_ref, nu_ref, x_ref, w1_ref, w3_ref, w2_ref, y_ref, acc_scr):
    b = pl.program_id(0)
    f = pl.program_id(1)

    @pl.when(b < nu_ref[0])
    def _():
        @pl.when(f == 0)
        def _():
            acc_scr[...] = jnp.zeros(acc_scr.shape, F32)

        h = _tiles_to_rows(x_ref, MOE_BLOCK).astype(BF16)
        u = jnp.dot(h, w1_ref[...], preferred_element_type=F32)
        v = jnp.dot(h, w3_ref[...], preferred_element_type=F32)
        a = (u * jax.nn.sigmoid(u) * v).astype(BF16)
        acc_scr[...] += jnp.dot(a, w2_ref[...], preferred_element_type=F32)

        @pl.when(f == pl.num_programs(1) - 1)
        def _():
            _rows_to_tiles(y_ref, acc_scr[...])

    @pl.when(jnp.logical_and(b >= nu_ref[0], f == 0))
    def _():
        y_ref[...] = jnp.zeros(y_ref.shape, F32)


def _moe_ffn(buf, block_expert, n_used, w1, w3, w2, tf=MOE_TF):
    R = buf.shape[0] // ROW_TILE
    E, D, F = w1.shape
    nb = R // MOE_BLOCK

    def last_live(b, nu):
        return jnp.maximum(jnp.minimum(b, nu[0] - 1), 0)

    def row_map(b, f, be, nu):
        return (last_live(b, nu), 0)

    def w_in_map(b, f, be, nu):
        live = b < nu[0]
        return (be[last_live(b, nu)], 0, jnp.where(live, f, F // tf - 1))

    def w_out_map(b, f, be, nu):
        live = b < nu[0]
        return (be[last_live(b, nu)], jnp.where(live, f, F // tf - 1), 0)

    return pl.pallas_call(
        _moe_ffn_kernel,
        grid_spec=pltpu.PrefetchScalarGridSpec(
            num_scalar_prefetch=2,
            grid=(nb, F // tf),
            in_specs=[pl.BlockSpec((MOE_BLOCK * ROW_TILE, LANES), row_map),
                      pl.BlockSpec((None, D, tf), w_in_map),
                      pl.BlockSpec((None, D, tf), w_in_map),
                      pl.BlockSpec((None, tf, D), w_out_map)],
            out_specs=pl.BlockSpec((MOE_BLOCK * ROW_TILE, LANES), lambda b, f, be, nu: (b, 0)),
            scratch_shapes=[pltpu.VMEM((MOE_BLOCK, D), F32)],
        ),
        out_shape=jax.ShapeDtypeStruct((R * ROW_TILE, LANES), F32),
        compiler_params=_cparams(("arbitrary", "arbitrary")),
        name="moe_ffn",
    )(block_expert, n_used, buf, w1, w3, w2)


INFO_LEN = 4 * MOE_TM
SIZE_BITS = tuple(range(MOE_TM.bit_length() - 1, -1, -1))


def _combine_kernel(info_ref, inext_ref, x_ref, mf_ref, g_ref, ybuf_hbm, y_ref, comp_scr, r0_scr, r1_scr, sems,
                    *, tm, final_norm):
    i = pl.program_id(0)
    last = pl.num_programs(0) - 1

    def ranges(iref, par, wait):
        for e in range(N_EXPERTS):
            start = iref[2 * tm + e]
            cnt = iref[2 * tm + N_EXPERTS + e]
            cum = iref[2 * tm + 2 * N_EXPERTS + e]
            for bit in SIZE_BITS:
                size = 1 << bit

                @pl.when((cnt >> bit) & 1 == 1)
                def _(bit=bit, size=size, start=start, cnt=cnt, cum=cum):
                    done = (cnt >> (bit + 1)) << (bit + 1)
                    src = pl.ds(pl.multiple_of((start + done) * ROW_TILE, ROW_TILE), size * ROW_TILE)
                    dst = pl.ds(pl.multiple_of((cum + done) * ROW_TILE, ROW_TILE), size * ROW_TILE)
                    cp = pltpu.make_async_copy(ybuf_hbm.at[src], comp_scr.at[par, dst], sems.at[par])
                    cp.wait() if wait else cp.start()

    def finish(par):
        ranges(info_ref, par, True)

        def pick(n, carry):
            rows = pl.ds(pl.multiple_of(n * ROW_TILE, ROW_TILE), ROW_TILE)
            j0 = pl.ds(pl.multiple_of(info_ref[2 * n] * ROW_TILE, ROW_TILE), ROW_TILE)
            j1 = pl.ds(pl.multiple_of(info_ref[2 * n + 1] * ROW_TILE, ROW_TILE), ROW_TILE)
            r0_scr[rows, :] = comp_scr[par, j0, :]
            r1_scr[rows, :] = comp_scr[par, j1, :]
            return carry

        lax.fori_loop(0, tm, pick, 0)
        mf = mf_ref[...]
        y = x_ref[...] + mf[:, 0:1] * _tiles_to_rows(r0_scr, tm) + mf[:, 1:2] * _tiles_to_rows(r1_scr, tm)
        if final_norm:
            ms = jnp.mean(y * y, axis=-1, keepdims=True)
            y = y * lax.rsqrt(ms + NORM_EPS) * g_ref[...]
        y_ref[...] = y

    @pl.when(i == 0)
    def _():
        ranges(info_ref, 0, False)

    for par in range(2):
        @pl.when(jnp.logical_and(i % 2 == par, i < last))
        def _(par=par):
            ranges(inext_ref, 1 - par, False)

        @pl.when(i % 2 == par)
        def _(par=par):
            finish(par)


def _combine(x2, mf, info, ybuf, g_final, final_norm, tm=MOE_TM):
    T, D = x2.shape
    nt = T // tm
    return pl.pallas_call(
        functools.partial(_combine_kernel, tm=tm, final_norm=final_norm),
        grid=(nt,),
        in_specs=[pl.BlockSpec((INFO_LEN,), lambda i: (i,), memory_space=pltpu.SMEM),
                  pl.BlockSpec((INFO_LEN,), lambda i: (jnp.minimum(i + 1, nt - 1),), memory_space=pltpu.SMEM),
                  pl.BlockSpec((tm, D), lambda i: (i, 0)),
                  pl.BlockSpec((tm, LANES), lambda i: (i, 0)),
                  pl.BlockSpec((1, D), lambda i: (0, 0)),
                  pl.BlockSpec(memory_space=pl.ANY)],
        out_specs=pl.BlockSpec((tm, D), lambda i: (i, 0)),
        out_shape=jax.ShapeDtypeStruct((T, D), F32),
        scratch_shapes=[pltpu.VMEM((2, 2 * tm * ROW_TILE, LANES), F32),
                        pltpu.VMEM((tm * ROW_TILE, LANES), F32), pltpu.VMEM((tm * ROW_TILE, LANES), F32),
                        pltpu.SemaphoreType.DMA((2,))],
        compiler_params=_cparams(("arbitrary",)),
        name="moe_combine",
    )(info, info, x2, mf, g_final.reshape(1, D), ybuf)


def _moe_layer(x2, o2, wo, g, router, w1, w3, w2, g_final, final_norm, buf0=None):
    T, D = x2.shape
    x2, h2, mi, mf, cnt, stats = _router(x2, o2, wo, g, router)
    counts = cnt[0, :N_EXPERTS].astype(jnp.int32)
    padded = ((counts + MOE_BLOCK - 1) // MOE_BLOCK) * MOE_BLOCK
    pend = jnp.cumsum(padded)
    pstart = pend - padded
    dest = pstart[mi[:, 0:2]] + mi[:, 2:4]
    dest_flat = dest.reshape(-1).astype(jnp.int32)
    nb = -(-(2 * T + N_EXPERTS * (MOE_BLOCK - 1)) // MOE_BLOCK)
    block_start = jnp.arange(nb, dtype=jnp.int32) * MOE_BLOCK
    block_expert = jnp.minimum(jnp.sum(pend[None, :] <= block_start[:, None], axis=1),
                               N_EXPERTS - 1).astype(jnp.int32)
    n_used = (pend[-1:] // MOE_BLOCK).astype(jnp.int32)
    if buf0 is None:
        buf0 = jnp.zeros((nb * MOE_BLOCK * ROW_TILE, LANES), F32)
    buf = _dispatch(h2, dest_flat, nb * MOE_BLOCK, buf0)
    ybuf = _moe_ffn(buf, block_expert, n_used, w1, w3, w2)
    nt = T // MOE_TM
    st = stats.reshape(nt, F32_SUBLANES, LANES)[:, :3, :N_EXPERTS].astype(jnp.int32)
    info = jnp.concatenate([mi[:, 4:6].reshape(nt, 2 * MOE_TM), pstart[None, :] + st[:, 0], st[:, 1], st[:, 2],
                            jnp.zeros((nt, INFO_LEN - 2 * MOE_TM - 3 * N_EXPERTS), jnp.int32)], axis=1)
    return _combine(x2, mf, info.reshape(-1), ybuf, g_final, final_norm), ybuf


def _mixer_a(x, g, w_in, w_out, tabs):
    B, S, D = x.shape
    w = w_in.astype(BF16)
    qkvs = _norm_proj_a(x, g, w, tabs)
    o = _attn_a(qkvs, B, S)
    return o.reshape(B * S, D), w_out.astype(BF16)


def _mixer_b(x, g, w_in, lq1, lk1, lq2, lk2, subln, w_out, lambda_init, tabs, side_casts=()):
    B, S, D = x.shape
    cos, sin = tabs[1]
    qkv = _norm_proj(x, g, w_in.astype(BF16), cos, sin, tc=D, rope_chunks=_QKV_ROPE_CHUNKS)
    lam_params = jnp.stack([lq1, lk1, lq2, lk2]).astype(F32)
    o, casted = _attn_b(qkv, lam_params, subln.astype(F32), lambda_init, B, S, side_casts)
    return o.reshape(B * S, D), w_out.astype(BF16), casted


def _mixer_c(x, g, w_in, sink, w_out, tabs):
    B, S, D = x.shape
    cos, sin = tabs[1]
    head_order = jnp.arange(16).reshape(2, 8).T.reshape(-1)
    col_order = (head_order[:, None] * HEAD_DIM + jnp.arange(HEAD_DIM)[None, :]).reshape(-1)
    w_q = w_in[:, :D][:, col_order]
    w = jnp.concatenate([w_q, w_in[:, D:]], axis=1).astype(BF16)
    ncols = w.shape[1]
    chunks = tuple(c < (ncols // LANES - 1) for c in range(ncols // LANES))
    qkv = _norm_proj(x, g, w, cos, sin, tc=ncols, rope_chunks=chunks)
    sk = sink.astype(F32)
    sink_tab = jnp.concatenate([jnp.broadcast_to(sk[:8, None], (8, LANES)),
                                jnp.broadcast_to(sk[8:, None], (8, LANES))], axis=1)
    o = _attn_c(qkv, sink_tab, B, S)
    w_o = w_out[col_order, :].astype(BF16)
    return o.reshape(B * S, D), w_o


def kernel(x, positions, l0_norm_mix, l0_a_w_in, l0_a_w_out, l0_norm_ffn, l0_ffn_w1, l0_ffn_w3, l0_ffn_w2, l1_norm_mix, l1_b_w_in, l1_b_lambda_q1, l1_b_lambda_k1, l1_b_lambda_q2, l1_b_lambda_k2, l1_b_subln, l1_b_w_out, l1_norm_ffn, l1_moe_router, l1_moe_w1, l1_moe_w3, l1_moe_w2, l2_norm_mix, l2_c_w_in, l2_c_sink, l2_c_w_out, l2_norm_ffn, l2_ffn_w1, l2_ffn_w3, l2_ffn_w2, l3_norm_mix, l3_a_w_in, l3_a_w_out, l3_norm_ffn, l3_moe_router, l3_moe_w1, l3_moe_w3, l3_moe_w2, final_norm):
    B, S, D = x.shape
    T = B * S
    cos, sin = lax.optimization_barrier(_rope_tables(positions))
    tabs = {d: (_to_strided(cos, d), _to_strided(sin, d)) for d in DILATIONS}

    o, wo = _mixer_a(x, l0_norm_mix, l0_a_w_in, l0_a_w_out, tabs)
    x = _ffn_dense(x.reshape(T, D), o, wo, l0_norm_ffn, l0_ffn_w1.astype(BF16), l0_ffn_w3.astype(BF16),
                   l0_ffn_w2.astype(BF16)).reshape(B, S, D)
    lambda_init = 0.8 - 0.6 * math.exp(-0.3 * 1)
    moe_w = (l1_moe_w1, l1_moe_w3, l1_moe_w2, l3_moe_w1, l3_moe_w3, l3_moe_w2)
    o, wo, moe_bf = _mixer_b(x, l1_norm_mix, l1_b_w_in, l1_b_lambda_q1, l1_b_lambda_k1, l1_b_lambda_q2,
                             l1_b_lambda_k2, l1_b_subln, l1_b_w_out, lambda_init, tabs,
                             side_casts=tuple(w.reshape(-1, w.shape[-1]) for w in moe_w))
    moe_bf = [wb.reshape(w.shape) for wb, w in zip(moe_bf, moe_w)]
    x, spare = _moe_layer(x.reshape(T, D), o, wo, l1_norm_ffn, l1_moe_router, moe_bf[0], moe_bf[1], moe_bf[2],
                          final_norm, False)
    x = x.reshape(B, S, D)
    o, wo = _mixer_c(x, l2_norm_mix, l2_c_w_in, l2_c_sink, l2_c_w_out, tabs)
    x = _ffn_dense(x.reshape(T, D), o, wo, l2_norm_ffn, l2_ffn_w1.astype(BF16), l2_ffn_w3.astype(BF16),
                   l2_ffn_w2.astype(BF16)).reshape(B, S, D)
    o, wo = _mixer_a(x, l3_norm_mix, l3_a_w_in, l3_a_w_out, tabs)
    x, _ = _moe_layer(x.reshape(T, D), o, wo, l3_norm_ffn, l3_moe_router, moe_bf[3], moe_bf[4], moe_bf[5],
                      final_norm, True, buf0=spare)
    return x.reshape(B, S, D)
```

```python
import functools
import math

import jax
import jax.numpy as jnp
from jax import lax
from jax.experimental import pallas as pl
from jax.experimental.pallas import tpu as pltpu

F32 = jnp.float32
BF16 = jnp.bfloat16

D_MODEL = 1024
HEAD_DIM = 64
ROT_DIM = HEAD_DIM // 4
ROPE_THETA = 500000.0
NORM_EPS = 1e-6
SUBLN_EPS = 1e-5
DILATIONS = (1, 4, 16)
A_HALF = 64
C_HALF = 128
N_EXPERTS = 8
_QKV_ROPE_CHUNKS = (True,) * 16 + (False,) * 8

LANES = 128
F32_SUBLANES = 8
BF16_SUBLANES = 16
VMEM_LIMIT = 56 * 1024 * 1024

PROJ_TM = 512
Q_SUB = LANES
ATTN_A_TB = 2048
ATTN_C_TB = 1024
ATTN_B_TQ = 1024
ATTN_B_TK = 2048
CK = 512
A_GROUP = 3
C_AHEAD = 4
FFN_TM = 512
MOE_TM = 512
MOE_BLOCK = 512
MOE_TF = 1792


def _cparams(sem):
    return pltpu.CompilerParams(dimension_semantics=sem, vmem_limit_bytes=VMEM_LIMIT)


def _rope_tables(positions):
    lane = jnp.arange(LANES) % HEAD_DIM
    half = ROT_DIM // 2
    inv_freq = ROPE_THETA ** (-((2 * (lane % half)).astype(F32) / ROT_DIM))
    freq = jnp.where(lane < ROT_DIM, inv_freq, 0.0)
    sign = jnp.where(lane < half, -1.0, jnp.where(lane < ROT_DIM, 1.0, 0.0))
    ang = positions.astype(F32)[..., None] * freq
    return jnp.cos(ang), jnp.sin(ang) * sign


def _to_strided(t, d):
    B, S, C = t.shape
    return jnp.swapaxes(t.reshape(B, S // d, d, C), 1, 2)


def _rope_chunk(x, cos, sin, first_half):
    partner = jnp.where(first_half, pltpu.roll(x, LANES - ROT_DIM // 2, 1),
                        pltpu.roll(x, ROT_DIM // 2, 1))
    return x * cos + partner * sin


def _proj_kernel(x_ref, g_ref, w_ref, cos_ref, sin_ref, o_ref, *, rope_chunks, tc):
    ncols = w_ref.shape[1]
    x = x_ref[...]
    ms = jnp.mean(x * x, axis=-1, keepdims=True)
    h = (x * lax.rsqrt(ms + NORM_EPS) * g_ref[...]).astype(BF16)
    lane = lax.broadcasted_iota(jnp.int32, (1, LANES), 1)
    first_half = (lane % HEAD_DIM) < (ROT_DIM // 2)
    cos = cos_ref[...]
    sin = sin_ref[...]
    for c0 in range(0, ncols, tc):
        res = jnp.dot(h, w_ref[:, c0:c0 + tc], preferred_element_type=F32)
        parts = []
        for cc in range(tc // LANES):
            xc = res[:, cc * LANES:(cc + 1) * LANES]
            if rope_chunks[c0 // LANES + cc]:
                xc = _rope_chunk(xc, cos, sin, first_half)
            parts.append(xc.astype(o_ref.dtype))
        o_ref[:, c0:c0 + tc] = jnp.concatenate(parts, axis=1)


def _norm_proj(x, g, w, cos, sin, *, tc, rope_chunks, tm=PROJ_TM):
    B, S, D = x.shape
    ncols = w.shape[1]
    return pl.pallas_call(
        functools.partial(_proj_kernel, rope_chunks=rope_chunks, tc=tc),
        grid=(B, S // tm),
        in_specs=[
            pl.BlockSpec((None, tm, D), lambda b, i: (b, i, 0)),
            pl.BlockSpec((1, D), lambda b, i: (0, 0)),
            pl.BlockSpec((D, ncols), lambda b, i: (0, 0)),
            pl.BlockSpec((None, None, tm, LANES), lambda b, i: (b, 0, i, 0)),
            pl.BlockSpec((None, None, tm, LANES), lambda b, i: (b, 0, i, 0)),
        ],
        out_specs=pl.BlockSpec((None, tm, ncols), lambda b, i: (b, i, 0)),
        out_shape=jax.ShapeDtypeStruct((B, S, ncols), BF16),
        compiler_params=_cparams(("parallel", "parallel")),
        name="norm_proj",
    )(x, g.reshape(1, D), w, cos, sin)


def _proj_a_kernel(x_ref, g_ref, w_ref, c1, s1, c4, s4, c16, s16, p4, p16, o1, o4, o16):
    tm = x_ref.shape[0]
    x = x_ref[...]
    ms = jnp.mean(x * x, axis=-1, keepdims=True)
    h1 = (x * lax.rsqrt(ms + NORM_EPS) * g_ref[...]).astype(BF16)
    lane = lax.broadcasted_iota(jnp.int32, (1, LANES), 1)
    first_half = (lane % HEAD_DIM) < (ROT_DIM // 2)
    ncols = o1.shape[-1]
    for gi, (dil, cos_ref, sin_ref, perm_ref, o_ref) in enumerate(
            ((1, c1, s1, None, o1), (4, c4, s4, p4, o4), (16, c16, s16, p16, o16))):
        n = tm // dil
        h = h1 if perm_ref is None else jnp.dot(perm_ref[...], h1, preferred_element_type=F32).astype(BF16)
        cos = cos_ref[...].reshape(tm, LANES)
        sin = sin_ref[...].reshape(tm, LANES)
        for c0 in range(0, ncols, D_MODEL):
            res = jnp.dot(h, w_ref[:, gi * ncols + c0:gi * ncols + c0 + D_MODEL], preferred_element_type=F32)
            parts = []
            for cc in range(D_MODEL // LANES):
                xc = res[:, cc * LANES:(cc + 1) * LANES]
                if _QKV_ROPE_CHUNKS[c0 // LANES + cc]:
                    xc = _rope_chunk(xc, cos, sin, first_half)
                parts.append(xc.astype(o_ref.dtype))
            val = jnp.concatenate(parts, axis=1)
            for r in range(dil):
                o_ref[r, :, c0:c0 + D_MODEL] = val[r * n:(r + 1) * n]


def _norm_proj_a(x, g, w, tabs, tm=PROJ_TM):
    B, S, D = x.shape
    ncols = 3 * D
    in_specs = [pl.BlockSpec((None, tm, D), lambda b, i: (b, i, 0)),
                pl.BlockSpec((1, D), lambda b, i: (0, 0)),
                pl.BlockSpec((D, 3 * ncols), lambda b, i: (0, 0), pipeline_mode=pl.Buffered(1))]
    args = [x, g.reshape(1, D), w]
    for dil in DILATIONS:
        n = tm // dil
        for tab in tabs[dil]:
            in_specs.append(pl.BlockSpec((None, dil, n, LANES), lambda b, i: (b, 0, i, 0)))
            args.append(tab)
    for dil in DILATIONS[1:]:
        n = tm // dil
        p = jnp.arange(tm)
        src = (p % n) * dil + p // n
        args.append((src[:, None] == jnp.arange(tm)[None, :]).astype(BF16))
        in_specs.append(pl.BlockSpec((tm, tm), lambda b, i: (0, 0)))
    return pl.pallas_call(
        _proj_a_kernel,
        grid=(B, S // tm),
        in_specs=in_specs,
        out_specs=[pl.BlockSpec((None, dil, tm // dil, ncols), lambda b, i: (b, 0, i, 0)) for dil in DILATIONS],
        out_shape=[jax.ShapeDtypeStruct((B, dil, S // dil, ncols), BF16) for dil in DILATIONS],
        compiler_params=_cparams(("parallel", "parallel")),
        name="norm_proj_a",
    )(*args)


def _fill_band_bias(bias_scr, tq, W, half):
    kk = lax.broadcasted_iota(jnp.int32, (W, 2 * tq), 0)
    qq = lax.broadcasted_iota(jnp.int32, (W, 2 * tq), 1) % tq
    for i in range(3):
        ok = jnp.abs(qq + i * half - kk) <= half
        bias_scr[i] = jnp.where(ok, 0.0, -jnp.inf).astype(F32)


def _band_scores(q2, kw, bias):
    lane_lo = lax.broadcasted_iota(jnp.int32, (1, LANES), 1) < HEAD_DIM
    zero = jnp.zeros_like(q2)
    qq = jnp.concatenate([jnp.where(lane_lo, q2, zero), jnp.where(lane_lo, zero, q2)], axis=0)
    qq = qq * jnp.asarray(HEAD_DIM ** -0.5, q2.dtype)
    return lax.dot_general(kw, qq, (((1,), (1,)), ((), ())), preferred_element_type=F32) + bias


def _band_finish(st, vw, want_lse, sink_row=None):
    tq = st.shape[1] // 2
    row_lo = lax.broadcasted_iota(jnp.int32, (LANES, 1), 0) < HEAD_DIM
    m = jnp.max(st, axis=0, keepdims=True)
    if sink_row is not None:
        m = jnp.maximum(m, sink_row)
    p = jnp.exp(st - m)
    l = jnp.sum(p, axis=0, keepdims=True)
    if sink_row is not None:
        l = l + jnp.exp(sink_row - m)
    ot = lax.dot_general(vw, p.astype(BF16), (((0,), (0,)), ((), ())), preferred_element_type=F32) / l
    o = jnp.where(row_lo, ot[:, :tq], ot[:, tq:]).T
    if not want_lse:
        return o, None
    lse_row = m + jnp.log(l)
    lse = jnp.where(row_lo, jnp.broadcast_to(lse_row[:, :tq], (LANES, tq)),
                    jnp.broadcast_to(lse_row[:, tq:], (LANES, tq))).T
    return o, lse


def _attn_a_kernel(*refs, TB, S):
    qkv = refs[:9]
    o_ref = refs[9]
    scr = refs[10:16]
    bias_scr = refs[16]
    t = pl.program_id(2)
    W = 4 * A_HALF

    @pl.when(t == 0)
    def _():
        _fill_band_bias(bias_scr, Q_SUB, W, A_HALF)

    subs = []
    for g, dil in enumerate(DILATIONS):
        nq = TB // dil
        tq = min(nq, Q_SUB)
        for r in range(dil):
            for jj in range(nq // tq):
                subs.append((g, dil, nq, tq, r, jj))

    def scores(sub):
        g, dil, nq, tq, r, jj = sub
        q_ref, k_ref = qkv[3 * g], qkv[3 * g + 1]
        L = S // dil
        qs = t * nq + jj * tq
        ws = pl.multiple_of(jnp.clip(qs - A_HALF, 0, L - W), A_HALF)
        q2 = q_ref[r, jj * tq:(jj + 1) * tq, :]
        return _band_scores(q2, k_ref[r, pl.ds(ws, W), :], bias_scr[(qs - ws) // A_HALF]), ws

    def finish(sub, st, ws):
        g, dil, nq, tq, r, jj = sub
        o_scr, l_scr = scr[2 * g], scr[2 * g + 1]
        o, lse = _band_finish(st, qkv[3 * g + 2][r, pl.ds(ws, W), :], True)
        row0 = jj * (tq * dil) + r
        if dil == 1:
            o_scr[row0:row0 + tq, :] = o
            l_scr[row0:row0 + tq, :] = lse
        else:
            o_scr[pl.ds(row0, tq, stride=dil), :] = o
            l_scr[pl.ds(row0, tq, stride=dil), :] = lse

    pend = [scores(sub) for sub in subs[:A_GROUP]]
    for n, sub in enumerate(subs):
        if n + A_GROUP < len(subs):
            pend.append(scores(subs[n + A_GROUP]))
        st, ws = pend[n]
        finish(sub, st, ws)

    l0, l1, l2 = scr[1][...], scr[3][...], scr[5][...]
    mx = jnp.maximum(jnp.maximum(l0, l1), l2)
    e0, e1, e2 = jnp.exp(l0 - mx), jnp.exp(l1 - mx), jnp.exp(l2 - mx)
    den = e0 + e1 + e2
    o = (e0 * scr[0][...] + e1 * scr[2][...] + e2 * scr[4][...]) / den
    o_ref[...] = o.astype(o_ref.dtype)


def _attn_a(qkvs, B, S, TB=ATTN_A_TB):
    n_pairs = (16 * HEAD_DIM) // LANES
    in_specs, args = [], []
    for g, dil in enumerate(DILATIONS):
        L = S // dil
        in_specs += [
            pl.BlockSpec((None, dil, TB // dil, LANES), lambda b, hp, t: (b, 0, t, hp)),
            pl.BlockSpec((None, dil, L, LANES), lambda b, hp, t: (b, 0, 0, n_pairs + hp)),
            pl.BlockSpec((None, dil, L, LANES), lambda b, hp, t: (b, 0, 0, 2 * n_pairs + hp)),
        ]
        args += [qkvs[g]] * 3
    return pl.pallas_call(
        functools.partial(_attn_a_kernel, TB=TB, S=S),
        grid=(B, n_pairs, S // TB),
        in_specs=in_specs,
        out_specs=pl.BlockSpec((None, TB, LANES), lambda b, hp, t: (b, t, hp)),
        out_shape=jax.ShapeDtypeStruct((B, S, n_pairs * LANES), BF16),
        scratch_shapes=[pltpu.VMEM((TB, LANES), F32)] * 6
        + [pltpu.VMEM((3, Q_SUB + 2 * A_HALF, 2 * Q_SUB), F32)],
        compiler_params=_cparams(("parallel", "parallel", "arbitrary")),
        name="attn_dilated",
    )(*args)


def _attn_c_kernel(q_ref, k_ref, v_ref, sink_ref, o_ref, bias_scr, *, TB, S):
    t = pl.program_id(1)
    tq = Q_SUB
    W = tq + 2 * C_HALF
    n_pairs = q_ref.shape[1] // LANES

    @pl.when(t == 0)
    def _():
        _fill_band_bias(bias_scr, tq, W, C_HALF)

    units = [(jj, p) for jj in range(TB // tq) for p in range(n_pairs)]

    def window(jj):
        qs = t * TB + jj * tq
        ws = pl.multiple_of(jnp.clip(qs - C_HALF, 0, S - W), C_HALF)
        return qs, ws

    def scores(unit):
        jj, p = unit
        qs, ws = window(jj)
        q2 = q_ref[jj * tq:(jj + 1) * tq, p * LANES:(p + 1) * LANES]
        return _band_scores(q2, k_ref[pl.ds(ws, W), :], bias_scr[(qs - ws) // C_HALF])

    pend = [scores(u) for u in units[:C_AHEAD]]
    for n, (jj, p) in enumerate(units):
        if n + C_AHEAD < len(units):
            pend.append(scores(units[n + C_AHEAD]))
        _, ws = window(jj)
        sink_row = sink_ref[p:p + 1, :]
        o, _ = _band_finish(pend[n], v_ref[pl.ds(ws, W), :], False, sink_row=sink_row)
        pend[n] = None
        o_ref[jj * tq:(jj + 1) * tq, p * LANES:(p + 1) * LANES] = o.astype(o_ref.dtype)


def _attn_c(qkv, sink_tab, B, S, TB=ATTN_C_TB):
    nq = 16 * HEAD_DIM
    return pl.pallas_call(
        functools.partial(_attn_c_kernel, TB=TB, S=S),
        grid=(B, S // TB),
        in_specs=[
            pl.BlockSpec((None, TB, nq), lambda b, t: (b, t, 0)),
            pl.BlockSpec((None, S, LANES), lambda b, t: (b, 0, nq // LANES)),
            pl.BlockSpec((None, S, LANES), lambda b, t: (b, 0, nq // LANES + 1)),
            pl.BlockSpec((8, 2 * LANES), lambda b, t: (0, 0)),
        ],
        out_specs=pl.BlockSpec((None, TB, nq), lambda b, t: (b, t, 0)),
        out_shape=jax.ShapeDtypeStruct((B, S, nq), BF16),
        scratch_shapes=[pltpu.VMEM((3, Q_SUB + 2 * C_HALF, 2 * Q_SUB), F32)],
        compiler_params=_cparams(("parallel", "arbitrary")),
        name="attn_swa_sink",
    )(qkv, qkv, qkv, sink_tab)


def _attn_b_kernel(*refs, lambda_init, n_cast):
    lam_ref, q_ref, qn_ref, k_ref, kn_ref, v_ref, subln_ref = refs[:7]
    cast_in = refs[7:7 + n_cast]
    o_ref = refs[7 + n_cast]
    cast_out = refs[8 + n_cast:8 + 2 * n_cast]
    s0_scr, s1_scr, m_scr, l_scr, acc_scr = refs[8 + 2 * n_cast:]
    _attn_b_body(lam_ref, q_ref, qn_ref, k_ref, kn_ref, v_ref, subln_ref, o_ref,
                 s0_scr, s1_scr, m_scr, l_scr, acc_scr, lambda_init=lambda_init)
    for src_ref, dst_ref in zip(cast_in, cast_out):
        dst_ref[...] = src_ref[...].astype(dst_ref.dtype)


def _attn_b_body(lam_ref, q_ref, qn_ref, k_ref, kn_ref, v_ref, subln_ref, o_ref,
                 s0_scr, s1_scr, m_scr, l_scr, acc_scr, *, lambda_init):
    i = pl.program_id(2)
    kv = pl.program_id(3)
    nkv = pl.num_programs(3)
    tk = k_ref.shape[0]
    lane = lax.broadcasted_iota(jnp.int32, (1, LANES), 1)
    lane_lo = lane < HEAD_DIM

    def scores(kref, qref, c, j):
        k = kref[j * CK:(j + 1) * CK, :] * jnp.asarray(HEAD_DIM ** -0.5, kref.dtype)
        sel = lane_lo if c == 0 else jnp.logical_not(lane_lo)
        kc = jnp.where(sel, k, jnp.zeros_like(k))
        return lax.dot_general(kc, qref[...], (((1,), (1,)), ((), ())), preferred_element_type=F32)

    @pl.when(kv == 0)
    def _():
        m_scr[...] = jnp.full(m_scr.shape, -jnp.inf, F32)
        l_scr[...] = jnp.zeros(l_scr.shape, F32)
        acc_scr[...] = jnp.zeros(acc_scr.shape, F32)

    @pl.when(jnp.logical_and(i == 0, kv == 0))
    def _():
        for c in range(2):
            for j in range(tk // CK):
                s0_scr[c * tk + j * CK:c * tk + (j + 1) * CK, :] = scores(k_ref, q_ref, c, j)

    def step(cur_scr, nxt_scr):
        vt = jnp.concatenate([v_ref[...].T, jnp.ones((BF16_SUBLANES, tk), BF16)], axis=0)
        for c in range(2):
            m, l, acc = m_scr[c], l_scr[c], acc_scr[c]
            for j in range(tk // CK):
                rows = slice(c * tk + j * CK, c * tk + (j + 1) * CK)
                nxt_scr[rows, :] = scores(kn_ref, qn_ref, c, j)
                s = cur_scr[rows, :]
                m_new = jnp.maximum(m, jnp.max(s, axis=0, keepdims=True))
                alpha = jnp.exp(m - m_new)
                p = jnp.exp(s - m_new).astype(BF16)
                pv = jnp.dot(vt[:, j * CK:(j + 1) * CK], p, preferred_element_type=F32)
                l = alpha * l + pv[LANES:LANES + 1]
                acc = alpha * acc + pv[:LANES]
                m = m_new
            m_scr[c], l_scr[c], acc_scr[c] = m, l, acc

    @pl.when(kv % 2 == 0)
    def _():
        step(s0_scr, s1_scr)

    @pl.when(kv % 2 == 1)
    def _():
        step(s1_scr, s0_scr)

    @pl.when(kv == nkv - 1)
    def _():
        lp = lam_ref[...]
        lam = (jnp.exp(jnp.sum(lp[0:1] * lp[1:2], axis=-1, keepdims=True))
               - jnp.exp(jnp.sum(lp[2:3] * lp[3:4], axis=-1, keepdims=True)) + lambda_init)
        o = acc_scr[0] / l_scr[0] - lam * (acc_scr[1] / l_scr[1])
        ms = jnp.mean(o * o, axis=0, keepdims=True)
        o = o * lax.rsqrt(ms + SUBLN_EPS) * subln_ref[...] * (1.0 - lambda_init)
        o_ref[...] = o.T.astype(o_ref.dtype)


def _attn_b(qkv, lam_params, subln, lambda_init, B, S, side_casts=(), tq=ATTN_B_TQ, tk=ATTN_B_TK):
    H = 8
    nq, nkv = S // tq, S // tk
    assert nkv % 2 == 0
    nsteps = B * H * nq * nkv
    cast_specs, cast_shapes = [], []
    for w in side_casts:
        rows, cols = w.shape
        nblk = nsteps
        while rows % nblk or (rows // nblk) % BF16_SUBLANES:
            nblk //= 2
        per = nsteps // nblk

        def w_map(b, h, i, kv, per=per):
            return ((((b * H + h) * nq + i) * nkv + kv) // per, 0)

        cast_specs.append(pl.BlockSpec((rows // nblk, cols), w_map))
        cast_shapes.append(jax.ShapeDtypeStruct((rows, cols), BF16))

    def q_next(b, h, i, kv):
        return (b, jnp.minimum(i + (kv + 1) // nkv, nq - 1), h)

    def k_next(b, h, i, kv):
        return (b, (kv + 1) % nkv, H + h)

    outs = pl.pallas_call(
        functools.partial(_attn_b_kernel, lambda_init=lambda_init, n_cast=len(side_casts)),
        grid=(B, H, nq, nkv),
        in_specs=[
            pl.BlockSpec((4, HEAD_DIM), lambda b, h, i, kv: (0, 0)),
            pl.BlockSpec((None, tq, LANES), lambda b, h, i, kv: (b, i, h)),
            pl.BlockSpec((None, tq, LANES), q_next),
            pl.BlockSpec((None, tk, LANES), lambda b, h, i, kv: (b, kv, H + h)),
            pl.BlockSpec((None, tk, LANES), k_next),
            pl.BlockSpec((None, tk, LANES), lambda b, h, i, kv: (b, kv, 2 * H + h)),
            pl.BlockSpec((LANES, 1), lambda b, h, i, kv: (0, 0)),
        ] + cast_specs,
        out_specs=[pl.BlockSpec((None, tq, LANES), lambda b, h, i, kv: (b, i, h))] + cast_specs,
        out_shape=[jax.ShapeDtypeStruct((B, S, H * LANES), BF16)] + cast_shapes,
        scratch_shapes=[pltpu.VMEM((2 * tk, tq), F32), pltpu.VMEM((2 * tk, tq), F32),
                        pltpu.VMEM((2, 1, tq), F32), pltpu.VMEM((2, 1, tq), F32),
                        pltpu.VMEM((2, LANES, tq), F32)],
        compiler_params=_cparams(("parallel", "parallel", "arbitrary", "arbitrary")),
        name="attn_diff",
    )(lam_params, qkv, qkv, qkv, qkv, qkv, subln.reshape(LANES, 1), *side_casts)
    return outs[0], outs[1:]


def _ffn_kernel(x_ref, o_ref, wo_ref, g_ref, w1_ref, w3_ref, w2_ref, y_ref):
    x = x_ref[...] + jnp.dot(o_ref[...], wo_ref[...], preferred_element_type=F32)
    ms = jnp.mean(x * x, axis=-1, keepdims=True)
    h = (x * lax.rsqrt(ms + NORM_EPS) * g_ref[...]).astype(BF16)
    u = jnp.dot(h, w1_ref[...], preferred_element_type=F32)
    v = jnp.dot(h, w3_ref[...], preferred_element_type=F32)
    a = (u * jax.nn.sigmoid(u) * v).astype(BF16)
    y_ref[...] = x + jnp.dot(a, w2_ref[...], preferred_element_type=F32)


def _ffn_dense(x2, o2, wo, g, w1, w3, w2, tm=FFN_TM):
    T, D = x2.shape
    K = o2.shape[1]
    F = w1.shape[1]
    return pl.pallas_call(
        _ffn_kernel,
        grid=(T // tm,),
        in_specs=[pl.BlockSpec((tm, D), lambda i: (i, 0)),
                  pl.BlockSpec((tm, K), lambda i: (i, 0)),
                  pl.BlockSpec((K, D), lambda i: (0, 0)),
                  pl.BlockSpec((1, D), lambda i: (0, 0)),
                  pl.BlockSpec((D, F), lambda i: (0, 0)),
                  pl.BlockSpec((D, F), lambda i: (0, 0)),
                  pl.BlockSpec((F, D), lambda i: (0, 0))],
        out_specs=pl.BlockSpec((tm, D), lambda i: (i, 0)),
        out_shape=jax.ShapeDtypeStruct((T, D), F32),
        compiler_params=_cparams(("parallel",)),
        name="ffn_dense",
    )(x2, o2, wo, g.reshape(1, D), w1, w3, w2)


ROW_TILE = F32_SUBLANES


def _rows_to_tiles(ref, val):
    n = val.shape[0]
    for a in range(ROW_TILE):
        ref[pl.ds(a, n, stride=ROW_TILE), :] = val[:, a * LANES:(a + 1) * LANES]


def _tiles_to_rows(ref, n):
    return jnp.concatenate([ref[pl.ds(a, n, stride=ROW_TILE), :] for a in range(ROW_TILE)], axis=1)


def _router_kernel(x_ref, o_ref, wo_ref, g_ref, r_ref, x1_ref, h_ref, mi_ref, mf_ref, cnt_ref, st_ref, carry_scr):
    i = pl.program_id(0)
    tm = x_ref.shape[0]

    @pl.when(i == 0)
    def _():
        carry_scr[...] = jnp.zeros(carry_scr.shape, F32)

    x = x_ref[...] + jnp.dot(o_ref[...], wo_ref[...], preferred_element_type=F32)
    x1_ref[...] = x
    ms = jnp.mean(x * x, axis=-1, keepdims=True)
    h = x * lax.rsqrt(ms + NORM_EPS) * g_ref[...]
    _rows_to_tiles(h_ref, h)
    lane = lax.broadcasted_iota(jnp.int32, (tm, LANES), 1)
    logits = jnp.full((tm, LANES), -jnp.inf, F32)
    for e in range(N_EXPERTS):
        col = jnp.sum(h * r_ref[e:e + 1, :], axis=-1, keepdims=True)
        logits = jnp.where(lane == e, col, logits)
    v0 = jnp.max(logits, axis=-1, keepdims=True)
    i0 = jnp.min(jnp.where(logits == v0, lane, LANES), axis=-1, keepdims=True)
    rest = jnp.where(lane == i0, -jnp.inf, logits)
    v1 = jnp.max(rest, axis=-1, keepdims=True)
    i1 = jnp.min(jnp.where(rest == v1, lane, LANES), axis=-1, keepdims=True)
    tt = jnp.exp(v1 - v0)
    g0 = 1.0 / (1.0 + tt)
    g1 = tt / (1.0 + tt)
    sel0 = lane == i0
    sel1 = lane == i1
    onehot = jnp.where(jnp.logical_or(sel0, sel1), 1.0, 0.0)
    row = lax.broadcasted_iota(jnp.int32, (tm, tm), 0)
    col = lax.broadcasted_iota(jnp.int32, (tm, tm), 1)
    tri = jnp.where(row > col, 1.0, 0.0).astype(BF16)
    prefix = jnp.dot(tri, onehot.astype(BF16), preferred_element_type=F32)
    before = carry_scr[...] + prefix
    rank0 = jnp.sum(jnp.where(sel0, before, 0.0), axis=-1, keepdims=True)
    rank1 = jnp.sum(jnp.where(sel1, before, 0.0), axis=-1, keepdims=True)
    tile_cnt = jnp.sum(onehot, axis=0, keepdims=True)
    lane1 = lane[0:1]
    cum = jnp.zeros((1, LANES), F32)
    for e in range(N_EXPERTS - 1):
        cum = cum + jnp.where(lane1 > e, tile_cnt[:, e:e + 1], 0.0)
    loc0 = jnp.sum(jnp.where(sel0, cum + prefix, 0.0), axis=-1, keepdims=True)
    loc1 = jnp.sum(jnp.where(sel1, cum + prefix, 0.0), axis=-1, keepdims=True)
    row8 = lax.broadcasted_iota(jnp.int32, (F32_SUBLANES, LANES), 0)
    st_ref[...] = jnp.where(row8 == 0, carry_scr[...], jnp.where(row8 == 1, tile_cnt, jnp.where(row8 == 2, cum, 0.0)))
    carry_scr[...] = carry_scr[...] + tile_cnt
    mi = jnp.where(lane == 0, i0, jnp.where(lane == 1, i1, 0))
    mi = jnp.where(lane == 2, rank0.astype(jnp.int32), jnp.where(lane == 3, rank1.astype(jnp.int32), mi))
    mi = jnp.where(lane == 4, loc0.astype(jnp.int32), jnp.where(lane == 5, loc1.astype(jnp.int32), mi))
    mi_ref[...] = mi
    mf_ref[...] = jnp.where(lane == 0, g0, jnp.where(lane == 1, g1, 0.0))
    cnt_ref[...] = carry_scr[...]


def _router(x2, o2, wo, g, router, tm=MOE_TM):
    T, D = x2.shape
    K = o2.shape[1]
    rt = router.astype(F32).T
    return pl.pallas_call(
        _router_kernel,
        grid=(T // tm,),
        in_specs=[pl.BlockSpec((tm, D), lambda i: (i, 0)),
                  pl.BlockSpec((tm, K), lambda i: (i, 0)),
                  pl.BlockSpec((K, D), lambda i: (0, 0)),
                  pl.BlockSpec((1, D), lambda i: (0, 0)),
                  pl.BlockSpec((N_EXPERTS, D), lambda i: (0, 0))],
        out_specs=[pl.BlockSpec((tm, D), lambda i: (i, 0)),
                   pl.BlockSpec((tm * ROW_TILE, LANES), lambda i: (i, 0)),
                   pl.BlockSpec((tm, LANES), lambda i: (i, 0)),
                   pl.BlockSpec((tm, LANES), lambda i: (i, 0)),
                   pl.BlockSpec((1, LANES), lambda i: (0, 0)),
                   pl.BlockSpec((F32_SUBLANES, LANES), lambda i: (i, 0))],
        out_shape=[jax.ShapeDtypeStruct((T, D), F32),
                   jax.ShapeDtypeStruct((T * ROW_TILE, LANES), F32),
                   jax.ShapeDtypeStruct((T, LANES), jnp.int32),
                   jax.ShapeDtypeStruct((T, LANES), F32),
                   jax.ShapeDtypeStruct((1, LANES), F32),
                   jax.ShapeDtypeStruct((T // tm * F32_SUBLANES, LANES), F32)],
        scratch_shapes=[pltpu.VMEM((1, LANES), F32)],
        compiler_params=_cparams(("arbitrary",)),
        name="moe_router",
    )(x2, o2, wo, g.reshape(1, D), rt)


INFO_LEN = 4 * MOE_TM
SIZE_BITS = tuple(range(MOE_TM.bit_length() - 1, -1, -1))


def _range_copies(iref, tm, hbm_ref, vmem_ref, sem, to_hbm, wait):
    for e in range(N_EXPERTS):
        start = iref[2 * tm + e]
        cnt = iref[2 * tm + N_EXPERTS + e]
        cum = iref[2 * tm + 2 * N_EXPERTS + e]
        for bit in SIZE_BITS:
            size = 1 << bit

            @pl.when((cnt >> bit) & 1 == 1)
            def _(bit=bit, size=size, start=start, cnt=cnt, cum=cum):
                done = (cnt >> (bit + 1)) << (bit + 1)
                far = hbm_ref.at[pl.ds(pl.multiple_of((start + done) * ROW_TILE, ROW_TILE), size * ROW_TILE)]
                near = vmem_ref.at[pl.ds(pl.multiple_of((cum + done) * ROW_TILE, ROW_TILE), size * ROW_TILE)]
                cp = pltpu.make_async_copy(near, far, sem) if to_hbm else pltpu.make_async_copy(far, near, sem)
                cp.wait() if wait else cp.start()


def _dispatch_kernel(info_ref, iprev_ref, h_ref, buf_in, buf_hbm, comp_scr, sems, *, tm):
    del buf_in
    i = pl.program_id(0)
    last = pl.num_programs(0) - 1

    def tile(par):
        def place(n, carry):
            row = h_ref[pl.ds(pl.multiple_of(n * ROW_TILE, ROW_TILE), ROW_TILE), :]
            comp_scr[par, pl.ds(pl.multiple_of(info_ref[2 * n] * ROW_TILE, ROW_TILE), ROW_TILE), :] = row
            comp_scr[par, pl.ds(pl.multiple_of(info_ref[2 * n + 1] * ROW_TILE, ROW_TILE), ROW_TILE), :] = row
            return carry

        lax.fori_loop(0, tm, place, 0)
        _range_copies(info_ref, tm, buf_hbm, comp_scr.at[par], sems.at[par], True, False)

        @pl.when(i >= 1)
        def _():
            _range_copies(iprev_ref, tm, buf_hbm, comp_scr.at[1 - par], sems.at[1 - par], True, True)

        @pl.when(i == last)
        def _():
            _range_copies(info_ref, tm, buf_hbm, comp_scr.at[par], sems.at[par], True, True)

    for par in range(2):
        @pl.when(i % 2 == par)
        def _(par=par):
            tile(par)


def _dispatch(h2, info, n_rows, buf0, tm=MOE_TM):
    T = h2.shape[0] // ROW_TILE
    return pl.pallas_call(
        functools.partial(_dispatch_kernel, tm=tm),
        grid=(T // tm,),
        in_specs=[pl.BlockSpec((INFO_LEN,), lambda i: (i,), memory_space=pltpu.SMEM),
                  pl.BlockSpec((INFO_LEN,), lambda i: (jnp.maximum(i - 1, 0),), memory_space=pltpu.SMEM),
                  pl.BlockSpec((tm * ROW_TILE, LANES), lambda i: (i, 0)),
                  pl.BlockSpec(memory_space=pl.ANY)],
        out_specs=pl.BlockSpec(memory_space=pl.ANY),
        out_shape=jax.ShapeDtypeStruct((n_rows * ROW_TILE, LANES), F32),
        scratch_shapes=[pltpu.VMEM((2, 2 * tm * ROW_TILE, LANES), F32), pltpu.SemaphoreType.DMA((2,))],
        input_output_aliases={3: 0},
        compiler_params=_cparams(("arbitrary",)),
        name="moe_dispatch",
    )(info, info, h2, buf0)


def _moe_ffn_kernel(be_ref, nu_ref, x_ref, w1_ref, w3_ref, w2_ref, y_ref, acc_scr):
    b = pl.program_id(0)
    f = pl.program_id(1)

    @pl.when(b < nu_ref[0])
    def _():
        @pl.when(f == 0)
        def _():
            acc_scr[...] = jnp.zeros(acc_scr.shape, F32)

        h = _tiles_to_rows(x_ref, MOE_BLOCK).astype(BF16)
        u = jnp.dot(h, w1_ref[...], preferred_element_type=F32)
        v = jnp.dot(h, w3_ref[...], preferred_element_type=F32)
        a = (u * jax.nn.sigmoid(u) * v).astype(BF16)
        acc_scr[...] += jnp.dot(a, w2_ref[...], preferred_element_type=F32)

        @pl.when(f == pl.num_programs(1) - 1)
        def _():
            _rows_to_tiles(y_ref, acc_scr[...])

    @pl.when(jnp.logical_and(b >= nu_ref[0], f == 0))
    def _():
        y_ref[...] = jnp.zeros(y_ref.shape, F32)


def _moe_ffn(buf, block_expert, n_used, w1, w3, w2, tf=MOE_TF):
    R = buf.shape[0] // ROW_TILE
    E, D, F = w1.shape
    nb = R // MOE_BLOCK

    def last_live(b, nu):
        return jnp.maximum(jnp.minimum(b, nu[0] - 1), 0)

    def row_map(b, f, be, nu):
        return (last_live(b, nu), 0)

    def w_in_map(b, f, be, nu):
        live = b < nu[0]
        return (be[last_live(b, nu)], 0, jnp.where(live, f, F // tf - 1))

    def w_out_map(b, f, be, nu):
        live = b < nu[0]
        return (be[last_live(b, nu)], jnp.where(live, f, F // tf - 1), 0)

    return pl.pallas_call(
        _moe_ffn_kernel,
        grid_spec=pltpu.PrefetchScalarGridSpec(
            num_scalar_prefetch=2,
            grid=(nb, F // tf),
            in_specs=[pl.BlockSpec((MOE_BLOCK * ROW_TILE, LANES), row_map),
                      pl.BlockSpec((None, D, tf), w_in_map),
                      pl.BlockSpec((None, D, tf), w_in_map),
                      pl.BlockSpec((None, tf, D), w_out_map)],
            out_specs=pl.BlockSpec((MOE_BLOCK * ROW_TILE, LANES), lambda b, f, be, nu: (b, 0)),
            scratch_shapes=[pltpu.VMEM((MOE_BLOCK, D), F32)],
        ),
        out_shape=jax.ShapeDtypeStruct((R * ROW_TILE, LANES), F32),
        compiler_params=_cparams(("arbitrary", "arbitrary")),
        name="moe_ffn",
    )(block_expert, n_used, buf, w1, w3, w2)


def _combine_kernel(info_ref, inext_ref, x_ref, mf_ref, g_ref, ybuf_hbm, y_ref, comp_scr, r0_scr, r1_scr, sems,
                    *, tm, final_norm):
    i = pl.program_id(0)
    last = pl.num_programs(0) - 1

    def ranges(iref, par, wait):
        _range_copies(iref, tm, ybuf_hbm, comp_scr.at[par], sems.at[par], False, wait)

    def finish(par):
        ranges(info_ref, par, True)

        def pick(n, carry):
            rows = pl.ds(pl.multiple_of(n * ROW_TILE, ROW_TILE), ROW_TILE)
            j0 = pl.ds(pl.multiple_of(info_ref[2 * n] * ROW_TILE, ROW_TILE), ROW_TILE)
            j1 = pl.ds(pl.multiple_of(info_ref[2 * n + 1] * ROW_TILE, ROW_TILE), ROW_TILE)
            r0_scr[rows, :] = comp_scr[par, j0, :]
            r1_scr[rows, :] = comp_scr[par, j1, :]
            return carry

        lax.fori_loop(0, tm, pick, 0)
        mf = mf_ref[...]
        y = x_ref[...] + mf[:, 0:1] * _tiles_to_rows(r0_scr, tm) + mf[:, 1:2] * _tiles_to_rows(r1_scr, tm)
        if final_norm:
            ms = jnp.mean(y * y, axis=-1, keepdims=True)
            y = y * lax.rsqrt(ms + NORM_EPS) * g_ref[...]
        y_ref[...] = y

    @pl.when(i == 0)
    def _():
        ranges(info_ref, 0, False)

    for par in range(2):
        @pl.when(jnp.logical_and(i % 2 == par, i < last))
        def _(par=par):
            ranges(inext_ref, 1 - par, False)

        @pl.when(i % 2 == par)
        def _(par=par):
            finish(par)


def _combine(x2, mf, info, ybuf, g_final, final_norm, tm=MOE_TM):
    T, D = x2.shape
    nt = T // tm
    return pl.pallas_call(
        functools.partial(_combine_kernel, tm=tm, final_norm=final_norm),
        grid=(nt,),
        in_specs=[pl.BlockSpec((INFO_LEN,), lambda i: (i,), memory_space=pltpu.SMEM),
                  pl.BlockSpec((INFO_LEN,), lambda i: (jnp.minimum(i + 1, nt - 1),), memory_space=pltpu.SMEM),
                  pl.BlockSpec((tm, D), lambda i: (i, 0)),
                  pl.BlockSpec((tm, LANES), lambda i: (i, 0)),
                  pl.BlockSpec((1, D), lambda i: (0, 0)),
                  pl.BlockSpec(memory_space=pl.ANY)],
        out_specs=pl.BlockSpec((tm, D), lambda i: (i, 0)),
        out_shape=jax.ShapeDtypeStruct((T, D), F32),
        scratch_shapes=[pltpu.VMEM((2, 2 * tm * ROW_TILE, LANES), F32),
                        pltpu.VMEM((tm * ROW_TILE, LANES), F32), pltpu.VMEM((tm * ROW_TILE, LANES), F32),
                        pltpu.SemaphoreType.DMA((2,))],
        compiler_params=_cparams(("arbitrary",)),
        name="moe_combine",
    )(info, info, x2, mf, g_final.reshape(1, D), ybuf)


def _moe_layer(x2, o2, wo, g, router, w1, w3, w2, g_final, final_norm, buf0=None):
    T, D = x2.shape
    x2, h2, mi, mf, cnt, stats = _router(x2, o2, wo, g, router)
    counts = cnt[0, :N_EXPERTS].astype(jnp.int32)
    padded = ((counts + MOE_BLOCK - 1) // MOE_BLOCK) * MOE_BLOCK
    pend = jnp.cumsum(padded)
    pstart = pend - padded
    nb = -(-(2 * T + N_EXPERTS * (MOE_BLOCK - 1)) // MOE_BLOCK)
    block_start = jnp.arange(nb, dtype=jnp.int32) * MOE_BLOCK
    block_expert = jnp.minimum(jnp.sum(pend[None, :] <= block_start[:, None], axis=1),
                               N_EXPERTS - 1).astype(jnp.int32)
    n_used = (pend[-1:] // MOE_BLOCK).astype(jnp.int32)
    if buf0 is None:
        buf0 = jnp.zeros((nb * MOE_BLOCK * ROW_TILE, LANES), F32)
    nt = T // MOE_TM
    st = stats.reshape(nt, F32_SUBLANES, LANES)[:, :3, :N_EXPERTS].astype(jnp.int32)
    info = jnp.concatenate([mi[:, 4:6].reshape(nt, 2 * MOE_TM), pstart[None, :] + st[:, 0], st[:, 1], st[:, 2],
                            jnp.zeros((nt, INFO_LEN - 2 * MOE_TM - 3 * N_EXPERTS), jnp.int32)], axis=1).reshape(-1)
    buf = _dispatch(h2, info, nb * MOE_BLOCK, buf0)
    ybuf = _moe_ffn(buf, block_expert, n_used, w1, w3, w2)
    return _combine(x2, mf, info, ybuf, g_final, final_norm), ybuf


def _mixer_a(x, g, w_in, w_out, tabs):
    B, S, D = x.shape
    w = w_in.astype(BF16)
    qkvs = _norm_proj_a(x, g, w, tabs)
    o = _attn_a(qkvs, B, S)
    return o.reshape(B * S, D), w_out.astype(BF16)


def _mixer_b(x, g, w_in, lq1, lk1, lq2, lk2, subln, w_out, lambda_init, tabs, side_casts=()):
    B, S, D = x.shape
    cos, sin = tabs[1]
    qkv = _norm_proj(x, g, w_in.astype(BF16), cos, sin, tc=D, rope_chunks=_QKV_ROPE_CHUNKS)
    lam_params = jnp.stack([lq1, lk1, lq2, lk2]).astype(F32)
    o, casted = _attn_b(qkv, lam_params, subln.astype(F32), lambda_init, B, S, side_casts)
    return o.reshape(B * S, D), w_out.astype(BF16), casted


def _mixer_c(x, g, w_in, sink, w_out, tabs):
    B, S, D = x.shape
    cos, sin = tabs[1]
    head_order = jnp.arange(16).reshape(2, 8).T.reshape(-1)
    col_order = (head_order[:, None] * HEAD_DIM + jnp.arange(HEAD_DIM)[None, :]).reshape(-1)
    w_q = w_in[:, :D][:, col_order]
    w = jnp.concatenate([w_q, w_in[:, D:]], axis=1).astype(BF16)
    ncols = w.shape[1]
    chunks = tuple(c < (ncols // LANES - 1) for c in range(ncols // LANES))
    qkv = _norm_proj(x, g, w, cos, sin, tc=ncols, rope_chunks=chunks)
    sk = sink.astype(F32)
    sink_tab = jnp.concatenate([jnp.broadcast_to(sk[:8, None], (8, LANES)),
                                jnp.broadcast_to(sk[8:, None], (8, LANES))], axis=1)
    o = _attn_c(qkv, sink_tab, B, S)
    w_o = w_out[col_order, :].astype(BF16)
    return o.reshape(B * S, D), w_o


def kernel(x, positions, l0_norm_mix, l0_a_w_in, l0_a_w_out, l0_norm_ffn, l0_ffn_w1, l0_ffn_w3, l0_ffn_w2, l1_norm_mix, l1_b_w_in, l1_b_lambda_q1, l1_b_lambda_k1, l1_b_lambda_q2, l1_b_lambda_k2, l1_b_subln, l1_b_w_out, l1_norm_ffn, l1_moe_router, l1_moe_w1, l1_moe_w3, l1_moe_w2, l2_norm_mix, l2_c_w_in, l2_c_sink, l2_c_w_out, l2_norm_ffn, l2_ffn_w1, l2_ffn_w3, l2_ffn_w2, l3_norm_mix, l3_a_w_in, l3_a_w_out, l3_norm_ffn, l3_moe_router, l3_moe_w1, l3_moe_w3, l3_moe_w2, final_norm):
    B, S, D = x.shape
    T = B * S
    cos, sin = lax.optimization_barrier(_rope_tables(positions))
    tabs = {d: (_to_strided(cos, d), _to_strided(sin, d)) for d in DILATIONS}

    o, wo = _mixer_a(x, l0_norm_mix, l0_a_w_in, l0_a_w_out, tabs)
    x = _ffn_dense(x.reshape(T, D), o, wo, l0_norm_ffn, l0_ffn_w1.astype(BF16), l0_ffn_w3.astype(BF16),
                   l0_ffn_w2.astype(BF16)).reshape(B, S, D)
    lambda_init = 0.8 - 0.6 * math.exp(-0.3 * 1)
    moe_w = (l1_moe_w1, l1_moe_w3, l1_moe_w2, l3_moe_w1, l3_moe_w3, l3_moe_w2)
    o, wo, moe_bf = _mixer_b(x, l1_norm_mix, l1_b_w_in, l1_b_lambda_q1, l1_b_lambda_k1, l1_b_lambda_q2,
                             l1_b_lambda_k2, l1_b_subln, l1_b_w_out, lambda_init, tabs,
                             side_casts=tuple(w.reshape(-1, w.shape[-1]) for w in moe_w))
    moe_bf = [wb.reshape(w.shape) for wb, w in zip(moe_bf, moe_w)]
    x, spare = _moe_layer(x.reshape(T, D), o, wo, l1_norm_ffn, l1_moe_router, moe_bf[0], moe_bf[1], moe_bf[2],
                          final_norm, False)
    x = x.reshape(B, S, D)
    o, wo = _mixer_c(x, l2_norm_mix, l2_c_w_in, l2_c_sink, l2_c_w_out, tabs)
    x = _ffn_dense(x.reshape(T, D), o, wo, l2_norm_ffn, l2_ffn_w1.astype(BF16), l2_ffn_w3.astype(BF16),
                   l2_ffn_w2.astype(BF16)).reshape(B, S, D)
    o, wo = _mixer_a(x, l3_norm_mix, l3_a_w_in, l3_a_w_out, tabs)
    x, _ = _moe_layer(x.reshape(T, D), o, wo, l3_norm_ffn, l3_moe_router, moe_bf[3], moe_bf[4], moe_bf[5],
                      final_norm, True, buf0=spare)
    return x.reshape(B, S, D)
```

```python
import functools
import math

import jax
import jax.numpy as jnp
from jax import lax
from jax.experimental import pallas as pl
from jax.experimental.pallas import tpu as pltpu

F32 = jnp.float32
BF16 = jnp.bfloat16

D_MODEL = 1024
HEAD_DIM = 64
ROT_DIM = HEAD_DIM // 4
ROPE_THETA = 500000.0
NORM_EPS = 1e-6
SUBLN_EPS = 1e-5
DILATIONS = (1, 4, 16)
A_HALF = 64
C_HALF = 128
N_EXPERTS = 8
_QKV_ROPE_CHUNKS = (True,) * 16 + (False,) * 8

LANES = 128
F32_SUBLANES = 8
BF16_SUBLANES = 16
VMEM_LIMIT = 56 * 1024 * 1024

PROJ_TM = 512
Q_SUB = LANES
ATTN_A_TB = 2048
ATTN_C_TB = 1024
ATTN_B_TQ = 1024
ATTN_B_TK = 2048
CK = 512
A_GROUP = 3
C_AHEAD = 4
FFN_TM = 512
MOE_TM = 512
MOE_BLOCK = 512
MOE_TF = 1792


def _cparams(sem):
    return pltpu.CompilerParams(dimension_semantics=sem, vmem_limit_bytes=VMEM_LIMIT)


def _rope_tables(positions):
    lane = jnp.arange(LANES) % HEAD_DIM
    half = ROT_DIM // 2
    inv_freq = ROPE_THETA ** (-((2 * (lane % half)).astype(F32) / ROT_DIM))
    freq = jnp.where(lane < ROT_DIM, inv_freq, 0.0)
    sign = jnp.where(lane < half, -1.0, jnp.where(lane < ROT_DIM, 1.0, 0.0))
    ang = positions.astype(F32)[..., None] * freq
    return jnp.cos(ang), jnp.sin(ang) * sign


def _to_strided(t, d):
    B, S, C = t.shape
    return jnp.swapaxes(t.reshape(B, S // d, d, C), 1, 2)


def _rope_chunk(x, cos, sin, first_half):
    partner = jnp.where(first_half, pltpu.roll(x, LANES - ROT_DIM // 2, 1),
                        pltpu.roll(x, ROT_DIM // 2, 1))
    return x * cos + partner * sin


def _proj_kernel(x_ref, g_ref, w_ref, cos_ref, sin_ref, o_ref, *, rope_chunks, tc):
    ncols = w_ref.shape[1]
    x = x_ref[...]
    ms = jnp.mean(x * x, axis=-1, keepdims=True)
    h = (x * lax.rsqrt(ms + NORM_EPS) * g_ref[...]).astype(BF16)
    lane = lax.broadcasted_iota(jnp.int32, (1, LANES), 1)
    first_half = (lane % HEAD_DIM) < (ROT_DIM // 2)
    cos = cos_ref[...]
    sin = sin_ref[...]
    for c0 in range(0, ncols, tc):
        res = jnp.dot(h, w_ref[:, c0:c0 + tc], preferred_element_type=F32)
        parts = []
        for cc in range(tc // LANES):
            xc = res[:, cc * LANES:(cc + 1) * LANES]
            if rope_chunks[c0 // LANES + cc]:
                xc = _rope_chunk(xc, cos, sin, first_half)
            parts.append(xc.astype(o_ref.dtype))
        o_ref[:, c0:c0 + tc] = jnp.concatenate(parts, axis=1)


def _norm_proj(x, g, w, cos, sin, *, tc, rope_chunks, tm=PROJ_TM):
    B, S, D = x.shape
    ncols = w.shape[1]
    return pl.pallas_call(
        functools.partial(_proj_kernel, rope_chunks=rope_chunks, tc=tc),
        grid=(B, S // tm),
        in_specs=[
            pl.BlockSpec((None, tm, D), lambda b, i: (b, i, 0)),
            pl.BlockSpec((1, D), lambda b, i: (0, 0)),
            pl.BlockSpec((D, ncols), lambda b, i: (0, 0)),
            pl.BlockSpec((None, None, tm, LANES), lambda b, i: (b, 0, i, 0)),
            pl.BlockSpec((None, None, tm, LANES), lambda b, i: (b, 0, i, 0)),
        ],
        out_specs=pl.BlockSpec((None, tm, ncols), lambda b, i: (b, i, 0)),
        out_shape=jax.ShapeDtypeStruct((B, S, ncols), BF16),
        compiler_params=_cparams(("parallel", "parallel")),
        name="norm_proj",
    )(x, g.reshape(1, D), w, cos, sin)


def _proj_a_kernel(x_ref, g_ref, w_ref, c1, s1, c4, s4, c16, s16, p4, p16, o1, o4, o16):
    tm = x_ref.shape[0]
    x = x_ref[...]
    ms = jnp.mean(x * x, axis=-1, keepdims=True)
    h1 = (x * lax.rsqrt(ms + NORM_EPS) * g_ref[...]).astype(BF16)
    lane = lax.broadcasted_iota(jnp.int32, (1, LANES), 1)
    first_half = (lane % HEAD_DIM) < (ROT_DIM // 2)
    ncols = o1.shape[-1]
    for gi, (dil, cos_ref, sin_ref, perm_ref, o_ref) in enumerate(
            ((1, c1, s1, None, o1), (4, c4, s4, p4, o4), (16, c16, s16, p16, o16))):
        n = tm // dil
        h = h1 if perm_ref is None else jnp.dot(perm_ref[...], h1, preferred_element_type=F32).astype(BF16)
        cos = cos_ref[...].reshape(tm, LANES)
        sin = sin_ref[...].reshape(tm, LANES)
        for c0 in range(0, ncols, D_MODEL):
            res = jnp.dot(h, w_ref[:, gi * ncols + c0:gi * ncols + c0 + D_MODEL], preferred_element_type=F32)
            parts = []
            for cc in range(D_MODEL // LANES):
                xc = res[:, cc * LANES:(cc + 1) * LANES]
                if _QKV_ROPE_CHUNKS[c0 // LANES + cc]:
                    xc = _rope_chunk(xc, cos, sin, first_half)
                parts.append(xc.astype(o_ref.dtype))
            val = jnp.concatenate(parts, axis=1)
            for r in range(dil):
                o_ref[r, :, c0:c0 + D_MODEL] = val[r * n:(r + 1) * n]


def _norm_proj_a(x, g, w, tabs, tm=PROJ_TM):
    B, S, D = x.shape
    ncols = 3 * D
    in_specs = [pl.BlockSpec((None, tm, D), lambda b, i: (b, i, 0)),
                pl.BlockSpec((1, D), lambda b, i: (0, 0)),
                pl.BlockSpec((D, 3 * ncols), lambda b, i: (0, 0), pipeline_mode=pl.Buffered(1))]
    args = [x, g.reshape(1, D), w]
    for dil in DILATIONS:
        n = tm // dil
        for tab in tabs[dil]:
            in_specs.append(pl.BlockSpec((None, dil, n, LANES), lambda b, i: (b, 0, i, 0)))
            args.append(tab)
    for dil in DILATIONS[1:]:
        n = tm // dil
        p = jnp.arange(tm)
        src = (p % n) * dil + p // n
        args.append((src[:, None] == jnp.arange(tm)[None, :]).astype(BF16))
        in_specs.append(pl.BlockSpec((tm, tm), lambda b, i: (0, 0)))
    return pl.pallas_call(
        _proj_a_kernel,
        grid=(B, S // tm),
        in_specs=in_specs,
        out_specs=[pl.BlockSpec((None, dil, tm // dil, ncols), lambda b, i: (b, 0, i, 0)) for dil in DILATIONS],
        out_shape=[jax.ShapeDtypeStruct((B, dil, S // dil, ncols), BF16) for dil in DILATIONS],
        compiler_params=_cparams(("parallel", "parallel")),
        name="norm_proj_a",
    )(*args)


def _fill_band_bias(bias_scr, tq, W, half):
    kk = lax.broadcasted_iota(jnp.int32, (W, 2 * tq), 0)
    qq = lax.broadcasted_iota(jnp.int32, (W, 2 * tq), 1) % tq
    for i in range(3):
        ok = jnp.abs(qq + i * half - kk) <= half
        bias_scr[i] = jnp.where(ok, 0.0, -jnp.inf).astype(F32)


def _band_scores(q2, kw, bias):
    lane_lo = lax.broadcasted_iota(jnp.int32, (1, LANES), 1) < HEAD_DIM
    zero = jnp.zeros_like(q2)
    qq = jnp.concatenate([jnp.where(lane_lo, q2, zero), jnp.where(lane_lo, zero, q2)], axis=0)
    qq = qq * jnp.asarray(HEAD_DIM ** -0.5, q2.dtype)
    return lax.dot_general(kw, qq, (((1,), (1,)), ((), ())), preferred_element_type=F32) + bias


def _band_finish(st, vw, want_lse, sink_row=None):
    tq = st.shape[1] // 2
    row_lo = lax.broadcasted_iota(jnp.int32, (LANES, 1), 0) < HEAD_DIM
    m = jnp.max(st, axis=0, keepdims=True)
    if sink_row is not None:
        m = jnp.maximum(m, sink_row)
    p = jnp.exp(st - m)
    l = jnp.sum(p, axis=0, keepdims=True)
    if sink_row is not None:
        l = l + jnp.exp(sink_row - m)
    ot = lax.dot_general(vw, p.astype(BF16), (((0,), (0,)), ((), ())), preferred_element_type=F32) / l
    o = jnp.where(row_lo, ot[:, :tq], ot[:, tq:]).T
    if not want_lse:
        return o, None
    lse_row = m + jnp.log(l)
    lse = jnp.where(row_lo, jnp.broadcast_to(lse_row[:, :tq], (LANES, tq)),
                    jnp.broadcast_to(lse_row[:, tq:], (LANES, tq))).T
    return o, lse


def _attn_a_kernel(*refs, TB, S):
    qkv = refs[:9]
    o_ref = refs[9]
    scr = refs[10:16]
    bias_scr = refs[16]
    t = pl.program_id(2)
    W = 4 * A_HALF

    @pl.when(t == 0)
    def _():
        _fill_band_bias(bias_scr, Q_SUB, W, A_HALF)

    subs = []
    for g, dil in enumerate(DILATIONS):
        nq = TB // dil
        tq = min(nq, Q_SUB)
        for r in range(dil):
            for jj in range(nq // tq):
                subs.append((g, dil, nq, tq, r, jj))

    def scores(sub):
        g, dil, nq, tq, r, jj = sub
        q_ref, k_ref = qkv[3 * g], qkv[3 * g + 1]
        L = S // dil
        qs = t * nq + jj * tq
        ws = pl.multiple_of(jnp.clip(qs - A_HALF, 0, L - W), A_HALF)
        q2 = q_ref[r, jj * tq:(jj + 1) * tq, :]
        return _band_scores(q2, k_ref[r, pl.ds(ws, W), :], bias_scr[(qs - ws) // A_HALF]), ws

    def finish(sub, st, ws):
        g, dil, nq, tq, r, jj = sub
        o_scr, l_scr = scr[2 * g], scr[2 * g + 1]
        o, lse = _band_finish(st, qkv[3 * g + 2][r, pl.ds(ws, W), :], True)
        row0 = jj * (tq * dil) + r
        if dil == 1:
            o_scr[row0:row0 + tq, :] = o
            l_scr[row0:row0 + tq, :] = lse
        else:
            o_scr[pl.ds(row0, tq, stride=dil), :] = o
            l_scr[pl.ds(row0, tq, stride=dil), :] = lse

    pend = [scores(sub) for sub in subs[:A_GROUP]]
    for n, sub in enumerate(subs):
        if n + A_GROUP < len(subs):
            pend.append(scores(subs[n + A_GROUP]))
        st, ws = pend[n]
        finish(sub, st, ws)

    l0, l1, l2 = scr[1][...], scr[3][...], scr[5][...]
    mx = jnp.maximum(jnp.maximum(l0, l1), l2)
    e0, e1, e2 = jnp.exp(l0 - mx), jnp.exp(l1 - mx), jnp.exp(l2 - mx)
    den = e0 + e1 + e2
    o = (e0 * scr[0][...] + e1 * scr[2][...] + e2 * scr[4][...]) / den
    o_ref[...] = o.astype(o_ref.dtype)


def _attn_a(qkvs, B, S, TB=ATTN_A_TB):
    n_pairs = (16 * HEAD_DIM) // LANES
    in_specs, args = [], []
    for g, dil in enumerate(DILATIONS):
        L = S // dil
        in_specs += [
            pl.BlockSpec((None, dil, TB // dil, LANES), lambda b, hp, t: (b, 0, t, hp)),
            pl.BlockSpec((None, dil, L, LANES), lambda b, hp, t: (b, 0, 0, n_pairs + hp)),
            pl.BlockSpec((None, dil, L, LANES), lambda b, hp, t: (b, 0, 0, 2 * n_pairs + hp)),
        ]
        args += [qkvs[g]] * 3
    return pl.pallas_call(
        functools.partial(_attn_a_kernel, TB=TB, S=S),
        grid=(B, n_pairs, S // TB),
        in_specs=in_specs,
        out_specs=pl.BlockSpec((None, TB, LANES), lambda b, hp, t: (b, t, hp)),
        out_shape=jax.ShapeDtypeStruct((B, S, n_pairs * LANES), BF16),
        scratch_shapes=[pltpu.VMEM((TB, LANES), F32)] * 6
        + [pltpu.VMEM((3, Q_SUB + 2 * A_HALF, 2 * Q_SUB), F32)],
        compiler_params=_cparams(("parallel", "parallel", "arbitrary")),
        name="attn_dilated",
    )(*args)


def _attn_c_kernel(q_ref, k_ref, v_ref, sink_ref, o_ref, bias_scr, *, TB, S):
    t = pl.program_id(1)
    tq = Q_SUB
    W = tq + 2 * C_HALF
    n_pairs = q_ref.shape[1] // LANES

    @pl.when(t == 0)
    def _():
        _fill_band_bias(bias_scr, tq, W, C_HALF)

    units = [(jj, p) for jj in range(TB // tq) for p in range(n_pairs)]

    def window(jj):
        qs = t * TB + jj * tq
        ws = pl.multiple_of(jnp.clip(qs - C_HALF, 0, S - W), C_HALF)
        return qs, ws

    def scores(unit):
        jj, p = unit
        qs, ws = window(jj)
        q2 = q_ref[jj * tq:(jj + 1) * tq, p * LANES:(p + 1) * LANES]
        return _band_scores(q2, k_ref[pl.ds(ws, W), :], bias_scr[(qs - ws) // C_HALF])

    pend = [scores(u) for u in units[:C_AHEAD]]
    for n, (jj, p) in enumerate(units):
        if n + C_AHEAD < len(units):
            pend.append(scores(units[n + C_AHEAD]))
        _, ws = window(jj)
        sink_row = sink_ref[p:p + 1, :]
        o, _ = _band_finish(pend[n], v_ref[pl.ds(ws, W), :], False, sink_row=sink_row)
        pend[n] = None
        o_ref[jj * tq:(jj + 1) * tq, p * LANES:(p + 1) * LANES] = o.astype(o_ref.dtype)


def _attn_c(qkv, sink_tab, B, S, TB=ATTN_C_TB):
    nq = 16 * HEAD_DIM
    return pl.pallas_call(
        functools.partial(_attn_c_kernel, TB=TB, S=S),
        grid=(B, S // TB),
        in_specs=[
            pl.BlockSpec((None, TB, nq), lambda b, t: (b, t, 0)),
            pl.BlockSpec((None, S, LANES), lambda b, t: (b, 0, nq // LANES)),
            pl.BlockSpec((None, S, LANES), lambda b, t: (b, 0, nq // LANES + 1)),
            pl.BlockSpec((8, 2 * LANES), lambda b, t: (0, 0)),
        ],
        out_specs=pl.BlockSpec((None, TB, nq), lambda b, t: (b, t, 0)),
        out_shape=jax.ShapeDtypeStruct((B, S, nq), BF16),
        scratch_shapes=[pltpu.VMEM((3, Q_SUB + 2 * C_HALF, 2 * Q_SUB), F32)],
        compiler_params=_cparams(("parallel", "arbitrary")),
        name="attn_swa_sink",
    )(qkv, qkv, qkv, sink_tab)


def _attn_b_kernel(*refs, lambda_init, n_cast):
    lam_ref, q_ref, qn_ref, k_ref, kn_ref, v_ref, subln_ref = refs[:7]
    cast_in = refs[7:7 + n_cast]
    o_ref = refs[7 + n_cast]
    cast_out = refs[8 + n_cast:8 + 2 * n_cast]
    s0_scr, s1_scr, m_scr, l_scr, acc_scr = refs[8 + 2 * n_cast:]
    _attn_b_body(lam_ref, q_ref, qn_ref, k_ref, kn_ref, v_ref, subln_ref, o_ref,
                 s0_scr, s1_scr, m_scr, l_scr, acc_scr, lambda_init=lambda_init)
    for src_ref, dst_ref in zip(cast_in, cast_out):
        dst_ref[...] = src_ref[...].astype(dst_ref.dtype)


def _attn_b_body(lam_ref, q_ref, qn_ref, k_ref, kn_ref, v_ref, subln_ref, o_ref,
                 s0_scr, s1_scr, m_scr, l_scr, acc_scr, *, lambda_init):
    i = pl.program_id(2)
    kv = pl.program_id(3)
    nkv = pl.num_programs(3)
    tk = k_ref.shape[0]
    lane = lax.broadcasted_iota(jnp.int32, (1, LANES), 1)
    lane_lo = lane < HEAD_DIM

    def scores(kref, qref, c, j):
        k = kref[j * CK:(j + 1) * CK, :] * jnp.asarray(HEAD_DIM ** -0.5, kref.dtype)
        sel = lane_lo if c == 0 else jnp.logical_not(lane_lo)
        kc = jnp.where(sel, k, jnp.zeros_like(k))
        return lax.dot_general(kc, qref[...], (((1,), (1,)), ((), ())), preferred_element_type=F32)

    @pl.when(kv == 0)
    def _():
        m_scr[...] = jnp.full(m_scr.shape, -jnp.inf, F32)
        l_scr[...] = jnp.zeros(l_scr.shape, F32)
        acc_scr[...] = jnp.zeros(acc_scr.shape, F32)

    @pl.when(jnp.logical_and(i == 0, kv == 0))
    def _():
        for c in range(2):
            for j in range(tk // CK):
                s0_scr[c * tk + j * CK:c * tk + (j + 1) * CK, :] = scores(k_ref, q_ref, c, j)

    def step(cur_scr, nxt_scr):
        vt = jnp.concatenate([v_ref[...].T, jnp.ones((BF16_SUBLANES, tk), BF16)], axis=0)
        for c in range(2):
            m, l, acc = m_scr[c], l_scr[c], acc_scr[c]
            for j in range(tk // CK):
                rows = slice(c * tk + j * CK, c * tk + (j + 1) * CK)
                nxt_scr[rows, :] = scores(kn_ref, qn_ref, c, j)
                s = cur_scr[rows, :]
                m_new = jnp.maximum(m, jnp.max(s, axis=0, keepdims=True))
                alpha = jnp.exp(m - m_new)
                p = jnp.exp(s - m_new).astype(BF16)
                pv = jnp.dot(vt[:, j * CK:(j + 1) * CK], p, preferred_element_type=F32)
                l = alpha * l + pv[LANES:LANES + 1]
                acc = alpha * acc + pv[:LANES]
                m = m_new
            m_scr[c], l_scr[c], acc_scr[c] = m, l, acc

    @pl.when(kv % 2 == 0)
    def _():
        step(s0_scr, s1_scr)

    @pl.when(kv % 2 == 1)
    def _():
        step(s1_scr, s0_scr)

    @pl.when(kv == nkv - 1)
    def _():
        lp = lam_ref[...]
        lam = (jnp.exp(jnp.sum(lp[0:1] * lp[1:2], axis=-1, keepdims=True))
               - jnp.exp(jnp.sum(lp[2:3] * lp[3:4], axis=-1, keepdims=True)) + lambda_init)
        o = acc_scr[0] / l_scr[0] - lam * (acc_scr[1] / l_scr[1])
        ms = jnp.mean(o * o, axis=0, keepdims=True)
        o = o * lax.rsqrt(ms + SUBLN_EPS) * subln_ref[...] * (1.0 - lambda_init)
        o_ref[...] = o.T.astype(o_ref.dtype)


def _attn_b(qkv, lam_params, subln, lambda_init, B, S, side_casts=(), tq=ATTN_B_TQ, tk=ATTN_B_TK):
    H = 8
    nq, nkv = S // tq, S // tk
    assert nkv % 2 == 0
    nsteps = B * H * nq * nkv
    cast_specs, cast_shapes = [], []
    for w in side_casts:
        rows, cols = w.shape
        nblk = nsteps
        while rows % nblk or (rows // nblk) % BF16_SUBLANES:
            nblk //= 2
        per = nsteps // nblk

        def w_map(b, h, i, kv, per=per):
            return ((((b * H + h) * nq + i) * nkv + kv) // per, 0)

        cast_specs.append(pl.BlockSpec((rows // nblk, cols), w_map))
        cast_shapes.append(jax.ShapeDtypeStruct((rows, cols), BF16))

    def q_next(b, h, i, kv):
        return (b, jnp.minimum(i + (kv + 1) // nkv, nq - 1), h)

    def k_next(b, h, i, kv):
        return (b, (kv + 1) % nkv, H + h)

    outs = pl.pallas_call(
        functools.partial(_attn_b_kernel, lambda_init=lambda_init, n_cast=len(side_casts)),
        grid=(B, H, nq, nkv),
        in_specs=[
            pl.BlockSpec((4, HEAD_DIM), lambda b, h, i, kv: (0, 0)),
            pl.BlockSpec((None, tq, LANES), lambda b, h, i, kv: (b, i, h)),
            pl.BlockSpec((None, tq, LANES), q_next),
            pl.BlockSpec((None, tk, LANES), lambda b, h, i, kv: (b, kv, H + h)),
            pl.BlockSpec((None, tk, LANES), k_next),
            pl.BlockSpec((None, tk, LANES), lambda b, h, i, kv: (b, kv, 2 * H + h)),
            pl.BlockSpec((LANES, 1), lambda b, h, i, kv: (0, 0)),
        ] + cast_specs,
        out_specs=[pl.BlockSpec((None, tq, LANES), lambda b, h, i, kv: (b, i, h))] + cast_specs,
        out_shape=[jax.ShapeDtypeStruct((B, S, H * LANES), BF16)] + cast_shapes,
        scratch_shapes=[pltpu.VMEM((2 * tk, tq), F32), pltpu.VMEM((2 * tk, tq), F32),
                        pltpu.VMEM((2, 1, tq), F32), pltpu.VMEM((2, 1, tq), F32),
                        pltpu.VMEM((2, LANES, tq), F32)],
        compiler_params=_cparams(("parallel", "parallel", "arbitrary", "arbitrary")),
        name="attn_diff",
    )(lam_params, qkv, qkv, qkv, qkv, qkv, subln.reshape(LANES, 1), *side_casts)
    return outs[0], outs[1:]


def _ffn_kernel(x_ref, o_ref, wo_ref, g_ref, w1_ref, w3_ref, w2_ref, y_ref):
    x = x_ref[...] + jnp.dot(o_ref[...], wo_ref[...], preferred_element_type=F32)
    ms = jnp.mean(x * x, axis=-1, keepdims=True)
    h = (x * lax.rsqrt(ms + NORM_EPS) * g_ref[...]).astype(BF16)
    u = jnp.dot(h, w1_ref[...], preferred_element_type=F32)
    v = jnp.dot(h, w3_ref[...], preferred_element_type=F32)
    a = (u * jax.nn.sigmoid(u) * v).astype(BF16)
    y_ref[...] = x + jnp.dot(a, w2_ref[...], preferred_element_type=F32)


def _ffn_dense(x2, o2, wo, g, w1, w3, w2, tm=FFN_TM):
    T, D = x2.shape
    K = o2.shape[1]
    F = w1.shape[1]
    return pl.pallas_call(
        _ffn_kernel,
        grid=(T // tm,),
        in_specs=[pl.BlockSpec((tm, D), lambda i: (i, 0)),
                  pl.BlockSpec((tm, K), lambda i: (i, 0)),
                  pl.BlockSpec((K, D), lambda i: (0, 0)),
                  pl.BlockSpec((1, D), lambda i: (0, 0)),
                  pl.BlockSpec((D, F), lambda i: (0, 0)),
                  pl.BlockSpec((D, F), lambda i: (0, 0)),
                  pl.BlockSpec((F, D), lambda i: (0, 0))],
        out_specs=pl.BlockSpec((tm, D), lambda i: (i, 0)),
        out_shape=jax.ShapeDtypeStruct((T, D), F32),
        compiler_params=_cparams(("parallel",)),
        name="ffn_dense",
    )(x2, o2, wo, g.reshape(1, D), w1, w3, w2)


ROW_TILE = F32_SUBLANES


def _rows_to_tiles(ref, val):
    n = val.shape[0]
    for a in range(ROW_TILE):
        ref[pl.ds(a, n, stride=ROW_TILE), :] = val[:, a * LANES:(a + 1) * LANES]


def _tiles_to_rows(ref, n):
    return jnp.concatenate([ref[pl.ds(a, n, stride=ROW_TILE), :] for a in range(ROW_TILE)], axis=1)


def _router_kernel(x_ref, o_ref, wo_ref, g_ref, r_ref, x1_ref, h_ref, mi_ref, mf_ref, cnt_ref, st_ref, carry_scr):
    i = pl.program_id(0)
    tm = x_ref.shape[0]

    @pl.when(i == 0)
    def _():
        carry_scr[...] = jnp.zeros(carry_scr.shape, F32)

    x = x_ref[...] + jnp.dot(o_ref[...], wo_ref[...], preferred_element_type=F32)
    x1_ref[...] = x
    ms = jnp.mean(x * x, axis=-1, keepdims=True)
    h = x * lax.rsqrt(ms + NORM_EPS) * g_ref[...]
    _rows_to_tiles(h_ref, h)
    lane = lax.broadcasted_iota(jnp.int32, (tm, LANES), 1)
    logits = jnp.full((tm, LANES), -jnp.inf, F32)
    for e in range(N_EXPERTS):
        col = jnp.sum(h * r_ref[e:e + 1, :], axis=-1, keepdims=True)
        logits = jnp.where(lane == e, col, logits)
    v0 = jnp.max(logits, axis=-1, keepdims=True)
    i0 = jnp.min(jnp.where(logits == v0, lane, LANES), axis=-1, keepdims=True)
    rest = jnp.where(lane == i0, -jnp.inf, logits)
    v1 = jnp.max(rest, axis=-1, keepdims=True)
    i1 = jnp.min(jnp.where(rest == v1, lane, LANES), axis=-1, keepdims=True)
    tt = jnp.exp(v1 - v0)
    g0 = 1.0 / (1.0 + tt)
    g1 = tt / (1.0 + tt)
    sel0 = lane == i0
    sel1 = lane == i1
    onehot = jnp.where(jnp.logical_or(sel0, sel1), 1.0, 0.0)
    row = lax.broadcasted_iota(jnp.int32, (tm, tm), 0)
    col = lax.broadcasted_iota(jnp.int32, (tm, tm), 1)
    tri = jnp.where(row > col, 1.0, 0.0).astype(BF16)
    prefix = jnp.dot(tri, onehot.astype(BF16), preferred_element_type=F32)
    before = carry_scr[...] + prefix
    rank0 = jnp.sum(jnp.where(sel0, before, 0.0), axis=-1, keepdims=True)
    rank1 = jnp.sum(jnp.where(sel1, before, 0.0), axis=-1, keepdims=True)
    tile_cnt = jnp.sum(onehot, axis=0, keepdims=True)
    lane1 = lane[0:1]
    cum = jnp.zeros((1, LANES), F32)
    for e in range(N_EXPERTS - 1):
        cum = cum + jnp.where(lane1 > e, tile_cnt[:, e:e + 1], 0.0)
    loc0 = jnp.sum(jnp.where(sel0, cum + prefix, 0.0), axis=-1, keepdims=True)
    loc1 = jnp.sum(jnp.where(sel1, cum + prefix, 0.0), axis=-1, keepdims=True)
    row8 = lax.broadcasted_iota(jnp.int32, (F32_SUBLANES, LANES), 0)
    st_ref[...] = jnp.where(row8 == 0, carry_scr[...], jnp.where(row8 == 1, tile_cnt, jnp.where(row8 == 2, cum, 0.0)))
    carry_scr[...] = carry_scr[...] + tile_cnt
    mi = jnp.where(lane == 0, i0, jnp.where(lane == 1, i1, 0))
    mi = jnp.where(lane == 2, rank0.astype(jnp.int32), jnp.where(lane == 3, rank1.astype(jnp.int32), mi))
    mi = jnp.where(lane == 4, loc0.astype(jnp.int32), jnp.where(lane == 5, loc1.astype(jnp.int32), mi))
    mi_ref[...] = mi
    mf_ref[...] = jnp.where(lane == 0, g0, jnp.where(lane == 1, g1, 0.0))
    cnt_ref[...] = carry_scr[...]


def _router(x2, o2, wo, g, router, tm=MOE_TM):
    T, D = x2.shape
    K = o2.shape[1]
    rt = router.astype(F32).T
    return pl.pallas_call(
        _router_kernel,
        grid=(T // tm,),
        in_specs=[pl.BlockSpec((tm, D), lambda i: (i, 0)),
                  pl.BlockSpec((tm, K), lambda i: (i, 0)),
                  pl.BlockSpec((K, D), lambda i: (0, 0)),
                  pl.BlockSpec((1, D), lambda i: (0, 0)),
                  pl.BlockSpec((N_EXPERTS, D), lambda i: (0, 0))],
        out_specs=[pl.BlockSpec((tm, D), lambda i: (i, 0)),
                   pl.BlockSpec((tm * ROW_TILE, LANES), lambda i: (i, 0)),
                   pl.BlockSpec((tm, LANES), lambda i: (i, 0)),
                   pl.BlockSpec((tm, LANES), lambda i: (i, 0)),
                   pl.BlockSpec((1, LANES), lambda i: (0, 0)),
                   pl.BlockSpec((F32_SUBLANES, LANES), lambda i: (i, 0))],
        out_shape=[jax.ShapeDtypeStruct((T, D), F32),
                   jax.ShapeDtypeStruct((T * ROW_TILE, LANES), F32),
                   jax.ShapeDtypeStruct((T, LANES), jnp.int32),
                   jax.ShapeDtypeStruct((T, LANES), F32),
                   jax.ShapeDtypeStruct((1, LANES), F32),
                   jax.ShapeDtypeStruct((T // tm * F32_SUBLANES, LANES), F32)],
        scratch_shapes=[pltpu.VMEM((1, LANES), F32)],
        compiler_params=_cparams(("arbitrary",)),
        name="moe_router",
    )(x2, o2, wo, g.reshape(1, D), rt)


INFO_LEN = 4 * MOE_TM
SIZE_BITS = tuple(range(MOE_TM.bit_length() - 1, -1, -1))


def _range_copies(iref, tm, hbm_ref, vmem_ref, sem, to_hbm, wait):
    for e in range(N_EXPERTS):
        start = iref[2 * tm + e]
        cnt = iref[2 * tm + N_EXPERTS + e]
        cum = iref[2 * tm + 2 * N_EXPERTS + e]
        for bit in SIZE_BITS:
            size = 1 << bit

            @pl.when((cnt >> bit) & 1 == 1)
            def _(bit=bit, size=size, start=start, cnt=cnt, cum=cum):
                done = (cnt >> (bit + 1)) << (bit + 1)
                far = hbm_ref.at[pl.ds(pl.multiple_of((start + done) * ROW_TILE, ROW_TILE), size * ROW_TILE)]
                near = vmem_ref.at[pl.ds(pl.multiple_of((cum + done) * ROW_TILE, ROW_TILE), size * ROW_TILE)]
                cp = pltpu.make_async_copy(near, far, sem) if to_hbm else pltpu.make_async_copy(far, near, sem)
                cp.wait() if wait else cp.start()


def _dispatch_kernel(info_ref, iprev_ref, h_ref, buf_in, buf_hbm, comp_scr, sems, *, tm):
    del buf_in
    i = pl.program_id(0)
    last = pl.num_programs(0) - 1

    def tile(par):
        def place(n, carry):
            row = h_ref[pl.ds(pl.multiple_of(n * ROW_TILE, ROW_TILE), ROW_TILE), :]
            comp_scr[par, pl.ds(pl.multiple_of(info_ref[2 * n] * ROW_TILE, ROW_TILE), ROW_TILE), :] = row
            comp_scr[par, pl.ds(pl.multiple_of(info_ref[2 * n + 1] * ROW_TILE, ROW_TILE), ROW_TILE), :] = row
            return carry

        lax.fori_loop(0, tm, place, 0, unroll=2)
        _range_copies(info_ref, tm, buf_hbm, comp_scr.at[par], sems.at[par], True, False)

        @pl.when(i >= 1)
        def _():
            _range_copies(iprev_ref, tm, buf_hbm, comp_scr.at[1 - par], sems.at[1 - par], True, True)

        @pl.when(i == last)
        def _():
            _range_copies(info_ref, tm, buf_hbm, comp_scr.at[par], sems.at[par], True, True)

    for par in range(2):
        @pl.when(i % 2 == par)
        def _(par=par):
            tile(par)


def _dispatch(h2, info, n_rows, buf0, tm=MOE_TM):
    T = h2.shape[0] // ROW_TILE
    return pl.pallas_call(
        functools.partial(_dispatch_kernel, tm=tm),
        grid=(T // tm,),
        in_specs=[pl.BlockSpec((INFO_LEN,), lambda i: (i,), memory_space=pltpu.SMEM),
                  pl.BlockSpec((INFO_LEN,), lambda i: (jnp.maximum(i - 1, 0),), memory_space=pltpu.SMEM),
                  pl.BlockSpec((tm * ROW_TILE, LANES), lambda i: (i, 0)),
                  pl.BlockSpec(memory_space=pl.ANY)],
        out_specs=pl.BlockSpec(memory_space=pl.ANY),
        out_shape=jax.ShapeDtypeStruct((n_rows * ROW_TILE, LANES), F32),
        scratch_shapes=[pltpu.VMEM((2, 2 * tm * ROW_TILE, LANES), F32), pltpu.SemaphoreType.DMA((2,))],
        input_output_aliases={3: 0},
        compiler_params=_cparams(("arbitrary",)),
        name="moe_dispatch",
    )(info, info, h2, buf0)


def _moe_ffn_kernel(be_ref, nu_ref, x_ref, w1_ref, w3_ref, w2_ref, y_ref, acc_scr):
    b = pl.program_id(0)
    f = pl.program_id(1)

    @pl.when(b < nu_ref[0])
    def _():
        @pl.when(f == 0)
        def _():
            acc_scr[...] = jnp.zeros(acc_scr.shape, F32)

        h = _tiles_to_rows(x_ref, MOE_BLOCK).astype(BF16)
        u = jnp.dot(h, w1_ref[...], preferred_element_type=F32)
        v = jnp.dot(h, w3_ref[...], preferred_element_type=F32)
        a = (u * jax.nn.sigmoid(u) * v).astype(BF16)
        acc_scr[...] += jnp.dot(a, w2_ref[...], preferred_element_type=F32)

        @pl.when(f == pl.num_programs(1) - 1)
        def _():
            _rows_to_tiles(y_ref, acc_scr[...])

    @pl.when(jnp.logical_and(b >= nu_ref[0], f == 0))
    def _():
        y_ref[...] = jnp.zeros(y_ref.shape, F32)


def _moe_ffn(buf, block_expert, n_used, w1, w3, w2, tf=MOE_TF):
    R = buf.shape[0] // ROW_TILE
    E, D, F = w1.shape
    nb = R // MOE_BLOCK

    def last_live(b, nu):
        return jnp.maximum(jnp.minimum(b, nu[0] - 1), 0)

    def row_map(b, f, be, nu):
        return (last_live(b, nu), 0)

    def w_in_map(b, f, be, nu):
        live = b < nu[0]
        return (be[last_live(b, nu)], 0, jnp.where(live, f, F // tf - 1))

    def w_out_map(b, f, be, nu):
        live = b < nu[0]
        return (be[last_live(b, nu)], jnp.where(live, f, F // tf - 1), 0)

    return pl.pallas_call(
        _moe_ffn_kernel,
        grid_spec=pltpu.PrefetchScalarGridSpec(
            num_scalar_prefetch=2,
            grid=(nb, F // tf),
            in_specs=[pl.BlockSpec((MOE_BLOCK * ROW_TILE, LANES), row_map),
                      pl.BlockSpec((None, D, tf), w_in_map),
                      pl.BlockSpec((None, D, tf), w_in_map),
                      pl.BlockSpec((None, tf, D), w_out_map)],
            out_specs=pl.BlockSpec((MOE_BLOCK * ROW_TILE, LANES), lambda b, f, be, nu: (b, 0)),
            scratch_shapes=[pltpu.VMEM((MOE_BLOCK, D), F32)],
        ),
        out_shape=jax.ShapeDtypeStruct((R * ROW_TILE, LANES), F32),
        compiler_params=_cparams(("arbitrary", "arbitrary")),
        name="moe_ffn",
    )(block_expert, n_used, buf, w1, w3, w2)


def _combine_kernel(info_ref, inext_ref, x_ref, mf_ref, g_ref, ybuf_hbm, y_ref, comp_scr, r0_scr, r1_scr, sems,
                    *, tm, final_norm):
    i = pl.program_id(0)
    last = pl.num_programs(0) - 1

    def ranges(iref, par, wait):
        _range_copies(iref, tm, ybuf_hbm, comp_scr.at[par], sems.at[par], False, wait)

    def finish(par):
        ranges(info_ref, par, True)

        def pick(n, carry):
            rows = pl.ds(pl.multiple_of(n * ROW_TILE, ROW_TILE), ROW_TILE)
            j0 = pl.ds(pl.multiple_of(info_ref[2 * n] * ROW_TILE, ROW_TILE), ROW_TILE)
            j1 = pl.ds(pl.multiple_of(info_ref[2 * n + 1] * ROW_TILE, ROW_TILE), ROW_TILE)
            r0_scr[rows, :] = comp_scr[par, j0, :]
            r1_scr[rows, :] = comp_scr[par, j1, :]
            return carry

        lax.fori_loop(0, tm, pick, 0, unroll=8)
        mf = mf_ref[...]
        y = x_ref[...] + mf[:, 0:1] * _tiles_to_rows(r0_scr, tm) + mf[:, 1:2] * _tiles_to_rows(r1_scr, tm)
        if final_norm:
            ms = jnp.mean(y * y, axis=-1, keepdims=True)
            y = y * lax.rsqrt(ms + NORM_EPS) * g_ref[...]
        y_ref[...] = y

    @pl.when(i == 0)
    def _():
        ranges(info_ref, 0, False)

    for par in range(2):
        @pl.when(jnp.logical_and(i % 2 == par, i < last))
        def _(par=par):
            ranges(inext_ref, 1 - par, False)

        @pl.when(i % 2 == par)
        def _(par=par):
            finish(par)


def _combine(x2, mf, info, ybuf, g_final, final_norm, tm=MOE_TM):
    T, D = x2.shape
    nt = T // tm
    return pl.pallas_call(
        functools.partial(_combine_kernel, tm=tm, final_norm=final_norm),
        grid=(nt,),
        in_specs=[pl.BlockSpec((INFO_LEN,), lambda i: (i,), memory_space=pltpu.SMEM),
                  pl.BlockSpec((INFO_LEN,), lambda i: (jnp.minimum(i + 1, nt - 1),), memory_space=pltpu.SMEM),
                  pl.BlockSpec((tm, D), lambda i: (i, 0)),
                  pl.BlockSpec((tm, LANES), lambda i: (i, 0)),
                  pl.BlockSpec((1, D), lambda i: (0, 0)),
                  pl.BlockSpec(memory_space=pl.ANY)],
        out_specs=pl.BlockSpec((tm, D), lambda i: (i, 0)),
        out_shape=jax.ShapeDtypeStruct((T, D), F32),
        scratch_shapes=[pltpu.VMEM((2, 2 * tm * ROW_TILE, LANES), F32),
                        pltpu.VMEM((tm * ROW_TILE, LANES), F32), pltpu.VMEM((tm * ROW_TILE, LANES), F32),
                        pltpu.SemaphoreType.DMA((2,))],
        compiler_params=_cparams(("arbitrary",)),
        name="moe_combine",
    )(info, info, x2, mf, g_final.reshape(1, D), ybuf)


def _moe_layer(x2, o2, wo, g, router, w1, w3, w2, g_final, final_norm, buf0=None):
    T, D = x2.shape
    x2, h2, mi, mf, cnt, stats = _router(x2, o2, wo, g, router)
    counts = cnt[0, :N_EXPERTS].astype(jnp.int32)
    padded = ((counts + MOE_BLOCK - 1) // MOE_BLOCK) * MOE_BLOCK
    pend = jnp.cumsum(padded)
    pstart = pend - padded
    nb = -(-(2 * T + N_EXPERTS * (MOE_BLOCK - 1)) // MOE_BLOCK)
    block_start = jnp.arange(nb, dtype=jnp.int32) * MOE_BLOCK
    block_expert = jnp.minimum(jnp.sum(pend[None, :] <= block_start[:, None], axis=1),
                               N_EXPERTS - 1).astype(jnp.int32)
    n_used = (pend[-1:] // MOE_BLOCK).astype(jnp.int32)
    if buf0 is None:
        buf0 = jnp.zeros((nb * MOE_BLOCK * ROW_TILE, LANES), F32)
    nt = T // MOE_TM
    st = stats.reshape(nt, F32_SUBLANES, LANES)[:, :3, :N_EXPERTS].astype(jnp.int32)
    info = jnp.concatenate([mi[:, 4:6].reshape(nt, 2 * MOE_TM), pstart[None, :] + st[:, 0], st[:, 1], st[:, 2],
                            jnp.zeros((nt, INFO_LEN - 2 * MOE_TM - 3 * N_EXPERTS), jnp.int32)], axis=1).reshape(-1)
    buf = _dispatch(h2, info, nb * MOE_BLOCK, buf0)
    ybuf = _moe_ffn(buf, block_expert, n_used, w1, w3, w2)
    return _combine(x2, mf, info, ybuf, g_final, final_norm), ybuf


def _mixer_a(x, g, w_in, w_out, tabs):
    B, S, D = x.shape
    w = w_in.astype(BF16)
    qkvs = _norm_proj_a(x, g, w, tabs)
    o = _attn_a(qkvs, B, S)
    return o.reshape(B * S, D), w_out.astype(BF16)


def _mixer_b(x, g, w_in, lq1, lk1, lq2, lk2, subln, w_out, lambda_init, tabs, side_casts=()):
    B, S, D = x.shape
    cos, sin = tabs[1]
    qkv = _norm_proj(x, g, w_in.astype(BF16), cos, sin, tc=D, rope_chunks=_QKV_ROPE_CHUNKS)
    lam_params = jnp.stack([lq1, lk1, lq2, lk2]).astype(F32)
    o, casted = _attn_b(qkv, lam_params, subln.astype(F32), lambda_init, B, S, side_casts)
    return o.reshape(B * S, D), w_out.astype(BF16), casted


def _mixer_c(x, g, w_in, sink, w_out, tabs):
    B, S, D = x.shape
    cos, sin = tabs[1]
    head_order = jnp.arange(16).reshape(2, 8).T.reshape(-1)
    col_order = (head_order[:, None] * HEAD_DIM + jnp.arange(HEAD_DIM)[None, :]).reshape(-1)
    w_q = w_in[:, :D][:, col_order]
    w = jnp.concatenate([w_q, w_in[:, D:]], axis=1).astype(BF16)
    ncols = w.shape[1]
    chunks = tuple(c < (ncols // LANES - 1) for c in range(ncols // LANES))
    qkv = _norm_proj(x, g, w, cos, sin, tc=ncols, rope_chunks=chunks)
    sk = sink.astype(F32)
    sink_tab = jnp.concatenate([jnp.broadcast_to(sk[:8, None], (8, LANES)),
                                jnp.broadcast_to(sk[8:, None], (8, LANES))], axis=1)
    o = _attn_c(qkv, sink_tab, B, S)
    w_o = w_out[col_order, :].astype(BF16)
    return o.reshape(B * S, D), w_o


def kernel(x, positions, l0_norm_mix, l0_a_w_in, l0_a_w_out, l0_norm_ffn, l0_ffn_w1, l0_ffn_w3, l0_ffn_w2, l1_norm_mix, l1_b_w_in, l1_b_lambda_q1, l1_b_lambda_k1, l1_b_lambda_q2, l1_b_lambda_k2, l1_b_subln, l1_b_w_out, l1_norm_ffn, l1_moe_router, l1_moe_w1, l1_moe_w3, l1_moe_w2, l2_norm_mix, l2_c_w_in, l2_c_sink, l2_c_w_out, l2_norm_ffn, l2_ffn_w1, l2_ffn_w3, l2_ffn_w2, l3_norm_mix, l3_a_w_in, l3_a_w_out, l3_norm_ffn, l3_moe_router, l3_moe_w1, l3_moe_w3, l3_moe_w2, final_norm):
    B, S, D = x.shape
    T = B * S
    cos, sin = lax.optimization_barrier(_rope_tables(positions))
    tabs = {d: (_to_strided(cos, d), _to_strided(sin, d)) for d in DILATIONS}

    o, wo = _mixer_a(x, l0_norm_mix, l0_a_w_in, l0_a_w_out, tabs)
    x = _ffn_dense(x.reshape(T, D), o, wo, l0_norm_ffn, l0_ffn_w1.astype(BF16), l0_ffn_w3.astype(BF16),
                   l0_ffn_w2.astype(BF16)).reshape(B, S, D)
    lambda_init = 0.8 - 0.6 * math.exp(-0.3 * 1)
    moe_w = (l1_moe_w1, l1_moe_w3, l1_moe_w2, l3_moe_w1, l3_moe_w3, l3_moe_w2)
    o, wo, moe_bf = _mixer_b(x, l1_norm_mix, l1_b_w_in, l1_b_lambda_q1, l1_b_lambda_k1, l1_b_lambda_q2,
                             l1_b_lambda_k2, l1_b_subln, l1_b_w_out, lambda_init, tabs,
                             side_casts=tuple(w.reshape(-1, w.shape[-1]) for w in moe_w))
    moe_bf = [wb.reshape(w.shape) for wb, w in zip(moe_bf, moe_w)]
    x, spare = _moe_layer(x.reshape(T, D), o, wo, l1_norm_ffn, l1_moe_router, moe_bf[0], moe_bf[1], moe_bf[2],
                          final_norm, False)
    x = x.reshape(B, S, D)
    o, wo = _mixer_c(x, l2_norm_mix, l2_c_w_in, l2_c_sink, l2_c_w_out, tabs)
    x = _ffn_dense(x.reshape(T, D), o, wo, l2_norm_ffn, l2_ffn_w1.astype(BF16), l2_ffn_w3.astype(BF16),
                   l2_ffn_w2.astype(BF16)).reshape(B, S, D)
    o, wo = _mixer_a(x, l3_norm_mix, l3_a_w_in, l3_a_w_out, tabs)
    x, _ = _moe_layer(x.reshape(T, D), o, wo, l3_norm_ffn, l3_moe_router, moe_bf[3], moe_bf[4], moe_bf[5],
                      final_norm, True, buf0=spare)
    return x.reshape(B, S, D)
```
